```python
import jax, jax.numpy as jnp
from jax import lax
import numpy as np

D_MODEL = 1024
BATCH = 4
SEQ = 4096
DEPTH = 4

N_MIXERS = 2
N_CONV_LAYERS = (DEPTH + N_MIXERS - 1) // N_MIXERS
N_RWKV_LAYERS = DEPTH // N_MIXERS
CONV_WIDTH = 3
RWKV_HEAD_DIM = 64
RWKV_HEADS = D_MODEL // RWKV_HEAD_DIM
DECAY_LORA = 64
ICLR_LORA = 64
GATE_LORA = 128
N_DIRECTIONS = 2
N_TOKEN_SHIFT_MIX = 6
GN_EPS = 64e-5
MEM_LEN = 256
XATTN_HEADS = 4
XATTN_HEAD_DIM = D_MODEL // XATTN_HEADS
N_EXPERTS = 16
CAPACITY_FACTOR = 2
EXPERT_D_FF = 1024
RMS_EPS = 1e-6

kernel_name = "hybrid_conv_rwkv7_ecmoe_encoder"


def rms_norm(x, g):
    xf = x.astype(jnp.float32)
    y = xf * lax.rsqrt(jnp.mean(xf * xf, axis=-1, keepdims=True) + RMS_EPS)
    return (y * g.astype(jnp.float32)).astype(x.dtype)


def short_conv_mixer(xn, w_in, conv_w, w_out):
    s = xn.shape[1]
    b_gate, c_gate, hx = jnp.split(xn @ w_in, 3, axis=-1)
    u = c_gate * hx
    up = jnp.pad(u, ((0, 0), (1, 1), (0, 0)))
    conv = up[:, :s] * conv_w[0] + up[:, 1:s + 1] * conv_w[1] + up[:, 2:] * conv_w[2]
    return (b_gate * conv) @ w_out


def wkv_scan(r, w, k, v, kk, a, reverse):
    b, s, h, n = r.shape
    seqs = tuple(jnp.moveaxis(t, 1, 0) for t in (r, w, k, v, kk, kk * a))

    def step(state, inp):
        r_t, w_t, k_t, v_t, kk_t, b_t = inp
        sa = jnp.einsum('bhvk,bhk->bhv', state, kk_t)
        state = (state * w_t[:, :, None, :]
                 - sa[..., None] * b_t[:, :, None, :]
                 + v_t[..., None] * k_t[:, :, None, :])
        y_t = jnp.einsum('bhvk,bhk->bhv', state, r_t)
        return state, y_t

    state0 = jnp.zeros((b, h, n, n), jnp.float32)
    _, ys = lax.scan(step, state0, seqs, reverse=reverse)
    return jnp.moveaxis(ys, 0, 1)


def rwkv7_bidirectional(xn, mu, w_rkv, w0, w1, w2, a0, a1, a2, g1, g2,
                        k_k, k_a, r_k, ln_w, ln_b, w_out):
    b, s, d = xn.shape
    h, n = RWKV_HEADS, RWKV_HEAD_DIM
    f32 = jnp.float32
    xp = jnp.pad(xn, ((0, 0), (1, 1), (0, 0)))
    xx = 0.5 * (xp[:, :s] + xp[:, 2:]) - xn
    x_rkv = xn[None] + xx[None] * mu[:3, None, None, :]
    rkv = jnp.einsum('nbsd,nde->nbse', x_rkv, w_rkv)
    r, k, v = rkv[0], rkv[1], rkv[2]
    xw = xn + xx * mu[3]
    xa = xn + xx * mu[4]
    xg = xn + xx * mu[5]
    w_raw = w0[:, None, None, :] + jnp.einsum(
        'nbsl,nld->nbsd', jnp.tanh(jnp.einsum('bsd,ndl->nbsl', xw, w1)), w2)
    log_w = -jax.nn.softplus(-w_raw.astype(f32)) - 0.5
    decay = jnp.exp(-jnp.exp(log_w))
    iclr = jax.nn.sigmoid((a0[:, None, None, :] + jnp.einsum(
        'nbsl,nld->nbsd', jnp.einsum('bsd,ndl->nbsl', xa, a1), a2)).astype(f32))
    gate = jax.nn.sigmoid(xg @ g1) @ g2
    kk = (k * k_k).astype(f32).reshape(b, s, h, n)
    kk = kk / jnp.maximum(jnp.linalg.norm(kk, axis=-1, keepdims=True), 1e-12)
    k_dir = k.astype(f32)[None] * (1.0 + (iclr - 1.0) * k_a.astype(f32))

    def heads(t):
        return t.reshape(t.shape[:-1] + (h, n))

    r_h, v_h = heads(r.astype(f32)), heads(v.astype(f32))
    k_h, w_h, a_h = heads(k_dir), heads(decay), heads(iclr)
    y = (wkv_scan(r_h, w_h[0], k_h[0], v_h, kk, a_h[0], False)
         + wkv_scan(r_h, w_h[1], k_h[1], v_h, kk, a_h[1], True))
    mean = jnp.mean(y, axis=-1, keepdims=True)
    var = jnp.mean(jnp.square(y - mean), axis=-1, keepdims=True)
    y = ((y - mean) * lax.rsqrt(var + GN_EPS)).reshape(b, s, d)
    y = y * ln_w.astype(f32) + ln_b.astype(f32)
    bonus = jnp.sum(r_h[None] * k_h * r_k.astype(f32), axis=(0, -1))[..., None] * v_h
    y = y + bonus.reshape(b, s, d)
    return (y.astype(xn.dtype) * gate) @ w_out


def memory_cross_attention(xn, memn, w_q, w_kv, w_o):
    b, s, d = xn.shape
    m = memn.shape[1]
    q = (xn @ w_q).reshape(b, s, XATTN_HEADS, XATTN_HEAD_DIM)
    kv = (memn @ w_kv).reshape(b, m, 2, XATTN_HEADS, XATTN_HEAD_DIM)
    k, v = kv[:, :, 0], kv[:, :, 1]
    scores = jnp.einsum('bshd,bmhd->bhsm', q, k).astype(jnp.float32) * XATTN_HEAD_DIM ** -0.5
    p = jax.nn.softmax(scores, axis=-1).astype(v.dtype)
    o = jnp.einsum('bhsm,bmhd->bshd', p, v).reshape(b, s, d)
    return o @ w_o


def expert_choice_moe(xn, router, w_gate, w_up, w_down):
    b, s, d = xn.shape
    cap = CAPACITY_FACTOR * s // N_EXPERTS
    probs = jax.nn.softmax((xn @ router).astype(jnp.float32), axis=-1)
    aff, idx = lax.top_k(jnp.swapaxes(probs, 1, 2), cap)
    bidx = jnp.arange(b)[:, None, None]
    xe = xn[bidx, idx]
    hid = jax.nn.silu(jnp.einsum('becd,edf->becf', xe, w_gate)) * jnp.einsum('becd,edf->becf', xe, w_up)
    ye = jnp.einsum('becf,efd->becd', hid, w_down) * aff[..., None].astype(xn.dtype)
    return jnp.zeros_like(xn).at[bidx, idx].add(ye)


def setup_inputs(seed: int = 0) -> dict:
    key = jax.random.key(seed)
    keys = iter(jax.random.split(key, 48))
    D, NC, NR, ND = D_MODEL, N_CONV_LAYERS, N_RWKV_LAYERS, N_DIRECTIONS
    E, F = N_EXPERTS, EXPERT_D_FF

    def nrm(shape, scale):
        return jax.random.normal(next(keys), shape, jnp.float32) * scale

    def gain(shape):
        return 1.0 + nrm(shape, 0.02)

    return {
        "x": nrm((BATCH, SEQ, D), 1.0),
        "mem": nrm((BATCH, MEM_LEN, D), 1.0),
        "norm_mix": gain((DEPTH, D)),
        "norm_xattn": gain((DEPTH, D)),
        "norm_mem": gain((DEPTH, D)),
        "norm_ffn": gain((DEPTH, D)),
        "norm_final": gain((D,)),
        "conv_w_in": nrm((NC, D, 3 * D), D ** -0.5),
        "conv_w": nrm((NC, CONV_WIDTH, D), 0.5),
        "conv_w_out": nrm((NC, D, D), D ** -0.5),
        "rwkv_mu": jax.random.uniform(next(keys), (NR, N_TOKEN_SHIFT_MIX, D), jnp.float32),
        "rwkv_w_rkv": nrm((NR, 3, D, D), D ** -0.5),
        "rwkv_w0": nrm((NR, ND, D), 1.0),
        "rwkv_w1": nrm((NR, ND, D, DECAY_LORA), D ** -0.5),
        "rwkv_w2": nrm((NR, ND, DECAY_LORA, D), 0.5 * DECAY_LORA ** -0.5),
        "rwkv_a0": nrm((NR, ND, D), 0.5),
        "rwkv_a1": nrm((NR, ND, D, ICLR_LORA), D ** -0.5),
        "rwkv_a2": nrm((NR, ND, ICLR_LORA, D), 0.5 * ICLR_LORA ** -0.5),
        "rwkv_g1": nrm((NR, D, GATE_LORA), D ** -0.5),
        "rwkv_g2": nrm((NR, GATE_LORA, D), GATE_LORA ** -0.5),
        "rwkv_k_k": 0.85 + nrm((NR, D), 0.05),
        "rwkv_k_a": 1.0 + nrm((NR, D), 0.05),
        "rwkv_r_k": nrm((NR, RWKV_HEADS, RWKV_HEAD_DIM), 0.1),
        "rwkv_ln_w": gain((NR, D)),
        "rwkv_ln_b": nrm((NR, D), 0.02),
        "rwkv_w_out": nrm((NR, D, D), D ** -0.5),
        "xattn_w_q": nrm((DEPTH, D, D), D ** -0.5),
        "xattn_w_kv": nrm((DEPTH, D, 2 * D), D ** -0.5),
        "xattn_w_o": nrm((DEPTH, D, D), D ** -0.5),
        "moe_router": nrm((DEPTH, D, E), D ** -0.5),
        "moe_w_gate": nrm((DEPTH, E, D, F), D ** -0.5),
        "moe_w_up": nrm((DEPTH, E, D, F), D ** -0.5),
        "moe_w_down": nrm((DEPTH, E, F, D), F ** -0.5),
    }


def reference(x, mem, norm_mix, norm_xattn, norm_mem, norm_ffn, norm_final,
              conv_w_in, conv_w, conv_w_out,
              rwkv_mu, rwkv_w_rkv, rwkv_w0, rwkv_w1, rwkv_w2, rwkv_a0, rwkv_a1, rwkv_a2,
              rwkv_g1, rwkv_g2, rwkv_k_k, rwkv_k_a, rwkv_r_k, rwkv_ln_w, rwkv_ln_b, rwkv_w_out,
              xattn_w_q, xattn_w_kv, xattn_w_o,
              moe_router, moe_w_gate, moe_w_up, moe_w_down):
    h = x
    for i in range(DEPTH):
        j = i // N_MIXERS
        xn = rms_norm(h, norm_mix[i])
        if i % N_MIXERS == 0:
            h = h + short_conv_mixer(xn, conv_w_in[j], conv_w[j], conv_w_out[j])
        else:
            h = h + rwkv7_bidirectional(
                xn, rwkv_mu[j], rwkv_w_rkv[j], rwkv_w0[j], rwkv_w1[j], rwkv_w2[j],
                rwkv_a0[j], rwkv_a1[j], rwkv_a2[j], rwkv_g1[j], rwkv_g2[j],
                rwkv_k_k[j], rwkv_k_a[j], rwkv_r_k[j], rwkv_ln_w[j], rwkv_ln_b[j], rwkv_w_out[j])
        h = h + memory_cross_attention(rms_norm(h, norm_xattn[i]), rms_norm(mem, norm_mem[i]),
                                       xattn_w_q[i], xattn_w_kv[i], xattn_w_o[i])
        h = h + expert_choice_moe(rms_norm(h, norm_ffn[i]), moe_router[i],
                                  moe_w_gate[i], moe_w_up[i], moe_w_down[i])
    return rms_norm(h, norm_final)
```

```python
import functools

import jax
import jax.numpy as jnp
from jax import lax
from jax.experimental import pallas as pl
from jax.experimental.pallas import tpu as pltpu

F32 = jnp.float32
BF16 = jnp.bfloat16
I32 = jnp.int32

N_MIXERS = 2
RWKV_HEAD_DIM = 64
XATTN_HEADS = 4
CAPACITY_FACTOR = 2
GN_EPS = 64e-5
RMS_EPS = 1e-6
DECAY_SCALE = 0.6065306597126334

V7X_VMEM_LIMIT_BYTES = 56 * 1024 * 1024
LANES = 128
BF16_SUBLANES = 16

ROW_TILE = 512
RWKV_ROW_TILE = 256
HALO = BF16_SUBLANES
CHUNK = 64
CHUNK_GROUP = 4
TOKEN_BLOCK = 512


def _params(*semantics):
    return pltpu.CompilerParams(dimension_semantics=semantics,
                                vmem_limit_bytes=V7X_VMEM_LIMIT_BYTES)


def _rms(x, g):
    return x * lax.rsqrt(jnp.mean(x * x, axis=-1, keepdims=True) + RMS_EPS) * g


def _dot(a, b):
    return jnp.dot(a, b, preferred_element_type=F32)


def _dot_nt(a, b):
    return lax.dot_general(a, b, (((1,), (1,)), ((), ())), preferred_element_type=F32)


def _dot_tn(a, b):
    return lax.dot_general(a, b, (((0,), (0,)), ((), ())), preferred_element_type=F32)


def _full(shape):
    n = len(shape)
    return pl.BlockSpec(shape, lambda *_: (0,) * n)


def _conv_kernel(h_ref, hp_ref, hn_ref, g_ref, win_ref, cw_ref, wout_ref, o_ref,
                 xn_s, u_s, gate_s, *, tm, d, cb):
    i = pl.program_id(1)
    g = g_ref[...]
    x = h_ref[0]
    xn_s[0:HALO, :] = _rms(hp_ref[0], g).astype(BF16)
    xn_s[HALO:HALO + tm, :] = _rms(x, g).astype(BF16)
    xn_s[HALO + tm:, :] = _rms(hn_ref[0], g).astype(BF16)
    rows = tm + 2 * HALO
    row = lax.broadcasted_iota(I32, (rows, 1), 0)
    lo = jnp.where(i == 0, HALO, 0)
    hi = jnp.where(i == pl.num_programs(1) - 1, HALO + tm, rows)
    pad = jnp.logical_or(row < lo, row >= hi)
    xa = xn_s[...]
    for c0 in range(0, d, cb):
        c_gate = _dot(xa, win_ref[:, d + c0:d + c0 + cb])
        hx = _dot(xa, win_ref[:, 2 * d + c0:2 * d + c0 + cb])
        u_s[...] = jnp.where(pad, 0.0, c_gate * hx)
        conv = (u_s[pl.ds(HALO - 1, tm), :] * cw_ref[0:1, c0:c0 + cb]
                + u_s[pl.ds(HALO, tm), :] * cw_ref[1:2, c0:c0 + cb]
                + u_s[pl.ds(HALO + 1, tm), :] * cw_ref[2:3, c0:c0 + cb])
        b_gate = _dot(xn_s[HALO:HALO + tm, :], win_ref[:, c0:c0 + cb])
        gate_s[:, c0:c0 + cb] = (b_gate * conv).astype(BF16)
    o_ref[0] = x + _dot(gate_s[...], wout_ref[...])


def _halo_specs(tm, s, d):
    nb = tm // HALO
    last = s // HALO - 1
    return [
        pl.BlockSpec((1, tm, d), lambda b, i: (b, i, 0)),
        pl.BlockSpec((1, HALO, d), lambda b, i: (b, jnp.maximum(i * nb - 1, 0), 0)),
        pl.BlockSpec((1, HALO, d), lambda b, i: (b, jnp.minimum((i + 1) * nb, last), 0)),
    ]


def _conv_layer(h, g, w_in, conv_w, w_out):
    bsz, s, d = h.shape
    tm = min(ROW_TILE, s)
    cb = 512
    kern = functools.partial(_conv_kernel, tm=tm, d=d, cb=cb)
    return pl.pallas_call(
        kern,
        grid=(bsz, s // tm),
        in_specs=_halo_specs(tm, s, d) + [
            _full((1, d)), _full((d, 3 * d)), _full((3, d)), _full((d, d))],
        out_specs=pl.BlockSpec((1, tm, d), lambda b, i: (b, i, 0)),
        out_shape=jax.ShapeDtypeStruct((bsz, s, d), F32),
        scratch_shapes=[pltpu.VMEM((tm + 2 * HALO, d), BF16),
                        pltpu.VMEM((tm + 2 * HALO, cb), F32),
                        pltpu.VMEM((tm, d), BF16)],
        compiler_params=_params("parallel", "parallel"),
        name="conv_mixer",
    )(h, h, h, g.reshape(1, d), w_in.astype(BF16), conv_w, w_out.astype(BF16))


def _kv_kernel(m_ref, g_ref, w_ref, o_ref):
    xn = _rms(m_ref[0], g_ref[...]).astype(BF16)
    o_ref[0] = _dot(xn, w_ref[...]).astype(BF16)


def _kv_proj(mem, g, w_kv):
    bsz, m, d = mem.shape
    return pl.pallas_call(
        _kv_kernel,
        grid=(bsz,),
        in_specs=[pl.BlockSpec((1, m, d), lambda b: (b, 0, 0)), _full((1, d)), _full((d, 2 * d))],
        out_specs=pl.BlockSpec((1, m, 2 * d), lambda b: (b, 0, 0)),
        out_shape=jax.ShapeDtypeStruct((bsz, m, 2 * d), BF16),
        compiler_params=_params("parallel"),
        name="kv_proj",
    )(mem, g.reshape(1, d), w_kv.astype(BF16))


def _xattn_kernel(h_ref, gx_ref, wq_ref, kv_ref, wo_ref, gf_ref, rhi_ref, rlo_ref,
                  h_out, xn_out, probs_out, o_s, *, d, heads):
    x = h_ref[0]
    xn = _rms(x, gx_ref[...]).astype(BF16)
    hd = d // heads
    q = (_dot(xn, wq_ref[...]) * (hd ** -0.5)).astype(BF16)
    kv = kv_ref[0]
    for a in range(heads):
        sc = _dot_nt(q[:, a * hd:(a + 1) * hd], kv[:, a * hd:(a + 1) * hd])
        p = jnp.exp(sc - jnp.max(sc, axis=-1, keepdims=True))
        l = jnp.sum(p, axis=-1, keepdims=True)
        o = _dot(p.astype(BF16), kv[:, d + a * hd:d + (a + 1) * hd]) / l
        o_s[:, a * hd:(a + 1) * hd] = o.astype(BF16)
    hn = x + _dot(o_s[...], wo_ref[...])
    h_out[0] = hn
    xf = _rms(hn, gf_ref[...])
    hi = xf.astype(BF16)
    xn_out[0] = hi
    lo = (xf - hi.astype(F32)).astype(BF16)
    lg = _dot_nt(rhi_ref[...], hi) + _dot_nt(rhi_ref[...], lo) + _dot_nt(rlo_ref[...], hi)
    e = jnp.exp(lg - jnp.max(lg, axis=0, keepdims=True))
    probs_out[0] = e / jnp.sum(e, axis=0, keepdims=True)


def _xattn_layer(h, kv, gx, w_q, w_o, gf, router):
    bsz, s, d = h.shape
    m = kv.shape[1]
    e = router.shape[1]
    tm = min(ROW_TILE, s)
    rt = router.T
    r_hi = rt.astype(BF16)
    r_lo = (rt - r_hi.astype(F32)).astype(BF16)
    kern = functools.partial(_xattn_kernel, d=d, heads=XATTN_HEADS)
    return pl.pallas_call(
        kern,
        grid=(bsz, s // tm),
        in_specs=[pl.BlockSpec((1, tm, d), lambda b, i: (b, i, 0)),
                  _full((1, d)), _full((d, d)),
                  pl.BlockSpec((1, m, 2 * d), lambda b, i: (b, 0, 0)),
                  _full((d, d)), _full((1, d)), _full((e, d)), _full((e, d))],
        out_specs=[pl.BlockSpec((1, tm, d), lambda b, i: (b, i, 0)),
                   pl.BlockSpec((1, tm, d), lambda b, i: (b, i, 0)),
                   pl.BlockSpec((1, e, tm), lambda b, i: (b, 0, i))],
        out_shape=[jax.ShapeDtypeStruct((bsz, s, d), F32),
                   jax.ShapeDtypeStruct((bsz, s, d), BF16),
                   jax.ShapeDtypeStruct((bsz, e, s), F32)],
        scratch_shapes=[pltpu.VMEM((tm, d), BF16)],
        compiler_params=_params("parallel", "parallel"),
        name="xattn_router",
    )(h, gx.reshape(1, d), w_q.astype(BF16), kv, w_o.astype(BF16), gf.reshape(1, d), r_hi, r_lo)


def _select_kernel(p_ref, code_ref, *, cap, blk):
    p = p_ref[0]
    e, s = p.shape
    bits = pltpu.bitcast(p, I32)

    def count(mask):
        return jnp.sum(mask.astype(F32), axis=1, keepdims=True)

    def search(k, t):
        cand = jnp.bitwise_or(t, jnp.left_shift(jnp.int32(1), 30 - k))
        return jnp.where(count(bits >= cand) >= cap, cand, t)

    thr = lax.fori_loop(0, 31, search, jnp.zeros((e, 1), I32))
    gt = bits > thr
    eq = bits == thr
    need = cap - count(gt)

    r = lax.broadcasted_iota(I32, (blk, blk), 0)
    c = lax.broadcasted_iota(I32, (blk, blk), 1)
    tri = jnp.where(r < c, 1.0, 0.0).astype(BF16)

    def prefix(mask_f):
        out = []
        carry = jnp.zeros((e, 1), F32)
        for j in range(0, s, blk):
            mb = mask_f[:, j:j + blk]
            out.append(_dot(mb.astype(BF16), tri) + carry)
            carry = carry + jnp.sum(mb, axis=1, keepdims=True)
        return out

    eq_f = eq.astype(F32)
    tie_rank = prefix(eq_f)
    for j0, tr in zip(range(0, s, blk), tie_rank):
        sel_b = jnp.logical_or(gt[:, j0:j0 + blk],
                               jnp.logical_and(eq[:, j0:j0 + blk], tr < need))
        code_ref[0, :, j0:j0 + blk] = sel_b.astype(I32)
    sel_f = code_ref[0].astype(F32)
    rank = prefix(sel_f)
    for j0, rk in zip(range(0, s, blk), rank):
        code_ref[0, :, j0:j0 + blk] = jnp.where(sel_f[:, j0:j0 + blk] > 0.0, rk.astype(I32), -1)


def _select(probs_t, cap):
    bsz, e, s = probs_t.shape
    kern = functools.partial(_select_kernel, cap=cap, blk=min(TOKEN_BLOCK, s))
    return pl.pallas_call(
        kern,
        grid=(bsz,),
        in_specs=[pl.BlockSpec((1, e, s), lambda b: (b, 0, 0))],
        out_specs=pl.BlockSpec((1, e, s), lambda b: (b, 0, 0)),
        out_shape=jax.ShapeDtypeStruct((bsz, e, s), I32),
        compiler_params=_params("parallel"),
        name="expert_choice_select",
    )(probs_t)


def _ffn_kernel(x_ref, code_ref, wg_ref, wu_ref, wd_ref, y_ref, *, cap, blk):
    s = x_ref.shape[1]
    slot = lax.broadcasted_iota(I32, (cap, blk), 0)
    xe = None
    for j in range(0, s, blk):
        onehot = jnp.where(slot == code_ref[0, :, j:j + blk], 1.0, 0.0).astype(BF16)
        part = _dot(onehot, x_ref[0, j:j + blk, :])
        xe = part if xe is None else xe + part
    xe = xe.astype(BF16)
    gt = _dot(xe, wg_ref[0])
    up = _dot(xe, wu_ref[0])
    hid = (gt * jax.nn.sigmoid(gt) * up).astype(BF16)
    y_ref[0, 0] = _dot(hid, wd_ref[0]).astype(BF16)


def _expert_ffn(xn, code, w_gate, w_up, w_down, cap):
    bsz, s, d = xn.shape
    e, _, f = w_gate.shape
    kern = functools.partial(_ffn_kernel, cap=cap, blk=min(TOKEN_BLOCK, s))
    return pl.pallas_call(
        kern,
        grid=(e, bsz),
        in_specs=[pl.BlockSpec((1, s, d), lambda k, b: (b, 0, 0)),
                  pl.BlockSpec((1, 1, s), lambda k, b: (b * e + k, 0, 0)),
                  pl.BlockSpec((1, d, f), lambda k, b: (k, 0, 0)),
                  pl.BlockSpec((1, d, f), lambda k, b: (k, 0, 0)),
                  pl.BlockSpec((1, f, d), lambda k, b: (k, 0, 0))],
        out_specs=pl.BlockSpec((1, 1, cap, d), lambda k, b: (b, k, 0, 0)),
        out_shape=jax.ShapeDtypeStruct((bsz, e, cap, d), BF16),
        compiler_params=_params("parallel", "parallel"),
        name="expert_ffn",
    )(xn, code.reshape(bsz * e, 1, s), w_gate, w_up, w_down)


def _combine_kernel(h_ref, code_ref, p_ref, y_ref, g_ref, o_ref, *, cap, final_norm):
    acc = h_ref[0]
    tb = acc.shape[0]
    n_exp = y_ref.shape[1]
    slot = lax.broadcasted_iota(I32, (tb, cap), 1)
    code = code_ref[0]
    prob = p_ref[0]
    for k in range(n_exp):
        onehot = jnp.where(slot == code[:, k:k + 1], 1.0, 0.0).astype(BF16)
        acc = acc + _dot(onehot, y_ref[0, k]) * prob[:, k:k + 1]
    o_ref[0] = _rms(acc, g_ref[...]) if final_norm else acc


def _moe_combine(h, code_t, probs, ye, g_final, final_norm):
    bsz, s, d = h.shape
    e, cap = ye.shape[1], ye.shape[2]
    tb = min(TOKEN_BLOCK, s)
    kern = functools.partial(_combine_kernel, cap=cap, final_norm=final_norm)
    return pl.pallas_call(
        kern,
        grid=(bsz, s // tb),
        in_specs=[pl.BlockSpec((1, tb, d), lambda b, j: (b, j, 0)),
                  pl.BlockSpec((1, tb, e), lambda b, j: (b, j, 0)),
                  pl.BlockSpec((1, tb, e), lambda b, j: (b, j, 0)),
                  pl.BlockSpec((1, e, cap, d), lambda b, j: (b, 0, 0, 0)),
                  _full((1, d))],
        out_specs=pl.BlockSpec((1, tb, d), lambda b, j: (b, j, 0)),
        out_shape=jax.ShapeDtypeStruct((bsz, s, d), F32),
        compiler_params=_params("parallel", "parallel"),
        name="moe_combine",
    )(h, code_t, probs, ye, g_final.reshape(1, d))


def _rwkv_prep_kernel(h_ref, hp_ref, hn_ref, g_ref, mu_ref, wrkv_ref, w1_ref, a1_ref, g1_ref,
                      w2_ref, a2_ref, g2_ref, w0_ref, a0_ref, kk_ref, ka_ref, rk_ref, hs_ref,
                      r_out, v_out, kn_out, gate_out, bonus_out, lw_out, kd_out, bd_out,
                      xn_s, *, tm):
    i = pl.program_id(1)
    g = g_ref[...]
    keep_prev = jnp.where(i == 0, 0.0, 1.0)
    keep_next = jnp.where(i == pl.num_programs(1) - 1, 0.0, 1.0)
    xn_s[0:HALO, :] = _rms(hp_ref[0], g) * keep_prev
    xn_s[HALO:HALO + tm, :] = _rms(h_ref[0], g)
    xn_s[HALO + tm:, :] = _rms(hn_ref[0], g) * keep_next
    xn = xn_s[pl.ds(HALO, tm), :]
    xx = 0.5 * (xn_s[pl.ds(HALO - 1, tm), :] + xn_s[pl.ds(HALO + 1, tm), :]) - xn

    def mix(j):
        return (xn + xx * mu_ref[j:j + 1, :]).astype(BF16)

    r = _dot(mix(0), wrkv_ref[0])
    k = _dot(mix(1), wrkv_ref[1])
    v = _dot(mix(2), wrkv_ref[2])
    hw = jnp.tanh(_dot(mix(3), w1_ref[...])).astype(BF16)
    ha = _dot(mix(4), a1_ref[...]).astype(BF16)
    hg = jax.nn.sigmoid(_dot(mix(5), g1_ref[...])).astype(BF16)
    gate_out[0] = _dot(hg, g2_ref[...]).astype(BF16)

    hsum = hs_ref[...]
    kk = k * kk_ref[...]
    nrm2 = _dot((kk * kk).astype(BF16), hsum)
    kn = kk / jnp.maximum(jnp.sqrt(nrm2), 1e-12)
    r_out[0] = r.astype(BF16)
    v_out[0] = v.astype(BF16)
    kn_out[0] = kn.astype(BF16)
    ksum = None
    for n in range(2):
        w_raw = w0_ref[n:n + 1, :] + _dot(hw, w2_ref[n])
        lw_out[n, 0] = -DECAY_SCALE * jax.nn.sigmoid(w_raw)
        a = jax.nn.sigmoid(a0_ref[n:n + 1, :] + _dot(ha, a2_ref[n]))
        kd = k * (1.0 + (a - 1.0) * ka_ref[...])
        kd_out[n, 0] = kd.astype(BF16)
        bd_out[n, 0] = (kn * a).astype(BF16)
        ksum = kd if ksum is None else ksum + kd
    coef = _dot((r * ksum * rk_ref[...]).astype(BF16), hsum)
    bonus_out[0] = (coef * v).astype(BF16)


def _head_sum_matrix(d, n):
    idx = jnp.arange(d) // n
    return (idx[:, None] == idx[None, :]).astype(BF16)


def _rwkv_prep(h, g, mu, w_rkv, w0, w1, w2, a0, a1, a2, g1, g2, k_k, k_a, r_k):
    bsz, s, d = h.shape
    tm = min(RWKV_ROW_TILE, s)
    lora = w1.shape[-1]
    glora = g1.shape[-1]
    w1c = jnp.concatenate([w1[0], w1[1]], axis=1).astype(BF16)
    a1c = jnp.concatenate([a1[0], a1[1]], axis=1).astype(BF16)
    keep = (jnp.arange(2 * lora)[None, :, None] // lora) == jnp.arange(2)[:, None, None]
    w2p = jnp.where(keep, jnp.concatenate([w2, w2], axis=1), 0.0).astype(BF16)
    a2p = jnp.where(keep, jnp.concatenate([a2, a2], axis=1), 0.0).astype(BF16)
    kern = functools.partial(_rwkv_prep_kernel, tm=tm)
    tok = pl.BlockSpec((1, tm, d), lambda b, i: (b, i, 0))
    tok2 = pl.BlockSpec((2, 1, tm, d), lambda b, i: (0, b, i, 0))
    sd = jax.ShapeDtypeStruct
    return pl.pallas_call(
        kern,
        grid=(bsz, s // tm),
        in_specs=_halo_specs(tm, s, d) + [
            _full((1, d)), _full((6, d)), _full((3, d, d)),
            _full((d, 2 * lora)), _full((d, 2 * lora)), _full((d, glora)),
            _full((2, 2 * lora, d)), _full((2, 2 * lora, d)), _full((glora, d)),
            _full((2, d)), _full((2, d)), _full((1, d)), _full((1, d)), _full((1, d)),
            _full((d, d))],
        out_specs=[tok, tok, tok, tok, tok, tok2, tok2, tok2],
        out_shape=[sd((bsz, s, d), BF16)] * 5 + [sd((2, bsz, s, d), F32),
                                                 sd((2, bsz, s, d), BF16),
                                                 sd((2, bsz, s, d), BF16)],
        scratch_shapes=[pltpu.VMEM((tm + 2 * HALO, d), F32)],
        compiler_params=_params("parallel", "parallel"),
        name="rwkv_prep",
    )(h, h, h, g.reshape(1, d), mu, w_rkv.astype(BF16), w1c, a1c, g1.astype(BF16),
      w2p, a2p, g2.astype(BF16), w0, a0,
      k_k.reshape(1, d), k_a.reshape(1, d), r_k.reshape(1, d),
      _head_sum_matrix(d, RWKV_HEAD_DIM))


def _wkv_kernel(r_ref, v_ref, kn_ref, lw_ref, kd_ref, bd_ref, y_ref, q_s, *, c, ng, reverse):
    hd = RWKV_HEAD_DIM
    w = 2 * hd

    @pl.when(pl.program_id(2) == 0)
    def _():
        q_s[...] = jnp.zeros_like(q_s)

    ti = lax.broadcasted_iota(I32, (c, c), 0)
    si = lax.broadcasted_iota(I32, (c, c), 1)
    tri_incl = jnp.where(si >= ti if reverse else si <= ti, 1.0, 0.0).astype(BF16)
    lane = lax.broadcasted_iota(I32, (1, w), 1)
    m_lo = lane < hd
    ti2 = lax.broadcasted_iota(I32, (c, 2 * c), 0)
    si2 = lax.broadcasted_iota(I32, (c, 2 * c), 1)
    si2 = jnp.where(si2 >= c, si2 - c, si2)
    if reverse:
        incl2, strict2 = si2 >= ti2, si2 > ti2
    else:
        incl2, strict2 = si2 <= ti2, si2 < ti2
    eye2 = jnp.where(si2 == ti2, 1.0, 0.0)
    rr = lax.broadcasted_iota(I32, (2 * c, 2 * c), 0)
    cc = lax.broadcasted_iota(I32, (2 * c, 2 * c), 1)
    bd_mask_c = jnp.where(rr < c, 0, 1) == jnp.where(cc < c, 0, 1)
    rr = lax.broadcasted_iota(I32, (w, w), 0)
    cc = lax.broadcasted_iota(I32, (w, w), 1)
    bd_mask_h = jnp.where(rr < hd, 0, 1) == jnp.where(cc < hd, 0, 1)

    def row_stack(x):
        return jnp.concatenate([jnp.where(m_lo, x, 0.0), jnp.where(m_lo, 0.0, x)],
                               axis=0).astype(BF16)

    def block_diag(xp):
        return jnp.where(bd_mask_c, jnp.concatenate([xp, xp], axis=0), 0.0).astype(BF16)

    order = range(ng - 1, -1, -1) if reverse else range(ng)
    for ci in order:
        sl = pl.ds(ci * c, c)
        r = r_ref[0, sl, :].astype(F32)
        v = v_ref[0, sl, :].astype(F32)
        kn = kn_ref[0, sl, :].astype(F32)
        lw = lw_ref[0, 0, sl, :]
        kd = kd_ref[0, 0, sl, :].astype(F32)
        bd = bd_ref[0, 0, sl, :].astype(F32)

        lw_hi = lw.astype(BF16)
        lw_lo = (lw - lw_hi.astype(F32)).astype(BF16)
        l_incl = _dot(tri_incl, lw_hi) + _dot(tri_incl, lw_lo)
        l_excl = l_incl - lw
        l_tot = l_incl[0:1, :] if reverse else l_incl[c - 1:c, :]
        g_tot = jnp.exp(l_tot)
        g_inv = jnp.exp(-l_incl)
        g_end = jnp.exp(l_tot - l_incl)
        rt = r * jnp.exp(l_incl)
        at = -kn * jnp.exp(l_excl)
        v_rs = row_stack(v)

        lhs = jnp.concatenate([at, rt], axis=0).astype(BF16)
        rhs = jnp.concatenate([row_stack(bd * g_inv), row_stack(kd * g_inv)], axis=0)
        gm = _dot_nt(lhs, rhs)
        a_ab = jnp.where(strict2, gm[0:c, 0:2 * c], 0.0)
        a_ak = jnp.where(strict2, gm[0:c, 2 * c:4 * c], 0.0)
        a_rb = jnp.where(incl2, gm[c:2 * c, 0:2 * c], 0.0)
        a_rk = jnp.where(incl2, gm[c:2 * c, 2 * c:4 * c], 0.0)

        t_p = eye2 + a_ab
        pw = a_ab
        pw_bd = block_diag(pw)
        for _ in range(max(c.bit_length() - 2, 0)):
            pw = _dot(pw.astype(BF16), pw_bd)
            pw_bd = block_diag(pw)
            t_p = t_p + _dot(t_p.astype(BF16), pw_bd)
        t_b = t_p.astype(BF16)

        akv = _dot(a_ak.astype(BF16), v_rs)
        wu = _dot(t_b, jnp.concatenate([row_stack(at), row_stack(akv)], axis=1))
        w_m = wu[:, 0:w]
        u_t = wu[:, w:2 * w]
        zeros = jnp.zeros((2 * c, w), BF16)
        big_l = jnp.concatenate([a_rb, a_rk], axis=1).astype(BF16)
        big_r = jnp.concatenate(
            [jnp.concatenate([row_stack(w_m), row_stack(u_t)], axis=1),
             jnp.concatenate([zeros, v_rs], axis=1)], axis=0)
        ry = _dot(big_l, big_r)
        r_hat = rt + ry[:, 0:w]
        y0 = ry[:, w:2 * w]

        bh = (bd * g_end).astype(BF16)
        kh = (kd * g_end).astype(BF16)
        n_m = jnp.where(bd_mask_h, _dot_tn(w_m.astype(BF16), bh), 0.0)
        q0 = jnp.where(bd_mask_h,
                       _dot_tn(jnp.concatenate([u_t, v], axis=0).astype(BF16),
                               jnp.concatenate([bh, kh], axis=0)), 0.0)

        q = q_s[...]
        q_b = q.astype(BF16)
        y_ref[0, sl, :] = _dot_nt(r_hat.astype(BF16), q_b) + y0
        q_s[...] = q * g_tot + _dot(q_b, n_m.astype(BF16)) + q0


def _wkv(r, v, kn, lw, kd, bd, reverse):
    bsz, s, d = r.shape
    c = min(CHUNK, s)
    ng = min(CHUNK_GROUP, s // c)
    cg = c * ng
    n_steps = s // cg
    w = 2 * RWKV_HEAD_DIM
    n = 1 if reverse else 0
    if reverse:
        tok = pl.BlockSpec((1, cg, w), lambda b, p, j: (b, n_steps - 1 - j, p))
        tok2 = pl.BlockSpec((1, 1, cg, w), lambda b, p, j: (n, b, n_steps - 1 - j, p))
    else:
        tok = pl.BlockSpec((1, cg, w), lambda b, p, j: (b, j, p))
        tok2 = pl.BlockSpec((1, 1, cg, w), lambda b, p, j: (n, b, j, p))
    kern = functools.partial(_wkv_kernel, c=c, ng=ng, reverse=reverse)
    return pl.pallas_call(
        kern,
        grid=(bsz, d // w, n_steps),
        in_specs=[tok, tok, tok, tok2, tok2, tok2],
        out_specs=tok,
        out_shape=jax.ShapeDtypeStruct((bsz, s, d), F32),
        scratch_shapes=[pltpu.VMEM((w, w), F32)],
        compiler_params=_params("parallel", "parallel", "arbitrary"),
        name="wkv_rev" if reverse else "wkv_fwd",
    )(r, v, kn, lw, kd, bd)


def _rwkv_post_kernel(h_ref, yf_ref, yr_ref, bonus_ref, gate_ref, lnw_ref, lnb_ref, hs_ref,
                      wout_ref, o_ref, *, n):
    hsum = hs_ref[...]
    y = yf_ref[0] + yr_ref[0]
    y_hi = y.astype(BF16)
    y_lo = (y - y_hi.astype(F32)).astype(BF16)
    mean = (_dot(y_hi, hsum) + _dot(y_lo, hsum)) * (1.0 / n)
    yc = y - mean
    var = _dot((yc * yc).astype(BF16), hsum) * (1.0 / n)
    yn = yc * lax.rsqrt(var + GN_EPS) * lnw_ref[...] + lnb_ref[...]
    yn = yn + bonus_ref[0].astype(F32)
    z = (yn * gate_ref[0].astype(F32)).astype(BF16)
    o_ref[0] = h_ref[0] + _dot(z, wout_ref[...])


def _rwkv_post(h, yf, yr, bonus, gate, ln_w, ln_b, w_out):
    bsz, s, d = h.shape
    tm = min(ROW_TILE, s)
    tok = pl.BlockSpec((1, tm, d), lambda b, i: (b, i, 0))
    kern = functools.partial(_rwkv_post_kernel, n=RWKV_HEAD_DIM)
    return pl.pallas_call(
        kern,
        grid=(bsz, s // tm),
        in_specs=[tok, tok, tok, tok, tok, _full((1, d)), _full((1, d)), _full((d, d)),
                  _full((d, d))],
        out_specs=tok,
        out_shape=jax.ShapeDtypeStruct((bsz, s, d), F32),
        compiler_params=_params("parallel", "parallel"),
        name="rwkv_post",
    )(h, yf, yr, bonus, gate, ln_w.reshape(1, d), ln_b.reshape(1, d),
      _head_sum_matrix(d, RWKV_HEAD_DIM), w_out.astype(BF16))


def kernel(x, mem, norm_mix, norm_xattn, norm_mem, norm_ffn, norm_final,
           conv_w_in, conv_w, conv_w_out,
           rwkv_mu, rwkv_w_rkv, rwkv_w0, rwkv_w1, rwkv_w2, rwkv_a0, rwkv_a1, rwkv_a2,
           rwkv_g1, rwkv_g2, rwkv_k_k, rwkv_k_a, rwkv_r_k, rwkv_ln_w, rwkv_ln_b, rwkv_w_out,
           xattn_w_q, xattn_w_kv, xattn_w_o,
           moe_router, moe_w_gate, moe_w_up, moe_w_down):
    depth = norm_mix.shape[0]
    bsz, s, d = x.shape
    n_exp = moe_router.shape[-1]
    cap = CAPACITY_FACTOR * s // n_exp
    h = x
    for i in range(depth):
        j = i // N_MIXERS
        if i % N_MIXERS == 0:
            h = _conv_layer(h, norm_mix[i], conv_w_in[j], conv_w[j], conv_w_out[j])
        else:
            r, v, kn, gate, bonus, lw, kd, bd = _rwkv_prep(
                h, norm_mix[i], rwkv_mu[j], rwkv_w_rkv[j], rwkv_w0[j], rwkv_w1[j], rwkv_w2[j],
                rwkv_a0[j], rwkv_a1[j], rwkv_a2[j], rwkv_g1[j], rwkv_g2[j],
                rwkv_k_k[j], rwkv_k_a[j], rwkv_r_k[j])
            yf = _wkv(r, v, kn, lw, kd, bd, reverse=False)
            yr = _wkv(r, v, kn, lw, kd, bd, reverse=True)
            h = _rwkv_post(h, yf, yr, bonus, gate, rwkv_ln_w[j], rwkv_ln_b[j], rwkv_w_out[j])
        kv = _kv_proj(mem, norm_mem[i], xattn_w_kv[i])
        h, xn, probs_t = _xattn_layer(h, kv, norm_xattn[i], xattn_w_q[i], xattn_w_o[i],
                                      norm_ffn[i], moe_router[i])
        code = _select(probs_t, cap)
        ye = _expert_ffn(xn, code, moe_w_gate[i].astype(BF16), moe_w_up[i].astype(BF16),
                         moe_w_down[i].astype(BF16), cap)
        h = _moe_combine(h, jnp.swapaxes(code, 1, 2), jnp.swapaxes(probs_t, 1, 2), ye,
                         norm_final, final_norm=(i == depth - 1))
    return h
```

```python
import functools

import jax
import jax.numpy as jnp
from jax import lax
from jax.experimental import pallas as pl
from jax.experimental.pallas import tpu as pltpu

F32 = jnp.float32
BF16 = jnp.bfloat16
I32 = jnp.int32

N_MIXERS = 2
RWKV_HEAD_DIM = 64
XATTN_HEADS = 4
CAPACITY_FACTOR = 2
GN_EPS = 64e-5
RMS_EPS = 1e-6
DECAY_SCALE = 0.6065306597126334

V7X_VMEM_LIMIT_BYTES = 56 * 1024 * 1024
LANES = 128
BF16_SUBLANES = 16

ROW_TILE = 512
RWKV_ROW_TILE = 256
HALO = BF16_SUBLANES
CHUNK = 64
CHUNK_GROUP = 8
TOKEN_BLOCK = 512


def _params(*semantics):
    return pltpu.CompilerParams(dimension_semantics=semantics,
                                vmem_limit_bytes=V7X_VMEM_LIMIT_BYTES)


def _rms(x, g):
    return x * lax.rsqrt(jnp.mean(x * x, axis=-1, keepdims=True) + RMS_EPS) * g


def _dot(a, b):
    return jnp.dot(a, b, preferred_element_type=F32)


def _dot_nt(a, b):
    return lax.dot_general(a, b, (((1,), (1,)), ((), ())), preferred_element_type=F32)


def _dot_tn(a, b):
    return lax.dot_general(a, b, (((0,), (0,)), ((), ())), preferred_element_type=F32)


def _full(shape):
    n = len(shape)
    return pl.BlockSpec(shape, lambda *_: (0,) * n)


def _conv_kernel(h_ref, hp_ref, hn_ref, g_ref, win_ref, cw_ref, wout_ref, o_ref,
                 xn_s, u_s, gate_s, *, tm, d, cb):
    i = pl.program_id(1)
    g = g_ref[...]
    x = h_ref[0]
    xn_s[0:HALO, :] = _rms(hp_ref[0], g).astype(BF16)
    xn_s[HALO:HALO + tm, :] = _rms(x, g).astype(BF16)
    xn_s[HALO + tm:, :] = _rms(hn_ref[0], g).astype(BF16)
    rows = tm + 2 * HALO
    row = lax.broadcasted_iota(I32, (rows, 1), 0)
    lo = jnp.where(i == 0, HALO, 0)
    hi = jnp.where(i == pl.num_programs(1) - 1, HALO + tm, rows)
    pad = jnp.logical_or(row < lo, row >= hi)
    xa = xn_s[...]
    for c0 in range(0, d, cb):
        c_gate = _dot(xa, win_ref[:, d + c0:d + c0 + cb])
        hx = _dot(xa, win_ref[:, 2 * d + c0:2 * d + c0 + cb])
        u_s[...] = jnp.where(pad, 0.0, c_gate * hx)
        conv = (u_s[pl.ds(HALO - 1, tm), :] * cw_ref[0:1, c0:c0 + cb]
                + u_s[pl.ds(HALO, tm), :] * cw_ref[1:2, c0:c0 + cb]
                + u_s[pl.ds(HALO + 1, tm), :] * cw_ref[2:3, c0:c0 + cb])
        b_gate = _dot(xn_s[HALO:HALO + tm, :], win_ref[:, c0:c0 + cb])
        gate_s[:, c0:c0 + cb] = (b_gate * conv).astype(BF16)
    o_ref[0] = x + _dot(gate_s[...], wout_ref[...])


def _halo_specs(tm, s, d):
    nb = tm // HALO
    last = s // HALO - 1
    return [
        pl.BlockSpec((1, tm, d), lambda b, i: (b, i, 0)),
        pl.BlockSpec((1, HALO, d), lambda b, i: (b, jnp.maximum(i * nb - 1, 0), 0)),
        pl.BlockSpec((1, HALO, d), lambda b, i: (b, jnp.minimum((i + 1) * nb, last), 0)),
    ]


def _conv_layer(h, g, w_in, conv_w, w_out):
    bsz, s, d = h.shape
    tm = min(ROW_TILE, s)
    cb = 512
    kern = functools.partial(_conv_kernel, tm=tm, d=d, cb=cb)
    return pl.pallas_call(
        kern,
        grid=(bsz, s // tm),
        in_specs=_halo_specs(tm, s, d) + [
            _full((1, d)), _full((d, 3 * d)), _full((3, d)), _full((d, d))],
        out_specs=pl.BlockSpec((1, tm, d), lambda b, i: (b, i, 0)),
        out_shape=jax.ShapeDtypeStruct((bsz, s, d), F32),
        scratch_shapes=[pltpu.VMEM((tm + 2 * HALO, d), BF16),
                        pltpu.VMEM((tm + 2 * HALO, cb), F32),
                        pltpu.VMEM((tm, d), BF16)],
        compiler_params=_params("parallel", "parallel"),
        name="conv_mixer",
    )(h, h, h, g.reshape(1, d), w_in.astype(BF16), conv_w, w_out.astype(BF16))


def _kv_kernel(m_ref, g_ref, w_ref, o_ref):
    xn = _rms(m_ref[0], g_ref[...]).astype(BF16)
    o_ref[0] = _dot(xn, w_ref[...]).astype(BF16)


def _kv_proj(mem, g, w_kv):
    bsz, m, d = mem.shape
    return pl.pallas_call(
        _kv_kernel,
        grid=(bsz,),
        in_specs=[pl.BlockSpec((1, m, d), lambda b: (b, 0, 0)), _full((1, d)), _full((d, 2 * d))],
        out_specs=pl.BlockSpec((1, m, 2 * d), lambda b: (b, 0, 0)),
        out_shape=jax.ShapeDtypeStruct((bsz, m, 2 * d), BF16),
        compiler_params=_params("parallel"),
        name="kv_proj",
    )(mem, g.reshape(1, d), w_kv.astype(BF16))


def _xattn_kernel(h_ref, gx_ref, wq_ref, kv_ref, wo_ref, gf_ref, rhi_ref, rlo_ref,
                  h_out, xn_out, probs_out, o_s, *, d, heads):
    x = h_ref[0]
    xn = _rms(x, gx_ref[...]).astype(BF16)
    hd = d // heads
    q = (_dot(xn, wq_ref[...]) * (hd ** -0.5)).astype(BF16)
    kv = kv_ref[0]
    for a in range(heads):
        sc = _dot_nt(q[:, a * hd:(a + 1) * hd], kv[:, a * hd:(a + 1) * hd])
        p = jnp.exp(sc - jnp.max(sc, axis=-1, keepdims=True))
        l = jnp.sum(p, axis=-1, keepdims=True)
        o = _dot(p.astype(BF16), kv[:, d + a * hd:d + (a + 1) * hd]) / l
        o_s[:, a * hd:(a + 1) * hd] = o.astype(BF16)
    hn = x + _dot(o_s[...], wo_ref[...])
    h_out[0] = hn
    xf = _rms(hn, gf_ref[...])
    hi = xf.astype(BF16)
    xn_out[0] = hi
    lo = (xf - hi.astype(F32)).astype(BF16)
    lg = _dot_nt(rhi_ref[...], hi) + _dot_nt(rhi_ref[...], lo) + _dot_nt(rlo_ref[...], hi)
    e = jnp.exp(lg - jnp.max(lg, axis=0, keepdims=True))
    probs_out[0] = e / jnp.sum(e, axis=0, keepdims=True)


def _xattn_layer(h, kv, gx, w_q, w_o, gf, router):
    bsz, s, d = h.shape
    m = kv.shape[1]
    e = router.shape[1]
    tm = min(ROW_TILE, s)
    rt = router.T
    r_hi = rt.astype(BF16)
    r_lo = (rt - r_hi.astype(F32)).astype(BF16)
    kern = functools.partial(_xattn_kernel, d=d, heads=XATTN_HEADS)
    return pl.pallas_call(
        kern,
        grid=(bsz, s // tm),
        in_specs=[pl.BlockSpec((1, tm, d), lambda b, i: (b, i, 0)),
                  _full((1, d)), _full((d, d)),
                  pl.BlockSpec((1, m, 2 * d), lambda b, i: (b, 0, 0)),
                  _full((d, d)), _full((1, d)), _full((e, d)), _full((e, d))],
        out_specs=[pl.BlockSpec((1, tm, d), lambda b, i: (b, i, 0)),
                   pl.BlockSpec((1, tm, d), lambda b, i: (b, i, 0)),
                   pl.BlockSpec((1, e, tm), lambda b, i: (b, 0, i))],
        out_shape=[jax.ShapeDtypeStruct((bsz, s, d), F32),
                   jax.ShapeDtypeStruct((bsz, s, d), BF16),
                   jax.ShapeDtypeStruct((bsz, e, s), F32)],
        scratch_shapes=[pltpu.VMEM((tm, d), BF16)],
        compiler_params=_params("parallel", "parallel"),
        name="xattn_router",
    )(h, gx.reshape(1, d), w_q.astype(BF16), kv, w_o.astype(BF16), gf.reshape(1, d), r_hi, r_lo)


def _select_kernel(p_ref, code_ref, *, cap, blk):
    p = p_ref[0]
    e, s = p.shape
    bits = pltpu.bitcast(p, I32)

    def count(mask):
        return jnp.sum(mask.astype(F32), axis=1, keepdims=True)

    def search(k, t):
        cand = jnp.bitwise_or(t, jnp.left_shift(jnp.int32(1), 30 - k))
        return jnp.where(count(bits >= cand) >= cap, cand, t)

    thr = lax.fori_loop(0, 31, search, jnp.zeros((e, 1), I32))
    gt = bits > thr
    eq = bits == thr
    need = cap - count(gt)

    r = lax.broadcasted_iota(I32, (blk, blk), 0)
    c = lax.broadcasted_iota(I32, (blk, blk), 1)
    tri = jnp.where(r < c, 1.0, 0.0).astype(BF16)

    def prefix(mask_f):
        out = []
        carry = jnp.zeros((e, 1), F32)
        for j in range(0, s, blk):
            mb = mask_f[:, j:j + blk]
            out.append(_dot(mb.astype(BF16), tri) + carry)
            carry = carry + jnp.sum(mb, axis=1, keepdims=True)
        return out

    eq_f = eq.astype(F32)
    tie_rank = prefix(eq_f)
    for j0, tr in zip(range(0, s, blk), tie_rank):
        sel_b = jnp.logical_or(gt[:, j0:j0 + blk],
                               jnp.logical_and(eq[:, j0:j0 + blk], tr < need))
        code_ref[0, :, j0:j0 + blk] = sel_b.astype(I32)
    sel_f = code_ref[0].astype(F32)
    rank = prefix(sel_f)
    for j0, rk in zip(range(0, s, blk), rank):
        code_ref[0, :, j0:j0 + blk] = jnp.where(sel_f[:, j0:j0 + blk] > 0.0, rk.astype(I32), -1)


def _select(probs_t, cap):
    bsz, e, s = probs_t.shape
    kern = functools.partial(_select_kernel, cap=cap, blk=min(TOKEN_BLOCK, s))
    return pl.pallas_call(
        kern,
        grid=(bsz,),
        in_specs=[pl.BlockSpec((1, e, s), lambda b: (b, 0, 0))],
        out_specs=pl.BlockSpec((1, e, s), lambda b: (b, 0, 0)),
        out_shape=jax.ShapeDtypeStruct((bsz, e, s), I32),
        compiler_params=_params("parallel"),
        name="expert_choice_select",
    )(probs_t)


def _ffn_kernel(x_ref, code_ref, wg_ref, wu_ref, wd_ref, y_ref, *, cap, blk):
    s = x_ref.shape[1]
    slot = lax.broadcasted_iota(I32, (cap, blk), 0)
    xe = None
    for j in range(0, s, blk):
        onehot = jnp.where(slot == code_ref[0, :, j:j + blk], 1.0, 0.0).astype(BF16)
        part = _dot(onehot, x_ref[0, j:j + blk, :])
        xe = part if xe is None else xe + part
    xe = xe.astype(BF16)
    gt = _dot(xe, wg_ref[0])
    up = _dot(xe, wu_ref[0])
    hid = (gt * jax.nn.sigmoid(gt) * up).astype(BF16)
    y_ref[0, 0] = _dot(hid, wd_ref[0]).astype(BF16)


def _expert_ffn(xn, code, w_gate, w_up, w_down, cap):
    bsz, s, d = xn.shape
    e, _, f = w_gate.shape
    kern = functools.partial(_ffn_kernel, cap=cap, blk=min(TOKEN_BLOCK, s))
    return pl.pallas_call(
        kern,
        grid=(e, bsz),
        in_specs=[pl.BlockSpec((1, s, d), lambda k, b: (b, 0, 0)),
                  pl.BlockSpec((1, 1, s), lambda k, b: (b * e + k, 0, 0)),
                  pl.BlockSpec((1, d, f), lambda k, b: (k, 0, 0)),
                  pl.BlockSpec((1, d, f), lambda k, b: (k, 0, 0)),
                  pl.BlockSpec((1, f, d), lambda k, b: (k, 0, 0))],
        out_specs=pl.BlockSpec((1, 1, cap, d), lambda k, b: (b, k, 0, 0)),
        out_shape=jax.ShapeDtypeStruct((bsz, e, cap, d), BF16),
        compiler_params=_params("parallel", "parallel"),
        name="expert_ffn",
    )(xn, code.reshape(bsz * e, 1, s), w_gate, w_up, w_down)


def _combine_kernel(h_ref, code_ref, p_ref, y_ref, g_ref, o_ref, *, cap, final_norm):
    acc = h_ref[0]
    tb = acc.shape[0]
    n_exp = y_ref.shape[1]
    slot = lax.broadcasted_iota(I32, (tb, cap), 1)
    code = code_ref[0]
    prob = p_ref[0]
    for k in range(n_exp):
        onehot = jnp.where(slot == code[:, k:k + 1], 1.0, 0.0).astype(BF16)
        acc = acc + _dot(onehot, y_ref[0, k]) * prob[:, k:k + 1]
    o_ref[0] = _rms(acc, g_ref[...]) if final_norm else acc


def _moe_combine(h, code_t, probs, ye, g_final, final_norm):
    bsz, s, d = h.shape
    e, cap = ye.shape[1], ye.shape[2]
    tb = min(TOKEN_BLOCK, s)
    kern = functools.partial(_combine_kernel, cap=cap, final_norm=final_norm)
    return pl.pallas_call(
        kern,
        grid=(bsz, s // tb),
        in_specs=[pl.BlockSpec((1, tb, d), lambda b, j: (b, j, 0)),
                  pl.BlockSpec((1, tb, e), lambda b, j: (b, j, 0)),
                  pl.BlockSpec((1, tb, e), lambda b, j: (b, j, 0)),
                  pl.BlockSpec((1, e, cap, d), lambda b, j: (b, 0, 0, 0)),
                  _full((1, d))],
        out_specs=pl.BlockSpec((1, tb, d), lambda b, j: (b, j, 0)),
        out_shape=jax.ShapeDtypeStruct((bsz, s, d), F32),
        compiler_params=_params("parallel", "parallel"),
        name="moe_combine",
    )(h, code_t, probs, ye, g_final.reshape(1, d))


def _rwkv_prep_kernel(h_ref, hp_ref, hn_ref, g_ref, mu_ref, wrkv_ref, w1_ref, a1_ref, g1_ref,
                      w2_ref, a2_ref, g2_ref, w0_ref, a0_ref, kk_ref, ka_ref, rk_ref, hs_ref,
                      r_out, v_out, kn_out, gate_out, bonus_out, lw_out, kd_out, bd_out,
                      xn_s, *, tm):
    i = pl.program_id(1)
    g = g_ref[...]
    keep_prev = jnp.where(i == 0, 0.0, 1.0)
    keep_next = jnp.where(i == pl.num_programs(1) - 1, 0.0, 1.0)
    xn_s[0:HALO, :] = _rms(hp_ref[0], g) * keep_prev
    xn_s[HALO:HALO + tm, :] = _rms(h_ref[0], g)
    xn_s[HALO + tm:, :] = _rms(hn_ref[0], g) * keep_next
    xn = xn_s[pl.ds(HALO, tm), :]
    xx = 0.5 * (xn_s[pl.ds(HALO - 1, tm), :] + xn_s[pl.ds(HALO + 1, tm), :]) - xn

    def mix(j):
        return (xn + xx * mu_ref[j:j + 1, :]).astype(BF16)

    r = _dot(mix(0), wrkv_ref[0])
    k = _dot(mix(1), wrkv_ref[1])
    v = _dot(mix(2), wrkv_ref[2])
    hw = jnp.tanh(_dot(mix(3), w1_ref[...])).astype(BF16)
    ha = _dot(mix(4), a1_ref[...]).astype(BF16)
    hg = jax.nn.sigmoid(_dot(mix(5), g1_ref[...])).astype(BF16)
    gate_out[0] = _dot(hg, g2_ref[...]).astype(BF16)

    hsum = hs_ref[...]
    kk = k * kk_ref[...]
    nrm2 = _dot((kk * kk).astype(BF16), hsum)
    kn = kk / jnp.maximum(jnp.sqrt(nrm2), 1e-12)
    r_out[0] = r.astype(BF16)
    v_out[0] = v.astype(BF16)
    kn_out[0] = kn.astype(BF16)
    ksum = None
    for n in range(2):
        w_raw = w0_ref[n:n + 1, :] + _dot(hw, w2_ref[n])
        lw_out[n, 0] = -DECAY_SCALE * jax.nn.sigmoid(w_raw)
        a = jax.nn.sigmoid(a0_ref[n:n + 1, :] + _dot(ha, a2_ref[n]))
        kd = k * (1.0 + (a - 1.0) * ka_ref[...])
        kd_out[n, 0] = kd.astype(BF16)
        bd_out[n, 0] = (kn * a).astype(BF16)
        ksum = kd if ksum is None else ksum + kd
    coef = _dot((r * ksum * rk_ref[...]).astype(BF16), hsum)
    bonus_out[0] = (coef * v).astype(BF16)


def _head_sum_matrix(d, n):
    idx = jnp.arange(d) // n
    return (idx[:, None] == idx[None, :]).astype(BF16)


def _rwkv_prep(h, g, mu, w_rkv, w0, w1, w2, a0, a1, a2, g1, g2, k_k, k_a, r_k):
    bsz, s, d = h.shape
    tm = min(RWKV_ROW_TILE, s)
    lora = w1.shape[-1]
    glora = g1.shape[-1]
    w1c = jnp.concatenate([w1[0], w1[1]], axis=1).astype(BF16)
    a1c = jnp.concatenate([a1[0], a1[1]], axis=1).astype(BF16)
    keep = (jnp.arange(2 * lora)[None, :, None] // lora) == jnp.arange(2)[:, None, None]
    w2p = jnp.where(keep, jnp.concatenate([w2, w2], axis=1), 0.0).astype(BF16)
    a2p = jnp.where(keep, jnp.concatenate([a2, a2], axis=1), 0.0).astype(BF16)
    kern = functools.partial(_rwkv_prep_kernel, tm=tm)
    tok = pl.BlockSpec((1, tm, d), lambda b, i: (b, i, 0))
    tok2 = pl.BlockSpec((2, 1, tm, d), lambda b, i: (0, b, i, 0))
    sd = jax.ShapeDtypeStruct
    return pl.pallas_call(
        kern,
        grid=(bsz, s // tm),
        in_specs=_halo_specs(tm, s, d) + [
            _full((1, d)), _full((6, d)), _full((3, d, d)),
            _full((d, 2 * lora)), _full((d, 2 * lora)), _full((d, glora)),
            _full((2, 2 * lora, d)), _full((2, 2 * lora, d)), _full((glora, d)),
            _full((2, d)), _full((2, d)), _full((1, d)), _full((1, d)), _full((1, d)),
            _full((d, d))],
        out_specs=[tok, tok, tok, tok, tok, tok2, tok2, tok2],
        out_shape=[sd((bsz, s, d), BF16)] * 5 + [sd((2, bsz, s, d), F32),
                                                 sd((2, bsz, s, d), BF16),
                                                 sd((2, bsz, s, d), BF16)],
        scratch_shapes=[pltpu.VMEM((tm + 2 * HALO, d), F32)],
        compiler_params=_params("parallel", "parallel"),
        name="rwkv_prep",
    )(h, h, h, g.reshape(1, d), mu, w_rkv.astype(BF16), w1c, a1c, g1.astype(BF16),
      w2p, a2p, g2.astype(BF16), w0, a0,
      k_k.reshape(1, d), k_a.reshape(1, d), r_k.reshape(1, d),
      _head_sum_matrix(d, RWKV_HEAD_DIM))


def _wkv_kernel(rf_ref, vf_ref, knf_ref, rr_ref, vr_ref, knr_ref,
                lwf_ref, kdf_ref, bdf_ref, lwr_ref, kdr_ref, bdr_ref,
                yf_ref, yr_ref, qf_s, qr_s, *, c, ng):
    hd = RWKV_HEAD_DIM
    w = 2 * hd

    @pl.when(pl.program_id(2) == 0)
    def _():
        qf_s[...] = jnp.zeros_like(qf_s)
        qr_s[...] = jnp.zeros_like(qr_s)

    ti = lax.broadcasted_iota(I32, (c, c), 0)
    si = lax.broadcasted_iota(I32, (c, c), 1)
    lane = lax.broadcasted_iota(I32, (1, w), 1)
    m_lo = lane < hd
    ti2 = lax.broadcasted_iota(I32, (c, 2 * c), 0)
    si2 = lax.broadcasted_iota(I32, (c, 2 * c), 1)
    si2 = jnp.where(si2 >= c, si2 - c, si2)
    eye2 = jnp.where(si2 == ti2, 1.0, 0.0)
    dir_masks = (
        (jnp.where(si <= ti, 1.0, 0.0).astype(BF16), si2 <= ti2, si2 < ti2),
        (jnp.where(si >= ti, 1.0, 0.0).astype(BF16), si2 >= ti2, si2 > ti2),
    )
    rr = lax.broadcasted_iota(I32, (2 * c, 2 * c), 0)
    cc = lax.broadcasted_iota(I32, (2 * c, 2 * c), 1)
    bd_mask_c = jnp.where(rr < c, 0, 1) == jnp.where(cc < c, 0, 1)
    rr = lax.broadcasted_iota(I32, (w, w), 0)
    cc = lax.broadcasted_iota(I32, (w, w), 1)
    bd_mask_h = jnp.where(rr < hd, 0, 1) == jnp.where(cc < hd, 0, 1)

    def row_stack(x):
        return jnp.concatenate([jnp.where(m_lo, x, 0.0), jnp.where(m_lo, 0.0, x)],
                               axis=0).astype(BF16)

    def block_diag(xp):
        return jnp.where(bd_mask_c, jnp.concatenate([xp, xp], axis=0), 0.0).astype(BF16)

    fwd = (0, rf_ref, vf_ref, knf_ref, lwf_ref, kdf_ref, bdf_ref)
    rev = (1, rr_ref, vr_ref, knr_ref, lwr_ref, kdr_ref, bdr_ref)
    probs = [fwd + (ci,) for ci in range(ng)] + [rev + (ci,) for ci in range(ng - 1, -1, -1)]

    st = []
    for dirn, r_ref, v_ref, kn_ref, lw_ref, kd_ref, bd_ref, ci in probs:
        tri_incl, _, _ = dir_masks[dirn]
        sl = pl.ds(ci * c, c)
        lw = lw_ref[0, 0, sl, :]
        lw_hi = lw.astype(BF16)
        lw_lo = (lw - lw_hi.astype(F32)).astype(BF16)
        l_incl = _dot(tri_incl, lw_hi) + _dot(tri_incl, lw_lo)
        st.append(dict(dirn=dirn, sl=sl, lw=lw, l_incl=l_incl,
                       r=r_ref[0, sl, :].astype(F32), v=v_ref[0, sl, :].astype(F32),
                       kn=kn_ref[0, sl, :].astype(F32), kd=kd_ref[0, 0, sl, :].astype(F32),
                       bd=bd_ref[0, 0, sl, :].astype(F32)))

    for p in st:
        l_incl = p["l_incl"]
        l_tot = l_incl[0:1, :] if p["dirn"] else l_incl[c - 1:c, :]
        g_inv = jnp.exp(-l_incl)
        g_end = jnp.exp(l_tot - l_incl)
        p["g_tot"] = jnp.exp(l_tot)
        p["rt"] = p["r"] * jnp.exp(l_incl)
        p["at"] = -p["kn"] * jnp.exp(l_incl - p["lw"])
        p["v_rs"] = row_stack(p["v"])
        p["bh"] = (p["bd"] * g_end).astype(BF16)
        p["kh"] = (p["kd"] * g_end).astype(BF16)
        lhs = jnp.concatenate([p["at"], p["rt"]], axis=0).astype(BF16)
        rhs = jnp.concatenate([row_stack(p["bd"] * g_inv), row_stack(p["kd"] * g_inv)],
                              axis=0)
        p["gm"] = _dot_nt(lhs, rhs)

    for p in st:
        _, incl2, strict2 = dir_masks[p["dirn"]]
        gm = p.pop("gm")
        a_ab = jnp.where(strict2, gm[0:c, 0:2 * c], 0.0)
        a_ak = jnp.where(strict2, gm[0:c, 2 * c:4 * c], 0.0)
        a_rb = jnp.where(incl2, gm[c:2 * c, 0:2 * c], 0.0)
        a_rk = jnp.where(incl2, gm[c:2 * c, 2 * c:4 * c], 0.0)
        p["a_rb"] = a_rb.astype(BF16)
        av = _dot(jnp.concatenate([a_ak, a_rk], axis=0).astype(BF16), p["v_rs"])
        p["akv"] = av[0:c]
        p["arkv"] = av[c:2 * c]
        p["t_p"] = eye2 + a_ab
        p["pw"] = _dot(a_ab.astype(BF16), block_diag(a_ab))

    for _ in range(max(c.bit_length() - 3, 0)):
        for p in st:
            res = _dot(jnp.concatenate([p["pw"], p["t_p"]], axis=0).astype(BF16),
                       block_diag(p["pw"]))
            p["pw"] = res[0:c]
            p["t_p"] = p["t_p"] + res[c:2 * c]

    for p in st:
        t_p = p["t_p"] + _dot(p["t_p"].astype(BF16), block_diag(p["pw"]))
        wu = _dot(t_p.astype(BF16),
                  jnp.concatenate([row_stack(p["at"]), row_stack(p["akv"])], axis=1))
        p["w_m"] = wu[:, 0:w]
        p["u_t"] = wu[:, w:2 * w]

    for p in st:
        ry = _dot(p["a_rb"],
                  jnp.concatenate([row_stack(p["w_m"]), row_stack(p["u_t"])], axis=1))
        p["r_hat"] = (p["rt"] + ry[:, 0:w]).astype(BF16)
        p["y0"] = ry[:, w:2 * w] + p["arkv"]
        p["n_m"] = jnp.where(bd_mask_h, _dot_tn(p["w_m"].astype(BF16), p["bh"]),
                             0.0).astype(BF16)
        p["q0"] = jnp.where(
            bd_mask_h,
            _dot_tn(jnp.concatenate([p["u_t"], p["v"]], axis=0).astype(BF16),
                    jnp.concatenate([p["bh"], p["kh"]], axis=0)), 0.0)

    q = [qf_s[...], qr_s[...]]
    y_refs = (yf_ref, yr_ref)
    for i in range(ng):
        for p in (st[i], st[ng + i]):
            dirn = p["dirn"]
            q_b = q[dirn].astype(BF16)
            y_refs[dirn][0, p["sl"], :] = _dot_nt(p["r_hat"], q_b) + p["y0"]
            q[dirn] = q[dirn] * p["g_tot"] + _dot(q_b, p["n_m"]) + p["q0"]
    qf_s[...] = q[0]
    qr_s[...] = q[1]


def _wkv(r, v, kn, lw, kd, bd):
    bsz, s, d = r.shape
    c = min(CHUNK, s)
    ng = min(CHUNK_GROUP, s // c)
    cg = c * ng
    n_steps = s // cg
    w = 2 * RWKV_HEAD_DIM
    tok_f = pl.BlockSpec((1, cg, w), lambda b, p, j: (b, j, p))
    tok_r = pl.BlockSpec((1, cg, w), lambda b, p, j: (b, n_steps - 1 - j, p))
    dir_f = pl.BlockSpec((1, 1, cg, w), lambda b, p, j: (0, b, j, p))
    dir_r = pl.BlockSpec((1, 1, cg, w), lambda b, p, j: (1, b, n_steps - 1 - j, p))
    kern = functools.partial(_wkv_kernel, c=c, ng=ng)
    return pl.pallas_call(
        kern,
        grid=(bsz, d // w, n_steps),
        in_specs=[tok_f, tok_f, tok_f, tok_r, tok_r, tok_r,
                  dir_f, dir_f, dir_f, dir_r, dir_r, dir_r],
        out_specs=[tok_f, tok_r],
        out_shape=[jax.ShapeDtypeStruct((bsz, s, d), F32)] * 2,
        scratch_shapes=[pltpu.VMEM((w, w), F32), pltpu.VMEM((w, w), F32)],
        compiler_params=_params("parallel", "parallel", "arbitrary"),
        name="wkv",
    )(r, v, kn, r, v, kn, lw, kd, bd, lw, kd, bd)


def _rwkv_post_kernel(h_ref, yf_ref, yr_ref, bonus_ref, gate_ref, lnw_ref, lnb_ref, hs_ref,
                      wout_ref, o_ref, *, n):
    hsum = hs_ref[...]
    y = yf_ref[0] + yr_ref[0]
    y_hi = y.astype(BF16)
    y_lo = (y - y_hi.astype(F32)).astype(BF16)
    mean = (_dot(y_hi, hsum) + _dot(y_lo, hsum)) * (1.0 / n)
    yc = y - mean
    var = _dot((yc * yc).astype(BF16), hsum) * (1.0 / n)
    yn = yc * lax.rsqrt(var + GN_EPS) * lnw_ref[...] + lnb_ref[...]
    yn = yn + bonus_ref[0].astype(F32)
    z = (yn * gate_ref[0].astype(F32)).astype(BF16)
    o_ref[0] = h_ref[0] + _dot(z, wout_ref[...])


def _rwkv_post(h, yf, yr, bonus, gate, ln_w, ln_b, w_out):
    bsz, s, d = h.shape
    tm = min(ROW_TILE, s)
    tok = pl.BlockSpec((1, tm, d), lambda b, i: (b, i, 0))
    kern = functools.partial(_rwkv_post_kernel, n=RWKV_HEAD_DIM)
    return pl.pallas_call(
        kern,
        grid=(bsz, s // tm),
        in_specs=[tok, tok, tok, tok, tok, _full((1, d)), _full((1, d)), _full((d, d)),
                  _full((d, d))],
        out_specs=tok,
        out_shape=jax.ShapeDtypeStruct((bsz, s, d), F32),
        compiler_params=_params("parallel", "parallel"),
        name="rwkv_post",
    )(h, yf, yr, bonus, gate, ln_w.reshape(1, d), ln_b.reshape(1, d),
      _head_sum_matrix(d, RWKV_HEAD_DIM), w_out.astype(BF16))


def kernel(x, mem, norm_mix, norm_xattn, norm_mem, norm_ffn, norm_final,
           conv_w_in, conv_w, conv_w_out,
           rwkv_mu, rwkv_w_rkv, rwkv_w0, rwkv_w1, rwkv_w2, rwkv_a0, rwkv_a1, rwkv_a2,
           rwkv_g1, rwkv_g2, rwkv_k_k, rwkv_k_a, rwkv_r_k, rwkv_ln_w, rwkv_ln_b, rwkv_w_out,
           xattn_w_q, xattn_w_kv, xattn_w_o,
           moe_router, moe_w_gate, moe_w_up, moe_w_down):
    depth = norm_mix.shape[0]
    bsz, s, d = x.shape
    n_exp = moe_router.shape[-1]
    cap = CAPACITY_FACTOR * s // n_exp
    h = x
    for i in range(depth):
        j = i // N_MIXERS
        if i % N_MIXERS == 0:
            h = _conv_layer(h, norm_mix[i], conv_w_in[j], conv_w[j], conv_w_out[j])
        else:
            r, v, kn, gate, bonus, lw, kd, bd = _rwkv_prep(
                h, norm_mix[i], rwkv_mu[j], rwkv_w_rkv[j], rwkv_w0[j], rwkv_w1[j], rwkv_w2[j],
                rwkv_a0[j], rwkv_a1[j], rwkv_a2[j], rwkv_g1[j], rwkv_g2[j],
                rwkv_k_k[j], rwkv_k_a[j], rwkv_r_k[j])
            yf, yr = _wkv(r, v, kn, lw, kd, bd)
            h = _rwkv_post(h, yf, yr, bonus, gate, rwkv_ln_w[j], rwkv_ln_b[j], rwkv_w_out[j])
        kv = _kv_proj(mem, norm_mem[i], xattn_w_kv[i])
        h, xn, probs_t = _xattn_layer(h, kv, norm_xattn[i], xattn_w_q[i], xattn_w_o[i],
                                      norm_ffn[i], moe_router[i])
        code = _select(probs_t, cap)
        ye = _expert_ffn(xn, code, moe_w_gate[i].astype(BF16), moe_w_up[i].astype(BF16),
                         moe_w_down[i].astype(BF16), cap)
        h = _moe_combine(h, jnp.swapaxes(code, 1, 2), jnp.swapaxes(probs_t, 1, 2), ye,
                         norm_final, final_norm=(i == depth - 1))
    return h
```

```python
import functools

import jax
import jax.numpy as jnp
from jax import lax
from jax.experimental import pallas as pl
from jax.experimental.pallas import tpu as pltpu

F32 = jnp.float32
BF16 = jnp.bfloat16
I32 = jnp.int32

N_MIXERS = 2
RWKV_HEAD_DIM = 64
XATTN_HEADS = 4
CAPACITY_FACTOR = 2
GN_EPS = 64e-5
RMS_EPS = 1e-6
DECAY_SCALE = 0.6065306597126334

V7X_VMEM_LIMIT_BYTES = 56 * 1024 * 1024
LANES = 128
BF16_SUBLANES = 16

ROW_TILE = 512
RWKV_ROW_TILE = 256
HALO = BF16_SUBLANES
CHUNK = 64
CHUNK_GROUP = 8
TOKEN_BLOCK = 512
PICK_WINDOW = 128


def _params(*semantics):
    return pltpu.CompilerParams(dimension_semantics=semantics,
                                vmem_limit_bytes=V7X_VMEM_LIMIT_BYTES)


def _rms(x, g):
    return x * lax.rsqrt(jnp.mean(x * x, axis=-1, keepdims=True) + RMS_EPS) * g


def _dot(a, b):
    return jnp.dot(a, b, preferred_element_type=F32)


def _dot_nt(a, b):
    return lax.dot_general(a, b, (((1,), (1,)), ((), ())), preferred_element_type=F32)


def _dot_tn(a, b):
    return lax.dot_general(a, b, (((0,), (0,)), ((), ())), preferred_element_type=F32)


def _full(shape):
    n = len(shape)
    return pl.BlockSpec(shape, lambda *_: (0,) * n)


def _conv_kernel(h_ref, hp_ref, hn_ref, g_ref, win_ref, cw_ref, wout_ref, o_ref,
                 xn_s, u_s, gate_s, *, tm, d, cb):
    i = pl.program_id(1)
    g = g_ref[...]
    x = h_ref[0]
    xn_s[0:HALO, :] = _rms(hp_ref[0], g).astype(BF16)
    xn_s[HALO:HALO + tm, :] = _rms(x, g).astype(BF16)
    xn_s[HALO + tm:, :] = _rms(hn_ref[0], g).astype(BF16)
    rows = tm + 2 * HALO
    row = lax.broadcasted_iota(I32, (rows, 1), 0)
    lo = jnp.where(i == 0, HALO, 0)
    hi = jnp.where(i == pl.num_programs(1) - 1, HALO + tm, rows)
    pad = jnp.logical_or(row < lo, row >= hi)
    xa = xn_s[...]
    for c0 in range(0, d, cb):
        c_gate = _dot(xa, win_ref[:, d + c0:d + c0 + cb])
        hx = _dot(xa, win_ref[:, 2 * d + c0:2 * d + c0 + cb])
        u_s[...] = jnp.where(pad, 0.0, c_gate * hx)
        conv = (u_s[pl.ds(HALO - 1, tm), :] * cw_ref[0:1, c0:c0 + cb]
                + u_s[pl.ds(HALO, tm), :] * cw_ref[1:2, c0:c0 + cb]
                + u_s[pl.ds(HALO + 1, tm), :] * cw_ref[2:3, c0:c0 + cb])
        b_gate = _dot(xn_s[HALO:HALO + tm, :], win_ref[:, c0:c0 + cb])
        gate_s[:, c0:c0 + cb] = (b_gate * conv).astype(BF16)
    o_ref[0] = x + _dot(gate_s[...], wout_ref[...])


def _halo_specs(tm, s, d):
    nb = tm // HALO
    last = s // HALO - 1
    return [
        pl.BlockSpec((1, tm, d), lambda b, i: (b, i, 0)),
        pl.BlockSpec((1, HALO, d), lambda b, i: (b, jnp.maximum(i * nb - 1, 0), 0)),
        pl.BlockSpec((1, HALO, d), lambda b, i: (b, jnp.minimum((i + 1) * nb, last), 0)),
    ]


def _conv_layer(h, g, w_in, conv_w, w_out):
    bsz, s, d = h.shape
    tm = min(ROW_TILE, s)
    cb = 512
    kern = functools.partial(_conv_kernel, tm=tm, d=d, cb=cb)
    return pl.pallas_call(
        kern,
        grid=(bsz, s // tm),
        in_specs=_halo_specs(tm, s, d) + [
            _full((1, d)), _full((d, 3 * d)), _full((3, d)), _full((d, d))],
        out_specs=pl.BlockSpec((1, tm, d), lambda b, i: (b, i, 0)),
        out_shape=jax.ShapeDtypeStruct((bsz, s, d), F32),
        scratch_shapes=[pltpu.VMEM((tm + 2 * HALO, d), BF16),
                        pltpu.VMEM((tm + 2 * HALO, cb), F32),
                        pltpu.VMEM((tm, d), BF16)],
        compiler_params=_params("parallel", "parallel"),
        name="conv_mixer",
    )(h, h, h, g.reshape(1, d), w_in.astype(BF16), conv_w, w_out.astype(BF16))


def _kv_kernel(m_ref, g_ref, w_ref, o_ref):
    xn = _rms(m_ref[0], g_ref[...]).astype(BF16)
    o_ref[0] = _dot(xn, w_ref[...]).astype(BF16)


def _kv_proj(mem, g, w_kv):
    bsz, m, d = mem.shape
    return pl.pallas_call(
        _kv_kernel,
        grid=(bsz,),
        in_specs=[pl.BlockSpec((1, m, d), lambda b: (b, 0, 0)), _full((1, d)), _full((d, 2 * d))],
        out_specs=pl.BlockSpec((1, m, 2 * d), lambda b: (b, 0, 0)),
        out_shape=jax.ShapeDtypeStruct((bsz, m, 2 * d), BF16),
        compiler_params=_params("parallel"),
        name="kv_proj",
    )(mem, g.reshape(1, d), w_kv.astype(BF16))


def _xattn_kernel(h_ref, gx_ref, wq_ref, kv_ref, wo_ref, gf_ref, rhi_ref, rlo_ref,
                  h_out, xn_out, probs_out, o_s, *, d, heads):
    x = h_ref[0]
    xn = _rms(x, gx_ref[...]).astype(BF16)
    hd = d // heads
    q = (_dot(xn, wq_ref[...]) * (hd ** -0.5)).astype(BF16)
    kv = kv_ref[0]
    for a in range(heads):
        sc = _dot_nt(q[:, a * hd:(a + 1) * hd], kv[:, a * hd:(a + 1) * hd])
        p = jnp.exp(sc - jnp.max(sc, axis=-1, keepdims=True))
        l = jnp.sum(p, axis=-1, keepdims=True)
        o = _dot(p.astype(BF16), kv[:, d + a * hd:d + (a + 1) * hd]) / l
        o_s[:, a * hd:(a + 1) * hd] = o.astype(BF16)
    hn = x + _dot(o_s[...], wo_ref[...])
    h_out[0] = hn
    xf = _rms(hn, gf_ref[...])
    hi = xf.astype(BF16)
    xn_out[0] = hi
    lo = (xf - hi.astype(F32)).astype(BF16)
    lg = _dot_nt(rhi_ref[...], hi) + _dot_nt(rhi_ref[...], lo) + _dot_nt(rlo_ref[...], hi)
    e = jnp.exp(lg - jnp.max(lg, axis=0, keepdims=True))
    probs_out[0] = e / jnp.sum(e, axis=0, keepdims=True)


def _xattn_layer(h, kv, gx, w_q, w_o, gf, router):
    bsz, s, d = h.shape
    m = kv.shape[1]
    e = router.shape[1]
    tm = min(ROW_TILE, s)
    rt = router.T
    r_hi = rt.astype(BF16)
    r_lo = (rt - r_hi.astype(F32)).astype(BF16)
    kern = functools.partial(_xattn_kernel, d=d, heads=XATTN_HEADS)
    return pl.pallas_call(
        kern,
        grid=(bsz, s // tm),
        in_specs=[pl.BlockSpec((1, tm, d), lambda b, i: (b, i, 0)),
                  _full((1, d)), _full((d, d)),
                  pl.BlockSpec((1, m, 2 * d), lambda b, i: (b, 0, 0)),
                  _full((d, d)), _full((1, d)), _full((e, d)), _full((e, d))],
        out_specs=[pl.BlockSpec((1, tm, d), lambda b, i: (b, i, 0)),
                   pl.BlockSpec((1, tm, d), lambda b, i: (b, i, 0)),
                   pl.BlockSpec((1, e, tm), lambda b, i: (b, 0, i))],
        out_shape=[jax.ShapeDtypeStruct((bsz, s, d), F32),
                   jax.ShapeDtypeStruct((bsz, s, d), BF16),
                   jax.ShapeDtypeStruct((bsz, e, s), F32)],
        scratch_shapes=[pltpu.VMEM((tm, d), BF16)],
        compiler_params=_params("parallel", "parallel"),
        name="xattn_router",
    )(h, gx.reshape(1, d), w_q.astype(BF16), kv, w_o.astype(BF16), gf.reshape(1, d), r_hi, r_lo)


def _select_kernel(p_ref, code_ref, start_ref, *, cap, blk):
    p = p_ref[0]
    e, s = p.shape
    bits = pltpu.bitcast(p, I32)

    def count(mask):
        return jnp.sum(mask.astype(F32), axis=1, keepdims=True)

    def search(k, t):
        cand = jnp.bitwise_or(t, jnp.left_shift(jnp.int32(1), 30 - k))
        return jnp.where(count(bits >= cand) >= cap, cand, t)

    thr = lax.fori_loop(0, 31, search, jnp.zeros((e, 1), I32))
    gt = bits > thr
    eq = bits == thr
    need = cap - count(gt)

    r = lax.broadcasted_iota(I32, (blk, blk), 0)
    c = lax.broadcasted_iota(I32, (blk, blk), 1)
    tri = jnp.where(r < c, 1.0, 0.0).astype(BF16)

    def prefix(mask_f):
        out, carries = [], []
        carry = jnp.zeros((e, 1), F32)
        for j in range(0, s, blk):
            mb = mask_f[:, j:j + blk]
            carries.append(carry)
            out.append(_dot(mb.astype(BF16), tri) + carry)
            carry = carry + jnp.sum(mb, axis=1, keepdims=True)
        return out, carries + [carry]

    eq_f = eq.astype(F32)
    tie_rank, _ = prefix(eq_f)
    for j0, tr in zip(range(0, s, blk), tie_rank):
        sel_b = jnp.logical_or(gt[:, j0:j0 + blk],
                               jnp.logical_and(eq[:, j0:j0 + blk], tr < need))
        code_ref[0, :, j0:j0 + blk] = sel_b.astype(I32)
    sel_f = code_ref[0].astype(F32)
    rank, starts = prefix(sel_f)
    for j0, rk in zip(range(0, s, blk), rank):
        code_ref[0, :, j0:j0 + blk] = jnp.where(sel_f[:, j0:j0 + blk] > 0.0, rk.astype(I32), -1)
    start_ref[...] = jnp.zeros_like(start_ref)
    for j, st in enumerate(starts):
        start_ref[0, :, j:j + 1] = st.astype(I32)


def _select(probs_t, cap):
    bsz, e, s = probs_t.shape
    blk = min(TOKEN_BLOCK, s)
    nb1 = s // blk + 1
    assert nb1 <= LANES
    kern = functools.partial(_select_kernel, cap=cap, blk=blk)
    code, starts = pl.pallas_call(
        kern,
        grid=(bsz,),
        in_specs=[pl.BlockSpec((1, e, s), lambda b: (b, 0, 0))],
        out_specs=[pl.BlockSpec((1, e, s), lambda b: (b, 0, 0)),
                   pl.BlockSpec((1, e, LANES), lambda b: (b, 0, 0))],
        out_shape=[jax.ShapeDtypeStruct((bsz, e, s), I32),
                   jax.ShapeDtypeStruct((bsz, e, LANES), I32)],
        compiler_params=_params("parallel"),
        name="expert_choice_select",
    )(probs_t)
    return code, starts[:, :, :nb1].reshape(-1)


def _window(s_lo, s_hi, win):
    s0 = jnp.bitwise_and(s_lo, -BF16_SUBLANES)
    n_win = jnp.where(s_hi > s_lo, lax.shift_right_logical(s_hi - s0 + (win - 1),
                                                           win.bit_length() - 1), 0)
    return s0, n_win


def _ffn_kernel(st_ref, x_ref, code_ref, wg_ref, wu_ref, wd_ref, y_ref, xe_s, *, cap, blk, win):
    k = pl.program_id(0)
    b = pl.program_id(1)
    s = x_ref.shape[1]
    nb = s // blk

    base = (b * pl.num_programs(0) + k) * (nb + 1)
    iota = lax.broadcasted_iota(I32, (win, blk), 0)

    def gather(j, row0):
        onehot = jnp.where(iota + row0 == code_ref[0, :, j * blk:(j + 1) * blk], 1.0, 0.0)
        part = _dot(onehot.astype(BF16), x_ref[0, j * blk:(j + 1) * blk, :])
        xe_s[pl.ds(row0, win), :] += part

    xe_s[...] = jnp.zeros_like(xe_s)
    wins = [_window(st_ref[base + j], st_ref[base + j + 1], win) for j in range(nb)]
    for j, (s0, _) in enumerate(wins):
        gather(j, pl.multiple_of(s0, BF16_SUBLANES))
    for j, (s0, n_win) in enumerate(wins):
        def more(i, carry, j=j, s0=s0):
            gather(j, pl.multiple_of(s0 + i * win, BF16_SUBLANES))
            return carry
        lax.fori_loop(1, n_win, more, 0)

    xe = xe_s[0:cap, :].astype(BF16)
    gt = _dot(xe, wg_ref[0, 0].astype(BF16))
    up = _dot(xe, wu_ref[0, 0].astype(BF16))
    hid = (gt * jax.nn.sigmoid(gt) * up).astype(BF16)
    y_ref[0, 0] = _dot(hid, wd_ref[0, 0].astype(BF16)).astype(BF16)


def _expert_ffn(xn, code, starts, w_gate, w_up, w_down, layer, cap):
    bsz, s, d = xn.shape
    _, e, _, f = w_gate.shape
    blk = min(TOKEN_BLOCK, s)
    win = min(PICK_WINDOW, cap)
    kern = functools.partial(_ffn_kernel, cap=cap, blk=blk, win=win)
    return pl.pallas_call(
        kern,
        grid_spec=pltpu.PrefetchScalarGridSpec(
            num_scalar_prefetch=1,
            grid=(e, bsz),
            in_specs=[pl.BlockSpec((1, s, d), lambda k, b, st: (b, 0, 0)),
                      pl.BlockSpec((1, 1, s), lambda k, b, st: (b * e + k, 0, 0)),
                      pl.BlockSpec((1, 1, d, f), lambda k, b, st: (layer, k, 0, 0)),
                      pl.BlockSpec((1, 1, d, f), lambda k, b, st: (layer, k, 0, 0)),
                      pl.BlockSpec((1, 1, f, d), lambda k, b, st: (layer, k, 0, 0))],
            out_specs=pl.BlockSpec((1, 1, cap, d), lambda k, b, st: (b, k, 0, 0)),
            scratch_shapes=[pltpu.VMEM((cap + win, d), F32)]),
        out_shape=jax.ShapeDtypeStruct((bsz, e, cap, d), BF16),
        compiler_params=_params("parallel", "parallel"),
        name="expert_ffn",
    )(starts, xn, code.reshape(bsz * e, 1, s), w_gate, w_up, w_down)


def _combine_kernel(st_ref, h_ref, code_ref, p_ref, y_ref, g_ref, o_ref, acc_s,
                    *, cap, win, final_norm):
    b = pl.program_id(0)
    j = pl.program_id(1)
    nb = pl.num_programs(1)
    tb = h_ref.shape[1]
    n_exp = y_ref.shape[1]
    iota = lax.broadcasted_iota(I32, (tb, win), 1)
    code = code_ref[0]
    prob = p_ref[0]

    def scatter(k, lo):
        row0 = pl.multiple_of(jnp.minimum(lo, cap - win), BF16_SUBLANES)
        code_k = jnp.where(code[:, k:k + 1] >= lo, code[:, k:k + 1], -1)
        onehot = jnp.where(iota + row0 == code_k, 1.0, 0.0).astype(BF16)
        return _dot(onehot, y_ref[0, k, pl.ds(row0, win), :]) * prob[:, k:k + 1]

    wins = []
    for k in range(n_exp):
        base = (b * n_exp + k) * (nb + 1) + j
        wins.append(_window(st_ref[base], st_ref[base + 1], win))
    acc = h_ref[0]
    for k, (s0, _) in enumerate(wins):
        acc = acc + scatter(k, s0)
    acc_s[...] = acc
    for k, (s0, n_win) in enumerate(wins):
        def more(i, carry, k=k, s0=s0):
            acc_s[...] += scatter(k, s0 + i * win)
            return carry
        lax.fori_loop(1, n_win, more, 0)
    acc = acc_s[...]
    o_ref[0] = _rms(acc, g_ref[...]) if final_norm else acc


def _moe_combine(h, code_t, probs, starts, ye, g_final, final_norm):
    bsz, s, d = h.shape
    e, cap = ye.shape[1], ye.shape[2]
    tb = min(TOKEN_BLOCK, s)
    win = min(PICK_WINDOW, cap)
    kern = functools.partial(_combine_kernel, cap=cap, win=win, final_norm=final_norm)
    return pl.pallas_call(
        kern,
        grid_spec=pltpu.PrefetchScalarGridSpec(
            num_scalar_prefetch=1,
            grid=(bsz, s // tb),
            in_specs=[pl.BlockSpec((1, tb, d), lambda b, j, st: (b, j, 0)),
                      pl.BlockSpec((1, tb, e), lambda b, j, st: (b, j, 0)),
                      pl.BlockSpec((1, tb, e), lambda b, j, st: (b, j, 0)),
                      pl.BlockSpec((1, e, cap, d), lambda b, j, st: (b, 0, 0, 0)),
                      pl.BlockSpec((1, d), lambda b, j, st: (0, 0))],
            out_specs=pl.BlockSpec((1, tb, d), lambda b, j, st: (b, j, 0)),
            scratch_shapes=[pltpu.VMEM((tb, d), F32)]),
        out_shape=jax.ShapeDtypeStruct((bsz, s, d), F32),
        compiler_params=_params("parallel", "parallel"),
        name="moe_combine",
    )(starts, h, code_t, probs, ye, g_final.reshape(1, d))


def _rwkv_prep_kernel(h_ref, hp_ref, hn_ref, g_ref, mu_ref, wrkv_ref, w1_ref, a1_ref, g1_ref,
                      w2_ref, a2_ref, g2_ref, w0_ref, a0_ref, kk_ref, ka_ref, rk_ref, hs_ref,
                      r_out, v_out, kn_out, gate_out, bonus_out, lw_out, kd_out, bd_out,
                      xn_s, *, tm):
    i = pl.program_id(1)
    g = g_ref[...]
    keep_prev = jnp.where(i == 0, 0.0, 1.0)
    keep_next = jnp.where(i == pl.num_programs(1) - 1, 0.0, 1.0)
    xn_s[0:HALO, :] = _rms(hp_ref[0], g) * keep_prev
    xn_s[HALO:HALO + tm, :] = _rms(h_ref[0], g)
    xn_s[HALO + tm:, :] = _rms(hn_ref[0], g) * keep_next
    xn = xn_s[pl.ds(HALO, tm), :]
    xx = 0.5 * (xn_s[pl.ds(HALO - 1, tm), :] + xn_s[pl.ds(HALO + 1, tm), :]) - xn

    def mix(j):
        return (xn + xx * mu_ref[j:j + 1, :]).astype(BF16)

    r = _dot(mix(0), wrkv_ref[0])
    k = _dot(mix(1), wrkv_ref[1])
    v = _dot(mix(2), wrkv_ref[2])
    hw = jnp.tanh(_dot(mix(3), w1_ref[...])).astype(BF16)
    ha = _dot(mix(4), a1_ref[...]).astype(BF16)
    hg = jax.nn.sigmoid(_dot(mix(5), g1_ref[...])).astype(BF16)
    gate_out[0] = _dot(hg, g2_ref[...]).astype(BF16)

    hsum = hs_ref[...]
    kk = k * kk_ref[...]
    nrm2 = _dot((kk * kk).astype(BF16), hsum)
    kn = kk / jnp.maximum(jnp.sqrt(nrm2), 1e-12)
    r_out[0] = r.astype(BF16)
    v_out[0] = v.astype(BF16)
    kn_out[0] = kn.astype(BF16)
    ksum = None
    for n in range(2):
        w_raw = w0_ref[n:n + 1, :] + _dot(hw, w2_ref[n])
        lw_out[n, 0] = -DECAY_SCALE * jax.nn.sigmoid(w_raw)
        a = jax.nn.sigmoid(a0_ref[n:n + 1, :] + _dot(ha, a2_ref[n]))
        kd = k * (1.0 + (a - 1.0) * ka_ref[...])
        kd_out[n, 0] = kd.astype(BF16)
        bd_out[n, 0] = (kn * a).astype(BF16)
        ksum = kd if ksum is None else ksum + kd
    coef = _dot((r * ksum * rk_ref[...]).astype(BF16), hsum)
    bonus_out[0] = (coef * v).astype(BF16)


def _head_sum_matrix(d, n):
    idx = jnp.arange(d) // n
    return (idx[:, None] == idx[None, :]).astype(BF16)


def _rwkv_prep(h, g, mu, w_rkv, w0, w1, w2, a0, a1, a2, g1, g2, k_k, k_a, r_k):
    bsz, s, d = h.shape
    tm = min(RWKV_ROW_TILE, s)
    lora = w1.shape[-1]
    glora = g1.shape[-1]
    w1c = jnp.concatenate([w1[0], w1[1]], axis=1).astype(BF16)
    a1c = jnp.concatenate([a1[0], a1[1]], axis=1).astype(BF16)
    keep = (jnp.arange(2 * lora)[None, :, None] // lora) == jnp.arange(2)[:, None, None]
    w2p = jnp.where(keep, jnp.concatenate([w2, w2], axis=1), 0.0).astype(BF16)
    a2p = jnp.where(keep, jnp.concatenate([a2, a2], axis=1), 0.0).astype(BF16)
    kern = functools.partial(_rwkv_prep_kernel, tm=tm)
    tok = pl.BlockSpec((1, tm, d), lambda b, i: (b, i, 0))
    tok2 = pl.BlockSpec((2, 1, tm, d), lambda b, i: (0, b, i, 0))
    sd = jax.ShapeDtypeStruct
    return pl.pallas_call(
        kern,
        grid=(bsz, s // tm),
        in_specs=_halo_specs(tm, s, d) + [
            _full((1, d)), _full((6, d)), _full((3, d, d)),
            _full((d, 2 * lora)), _full((d, 2 * lora)), _full((d, glora)),
            _full((2, 2 * lora, d)), _full((2, 2 * lora, d)), _full((glora, d)),
            _full((2, d)), _full((2, d)), _full((1, d)), _full((1, d)), _full((1, d)),
            _full((d, d))],
        out_specs=[tok, tok, tok, tok, tok, tok2, tok2, tok2],
        out_shape=[sd((bsz, s, d), BF16)] * 5 + [sd((2, bsz, s, d), F32),
                                                 sd((2, bsz, s, d), BF16),
                                                 sd((2, bsz, s, d), BF16)],
        scratch_shapes=[pltpu.VMEM((tm + 2 * HALO, d), F32)],
        compiler_params=_params("parallel", "parallel"),
        name="rwkv_prep",
    )(h, h, h, g.reshape(1, d), mu, w_rkv.astype(BF16), w1c, a1c, g1.astype(BF16),
      w2p, a2p, g2.astype(BF16), w0, a0,
      k_k.reshape(1, d), k_a.reshape(1, d), r_k.reshape(1, d),
      _head_sum_matrix(d, RWKV_HEAD_DIM))


def _wkv_kernel(rf_ref, vf_ref, knf_ref, rr_ref, vr_ref, knr_ref,
                lwf_ref, kdf_ref, bdf_ref, lwr_ref, kdr_ref, bdr_ref,
                yf_ref, yr_ref, qf_s, qr_s, *, c, ng):
    hd = RWKV_HEAD_DIM
    w = 2 * hd

    @pl.when(pl.program_id(2) == 0)
    def _():
        qf_s[...] = jnp.zeros_like(qf_s)
        qr_s[...] = jnp.zeros_like(qr_s)

    ti = lax.broadcasted_iota(I32, (c, c), 0)
    si = lax.broadcasted_iota(I32, (c, c), 1)
    lane = lax.broadcasted_iota(I32, (1, w), 1)
    m_lo = lane < hd
    ti2 = lax.broadcasted_iota(I32, (c, 2 * c), 0)
    si2 = lax.broadcasted_iota(I32, (c, 2 * c), 1)
    si2 = jnp.where(si2 >= c, si2 - c, si2)
    eye2 = jnp.where(si2 == ti2, 1.0, 0.0)
    dir_masks = (
        (jnp.where(si <= ti, 1.0, 0.0).astype(BF16), si2 <= ti2, si2 < ti2),
        (jnp.where(si >= ti, 1.0, 0.0).astype(BF16), si2 >= ti2, si2 > ti2),
    )
    rr = lax.broadcasted_iota(I32, (2 * c, 2 * c), 0)
    cc = lax.broadcasted_iota(I32, (2 * c, 2 * c), 1)
    bd_mask_c = jnp.where(rr < c, 0, 1) == jnp.where(cc < c, 0, 1)
    rr = lax.broadcasted_iota(I32, (w, w), 0)
    cc = lax.broadcasted_iota(I32, (w, w), 1)
    bd_mask_h = jnp.where(rr < hd, 0, 1) == jnp.where(cc < hd, 0, 1)

    def row_stack(x):
        return jnp.concatenate([jnp.where(m_lo, x, 0.0), jnp.where(m_lo, 0.0, x)],
                               axis=0).astype(BF16)

    def block_diag(xp):
        return jnp.where(bd_mask_c, jnp.concatenate([xp, xp], axis=0), 0.0).astype(BF16)

    fwd = (0, rf_ref, vf_ref, knf_ref, lwf_ref, kdf_ref, bdf_ref)
    rev = (1, rr_ref, vr_ref, knr_ref, lwr_ref, kdr_ref, bdr_ref)
    probs = [fwd + (ci,) for ci in range(ng)] + [rev + (ci,) for ci in range(ng - 1, -1, -1)]

    st = []
    for dirn, r_ref, v_ref, kn_ref, lw_ref, kd_ref, bd_ref, ci in probs:
        tri_incl, _, _ = dir_masks[dirn]
        sl = pl.ds(ci * c, c)
        lw = lw_ref[0, 0, sl, :]
        lw_hi = lw.astype(BF16)
        lw_lo = (lw - lw_hi.astype(F32)).astype(BF16)
        l_incl = _dot(tri_incl, lw_hi) + _dot(tri_incl, lw_lo)
        st.append(dict(dirn=dirn, sl=sl, lw=lw, l_incl=l_incl,
                       r=r_ref[0, sl, :].astype(F32), v=v_ref[0, sl, :].astype(F32),
                       kn=kn_ref[0, sl, :].astype(F32), kd=kd_ref[0, 0, sl, :].astype(F32),
                       bd=bd_ref[0, 0, sl, :].astype(F32)))

    for p in st:
        l_incl = p["l_incl"]
        l_tot = l_incl[0:1, :] if p["dirn"] else l_incl[c - 1:c, :]
        g_inv = jnp.exp(-l_incl)
        g_end = jnp.exp(l_tot - l_incl)
        p["g_tot"] = jnp.exp(l_tot)
        p["rt"] = p["r"] * jnp.exp(l_incl)
        p["at"] = -p["kn"] * jnp.exp(l_incl - p["lw"])
        p["v_rs"] = row_stack(p["v"])
        p["bh"] = (p["bd"] * g_end).astype(BF16)
        p["kh"] = (p["kd"] * g_end).astype(BF16)
        lhs = jnp.concatenate([p["at"], p["rt"]], axis=0).astype(BF16)
        rhs = jnp.concatenate([row_stack(p["bd"] * g_inv), row_stack(p["kd"] * g_inv)],
                              axis=0)
        p["gm"] = _dot_nt(lhs, rhs)

    for p in st:
        _, incl2, strict2 = dir_masks[p["dirn"]]
        gm = p.pop("gm")
        a_ab = jnp.where(strict2, gm[0:c, 0:2 * c], 0.0)
        a_ak = jnp.where(strict2, gm[0:c, 2 * c:4 * c], 0.0)
        a_rb = jnp.where(incl2, gm[c:2 * c, 0:2 * c], 0.0)
        a_rk = jnp.where(incl2, gm[c:2 * c, 2 * c:4 * c], 0.0)
        p["a_rb"] = a_rb.astype(BF16)
        av = _dot(jnp.concatenate([a_ak, a_rk], axis=0).astype(BF16), p["v_rs"])
        p["akv"] = av[0:c]
        p["arkv"] = av[c:2 * c]
        p["t_p"] = eye2 + a_ab
        p["pw"] = _dot(a_ab.astype(BF16), block_diag(a_ab))

    for _ in range(max(c.bit_length() - 3, 0)):
        for p in st:
            res = _dot(jnp.concatenate([p["pw"], p["t_p"]], axis=0).astype(BF16),
                       block_diag(p["pw"]))
            p["pw"] = res[0:c]
            p["t_p"] = p["t_p"] + res[c:2 * c]

    for p in st:
        t_p = p["t_p"] + _dot(p["t_p"].astype(BF16), block_diag(p["pw"]))
        wu = _dot(t_p.astype(BF16),
                  jnp.concatenate([row_stack(p["at"]), row_stack(p["akv"])], axis=1))
        p["w_m"] = wu[:, 0:w]
        p["u_t"] = wu[:, w:2 * w]

    for p in st:
        ry = _dot(p["a_rb"],
                  jnp.concatenate([row_stack(p["w_m"]), row_stack(p["u_t"])], axis=1))
        p["r_hat"] = (p["rt"] + ry[:, 0:w]).astype(BF16)
        p["y0"] = ry[:, w:2 * w] + p["arkv"]
        p["n_m"] = jnp.where(bd_mask_h, _dot_tn(p["w_m"].astype(BF16), p["bh"]),
                             0.0).astype(BF16)
        p["q0"] = jnp.where(
            bd_mask_h,
            _dot_tn(jnp.concatenate([p["u_t"], p["v"]], axis=0).astype(BF16),
                    jnp.concatenate([p["bh"], p["kh"]], axis=0)), 0.0)

    q = [qf_s[...], qr_s[...]]
    y_refs = (yf_ref, yr_ref)
    for i in range(ng):
        for p in (st[i], st[ng + i]):
            dirn = p["dirn"]
            q_b = q[dirn].astype(BF16)
            y_refs[dirn][0, p["sl"], :] = _dot_nt(p["r_hat"], q_b) + p["y0"]
            q[dirn] = q[dirn] * p["g_tot"] + _dot(q_b, p["n_m"]) + p["q0"]
    qf_s[...] = q[0]
    qr_s[...] = q[1]


def _wkv(r, v, kn, lw, kd, bd):
    bsz, s, d = r.shape
    c = min(CHUNK, s)
    ng = min(CHUNK_GROUP, s // c)
    cg = c * ng
    n_steps = s // cg
    w = 2 * RWKV_HEAD_DIM
    tok_f = pl.BlockSpec((1, cg, w), lambda b, p, j: (b, j, p))
    tok_r = pl.BlockSpec((1, cg, w), lambda b, p, j: (b, n_steps - 1 - j, p))
    dir_f = pl.BlockSpec((1, 1, cg, w), lambda b, p, j: (0, b, j, p))
    dir_r = pl.BlockSpec((1, 1, cg, w), lambda b, p, j: (1, b, n_steps - 1 - j, p))
    kern = functools.partial(_wkv_kernel, c=c, ng=ng)
    return pl.pallas_call(
        kern,
        grid=(bsz, d // w, n_steps),
        in_specs=[tok_f, tok_f, tok_f, tok_r, tok_r, tok_r,
                  dir_f, dir_f, dir_f, dir_r, dir_r, dir_r],
        out_specs=[tok_f, tok_r],
        out_shape=[jax.ShapeDtypeStruct((bsz, s, d), F32)] * 2,
        scratch_shapes=[pltpu.VMEM((w, w), F32), pltpu.VMEM((w, w), F32)],
        compiler_params=_params("parallel", "parallel", "arbitrary"),
        name="wkv",
    )(r, v, kn, r, v, kn, lw, kd, bd, lw, kd, bd)


def _rwkv_post_kernel(h_ref, yf_ref, yr_ref, bonus_ref, gate_ref, lnw_ref, lnb_ref, hs_ref,
                      wout_ref, o_ref, *, n):
    hsum = hs_ref[...]
    y = yf_ref[0] + yr_ref[0]
    y_hi = y.astype(BF16)
    y_lo = (y - y_hi.astype(F32)).astype(BF16)
    mean = (_dot(y_hi, hsum) + _dot(y_lo, hsum)) * (1.0 / n)
    yc = y - mean
    var = _dot((yc * yc).astype(BF16), hsum) * (1.0 / n)
    yn = yc * lax.rsqrt(var + GN_EPS) * lnw_ref[...] + lnb_ref[...]
    yn = yn + bonus_ref[0].astype(F32)
    z = (yn * gate_ref[0].astype(F32)).astype(BF16)
    o_ref[0] = h_ref[0] + _dot(z, wout_ref[...])


def _rwkv_post(h, yf, yr, bonus, gate, ln_w, ln_b, w_out):
    bsz, s, d = h.shape
    tm = min(ROW_TILE, s)
    tok = pl.BlockSpec((1, tm, d), lambda b, i: (b, i, 0))
    kern = functools.partial(_rwkv_post_kernel, n=RWKV_HEAD_DIM)
    return pl.pallas_call(
        kern,
        grid=(bsz, s // tm),
        in_specs=[tok, tok, tok, tok, tok, _full((1, d)), _full((1, d)), _full((d, d)),
                  _full((d, d))],
        out_specs=tok,
        out_shape=jax.ShapeDtypeStruct((bsz, s, d), F32),
        compiler_params=_params("parallel", "parallel"),
        name="rwkv_post",
    )(h, yf, yr, bonus, gate, ln_w.reshape(1, d), ln_b.reshape(1, d),
      _head_sum_matrix(d, RWKV_HEAD_DIM), w_out.astype(BF16))


def kernel(x, mem, norm_mix, norm_xattn, norm_mem, norm_ffn, norm_final,
           conv_w_in, conv_w, conv_w_out,
           rwkv_mu, rwkv_w_rkv, rwkv_w0, rwkv_w1, rwkv_w2, rwkv_a0, rwkv_a1, rwkv_a2,
           rwkv_g1, rwkv_g2, rwkv_k_k, rwkv_k_a, rwkv_r_k, rwkv_ln_w, rwkv_ln_b, rwkv_w_out,
           xattn_w_q, xattn_w_kv, xattn_w_o,
           moe_router, moe_w_gate, moe_w_up, moe_w_down):
    depth = norm_mix.shape[0]
    bsz, s, d = x.shape
    n_exp = moe_router.shape[-1]
    cap = CAPACITY_FACTOR * s // n_exp
    h = x
    for i in range(depth):
        j = i // N_MIXERS
        if i % N_MIXERS == 0:
            h = _conv_layer(h, norm_mix[i], conv_w_in[j], conv_w[j], conv_w_out[j])
        else:
            r, v, kn, gate, bonus, lw, kd, bd = _rwkv_prep(
                h, norm_mix[i], rwkv_mu[j], rwkv_w_rkv[j], rwkv_w0[j], rwkv_w1[j], rwkv_w2[j],
                rwkv_a0[j], rwkv_a1[j], rwkv_a2[j], rwkv_g1[j], rwkv_g2[j],
                rwkv_k_k[j], rwkv_k_a[j], rwkv_r_k[j])
            yf, yr = _wkv(r, v, kn, lw, kd, bd)
            h = _rwkv_post(h, yf, yr, bonus, gate, rwkv_ln_w[j], rwkv_ln_b[j], rwkv_w_out[j])
        kv = _kv_proj(mem, norm_mem[i], xattn_w_kv[i])
        h, xn, probs_t = _xattn_layer(h, kv, norm_xattn[i], xattn_w_q[i], xattn_w_o[i],
                                      norm_ffn[i], moe_router[i])
        code, starts = _select(probs_t, cap)
        ye = _expert_ffn(xn, code, starts, moe_w_gate, moe_w_up, moe_w_down, i, cap)
        h = _moe_combine(h, jnp.swapaxes(code, 1, 2), jnp.swapaxes(probs_t, 1, 2), starts, ye,
                         norm_final, final_norm=(i == depth - 1))
    return h
```

```python
import functools

import jax
import jax.numpy as jnp
from jax import lax
from jax.experimental import pallas as pl
from jax.experimental.pallas import tpu as pltpu

F32 = jnp.float32
BF16 = jnp.bfloat16
I32 = jnp.int32

N_MIXERS = 2
RWKV_HEAD_DIM = 64
XATTN_HEADS = 4
CAPACITY_FACTOR = 2
GN_EPS = 64e-5
RMS_EPS = 1e-6
DECAY_SCALE = 0.6065306597126334

V7X_VMEM_LIMIT_BYTES = 56 * 1024 * 1024
LANES = 128
BF16_SUBLANES = 16

ROW_TILE = 512
RWKV_ROW_TILE = 256
HALO = BF16_SUBLANES
CHUNK = 64
CHUNK_GROUP = 8
TOKEN_BLOCK = 512
PICK_WINDOW = 128


def _params(*semantics):
    return pltpu.CompilerParams(dimension_semantics=semantics,
                                vmem_limit_bytes=V7X_VMEM_LIMIT_BYTES)


def _rms(x, g):
    return x * lax.rsqrt(jnp.mean(x * x, axis=-1, keepdims=True) + RMS_EPS) * g


def _dot(a, b):
    return jnp.dot(a, b, preferred_element_type=F32)


def _dot_nt(a, b):
    return lax.dot_general(a, b, (((1,), (1,)), ((), ())), preferred_element_type=F32)


def _dot_tn(a, b):
    return lax.dot_general(a, b, (((0,), (0,)), ((), ())), preferred_element_type=F32)


def _full(shape):
    n = len(shape)
    return pl.BlockSpec(shape, lambda *_: (0,) * n)


def _conv_kernel(h_ref, hp_ref, hn_ref, g_ref, win_ref, cw_ref, wout_ref, o_ref,
                 xn_s, u_s, gate_s, *, tm, d, cb):
    i = pl.program_id(1)
    g = g_ref[...]
    x = h_ref[0]
    xn_s[0:HALO, :] = _rms(hp_ref[0], g).astype(BF16)
    xn_s[HALO:HALO + tm, :] = _rms(x, g).astype(BF16)
    xn_s[HALO + tm:, :] = _rms(hn_ref[0], g).astype(BF16)
    rows = tm + 2 * HALO
    row = lax.broadcasted_iota(I32, (rows, 1), 0)
    lo = jnp.where(i == 0, HALO, 0)
    hi = jnp.where(i == pl.num_programs(1) - 1, HALO + tm, rows)
    pad = jnp.logical_or(row < lo, row >= hi)
    xa = xn_s[...]
    for c0 in range(0, d, cb):
        c_gate = _dot(xa, win_ref[:, d + c0:d + c0 + cb])
        hx = _dot(xa, win_ref[:, 2 * d + c0:2 * d + c0 + cb])
        u_s[...] = jnp.where(pad, 0.0, c_gate * hx)
        conv = (u_s[pl.ds(HALO - 1, tm), :] * cw_ref[0:1, c0:c0 + cb]
                + u_s[pl.ds(HALO, tm), :] * cw_ref[1:2, c0:c0 + cb]
                + u_s[pl.ds(HALO + 1, tm), :] * cw_ref[2:3, c0:c0 + cb])
        b_gate = _dot(xn_s[HALO:HALO + tm, :], win_ref[:, c0:c0 + cb])
        gate_s[:, c0:c0 + cb] = (b_gate * conv).astype(BF16)
    o_ref[0] = x + _dot(gate_s[...], wout_ref[...])


def _halo_specs(tm, s, d):
    nb = tm // HALO
    last = s // HALO - 1
    return [
        pl.BlockSpec((1, tm, d), lambda b, i: (b, i, 0)),
        pl.BlockSpec((1, HALO, d), lambda b, i: (b, jnp.maximum(i * nb - 1, 0), 0)),
        pl.BlockSpec((1, HALO, d), lambda b, i: (b, jnp.minimum((i + 1) * nb, last), 0)),
    ]


def _conv_layer(h, g, w_in, conv_w, w_out):
    bsz, s, d = h.shape
    tm = min(ROW_TILE, s)
    cb = 512
    kern = functools.partial(_conv_kernel, tm=tm, d=d, cb=cb)
    return pl.pallas_call(
        kern,
        grid=(bsz, s // tm),
        in_specs=_halo_specs(tm, s, d) + [
            _full((1, d)), _full((d, 3 * d)), _full((3, d)), _full((d, d))],
        out_specs=pl.BlockSpec((1, tm, d), lambda b, i: (b, i, 0)),
        out_shape=jax.ShapeDtypeStruct((bsz, s, d), F32),
        scratch_shapes=[pltpu.VMEM((tm + 2 * HALO, d), BF16),
                        pltpu.VMEM((tm + 2 * HALO, cb), F32),
                        pltpu.VMEM((tm, d), BF16)],
        compiler_params=_params("parallel", "parallel"),
        name="conv_mixer",
    )(h, h, h, g.reshape(1, d), w_in.astype(BF16), conv_w, w_out.astype(BF16))


def _kv_kernel(m_ref, g_ref, w_ref, o_ref):
    xn = _rms(m_ref[0], g_ref[...]).astype(BF16)
    o_ref[0] = _dot(xn, w_ref[...]).astype(BF16)


def _kv_proj(mem, g, w_kv):
    bsz, m, d = mem.shape
    return pl.pallas_call(
        _kv_kernel,
        grid=(bsz,),
        in_specs=[pl.BlockSpec((1, m, d), lambda b: (b, 0, 0)), _full((1, d)), _full((d, 2 * d))],
        out_specs=pl.BlockSpec((1, m, 2 * d), lambda b: (b, 0, 0)),
        out_shape=jax.ShapeDtypeStruct((bsz, m, 2 * d), BF16),
        compiler_params=_params("parallel"),
        name="kv_proj",
    )(mem, g.reshape(1, d), w_kv.astype(BF16))


def _xattn_kernel(h_ref, gx_ref, wq_ref, kv_ref, wo_ref, gf_ref, rhi_ref, rlo_ref,
                  h_out, xn_out, probs_out, o_s, *, d, heads):
    x = h_ref[0]
    xn = _rms(x, gx_ref[...]).astype(BF16)
    hd = d // heads
    q = (_dot(xn, wq_ref[...]) * (hd ** -0.5)).astype(BF16)
    kv = kv_ref[0]
    for a in range(heads):
        sc = _dot_nt(q[:, a * hd:(a + 1) * hd], kv[:, a * hd:(a + 1) * hd])
        p = jnp.exp(sc - jnp.max(sc, axis=-1, keepdims=True))
        l = jnp.sum(p, axis=-1, keepdims=True)
        o = _dot(p.astype(BF16), kv[:, d + a * hd:d + (a + 1) * hd]) / l
        o_s[:, a * hd:(a + 1) * hd] = o.astype(BF16)
    hn = x + _dot(o_s[...], wo_ref[...])
    h_out[0] = hn
    xf = _rms(hn, gf_ref[...])
    hi = xf.astype(BF16)
    xn_out[0] = hi
    lo = (xf - hi.astype(F32)).astype(BF16)
    lg = _dot_nt(rhi_ref[...], hi) + _dot_nt(rhi_ref[...], lo) + _dot_nt(rlo_ref[...], hi)
    e = jnp.exp(lg - jnp.max(lg, axis=0, keepdims=True))
    probs_out[0] = e / jnp.sum(e, axis=0, keepdims=True)


def _xattn_layer(h, kv, gx, w_q, w_o, gf, router):
    bsz, s, d = h.shape
    m = kv.shape[1]
    e = router.shape[1]
    tm = min(ROW_TILE, s)
    rt = router.T
    r_hi = rt.astype(BF16)
    r_lo = (rt - r_hi.astype(F32)).astype(BF16)
    kern = functools.partial(_xattn_kernel, d=d, heads=XATTN_HEADS)
    return pl.pallas_call(
        kern,
        grid=(bsz, s // tm),
        in_specs=[pl.BlockSpec((1, tm, d), lambda b, i: (b, i, 0)),
                  _full((1, d)), _full((d, d)),
                  pl.BlockSpec((1, m, 2 * d), lambda b, i: (b, 0, 0)),
                  _full((d, d)), _full((1, d)), _full((e, d)), _full((e, d))],
        out_specs=[pl.BlockSpec((1, tm, d), lambda b, i: (b, i, 0)),
                   pl.BlockSpec((1, tm, d), lambda b, i: (b, i, 0)),
                   pl.BlockSpec((1, e, tm), lambda b, i: (b, 0, i))],
        out_shape=[jax.ShapeDtypeStruct((bsz, s, d), F32),
                   jax.ShapeDtypeStruct((bsz, s, d), BF16),
                   jax.ShapeDtypeStruct((bsz, e, s), F32)],
        scratch_shapes=[pltpu.VMEM((tm, d), BF16)],
        compiler_params=_params("parallel", "parallel"),
        name="xattn_router",
    )(h, gx.reshape(1, d), w_q.astype(BF16), kv, w_o.astype(BF16), gf.reshape(1, d), r_hi, r_lo)


def _select_kernel(p_ref, code_ref, start_ref, *, cap, blk):
    p = p_ref[0]
    e, s = p.shape
    bits = pltpu.bitcast(p, I32)

    def count(mask):
        return jnp.sum(mask.astype(F32), axis=1, keepdims=True)

    def search(k, t):
        cand = jnp.bitwise_or(t, jnp.left_shift(jnp.int32(1), 30 - k))
        return jnp.where(count(bits >= cand) >= cap, cand, t)

    thr = lax.fori_loop(0, 31, search, jnp.zeros((e, 1), I32))
    gt = bits > thr
    eq = bits == thr
    need = cap - count(gt)

    r = lax.broadcasted_iota(I32, (blk, blk), 0)
    c = lax.broadcasted_iota(I32, (blk, blk), 1)
    tri = jnp.where(r < c, 1.0, 0.0).astype(BF16)

    def prefix(mask_f):
        out, carries = [], []
        carry = jnp.zeros((e, 1), F32)
        for j in range(0, s, blk):
            mb = mask_f[:, j:j + blk]
            carries.append(carry)
            out.append(_dot(mb.astype(BF16), tri) + carry)
            carry = carry + jnp.sum(mb, axis=1, keepdims=True)
        return out, carries + [carry]

    eq_f = eq.astype(F32)
    tie_rank, _ = prefix(eq_f)
    for j0, tr in zip(range(0, s, blk), tie_rank):
        sel_b = jnp.logical_or(gt[:, j0:j0 + blk],
                               jnp.logical_and(eq[:, j0:j0 + blk], tr < need))
        code_ref[0, :, j0:j0 + blk] = sel_b.astype(I32)
    sel_f = code_ref[0].astype(F32)
    rank, starts = prefix(sel_f)
    for j0, rk in zip(range(0, s, blk), rank):
        code_ref[0, :, j0:j0 + blk] = jnp.where(sel_f[:, j0:j0 + blk] > 0.0, rk.astype(I32), -1)
    start_ref[...] = jnp.zeros_like(start_ref)
    for j, st in enumerate(starts):
        start_ref[0, :, j:j + 1] = st.astype(I32)


def _select(probs_t, cap):
    bsz, e, s = probs_t.shape
    blk = min(TOKEN_BLOCK, s)
    nb1 = s // blk + 1
    assert nb1 <= LANES
    kern = functools.partial(_select_kernel, cap=cap, blk=blk)
    code, starts = pl.pallas_call(
        kern,
        grid=(bsz,),
        in_specs=[pl.BlockSpec((1, e, s), lambda b: (b, 0, 0))],
        out_specs=[pl.BlockSpec((1, e, s), lambda b: (b, 0, 0)),
                   pl.BlockSpec((1, e, LANES), lambda b: (b, 0, 0))],
        out_shape=[jax.ShapeDtypeStruct((bsz, e, s), I32),
                   jax.ShapeDtypeStruct((bsz, e, LANES), I32)],
        compiler_params=_params("parallel"),
        name="expert_choice_select",
    )(probs_t)
    return code, starts[:, :, :nb1].reshape(-1)


def _window(s_lo, s_hi, win):
    s0 = jnp.bitwise_and(s_lo, -BF16_SUBLANES)
    n_win = jnp.where(s_hi > s_lo, lax.shift_right_logical(s_hi - s0 + (win - 1),
                                                           win.bit_length() - 1), 0)
    return s0, n_win


def _ffn_kernel(st_ref, x_ref, code_ref, p_ref, wg_ref, wu_ref, wd_ref, y_ref, xe_s, aff_s,
                *, cap, blk, win):
    k = pl.program_id(0)
    b = pl.program_id(1)
    s = x_ref.shape[1]
    nb = s // blk

    base = (b * pl.num_programs(0) + k) * (nb + 1)
    iota = lax.broadcasted_iota(I32, (win, blk), 0)

    def gather(j, row0):
        onehot = jnp.where(iota + row0 == code_ref[0, :, j * blk:(j + 1) * blk], 1.0, 0.0)
        part = _dot(onehot.astype(BF16), x_ref[0, j * blk:(j + 1) * blk, :])
        xe_s[pl.ds(row0, win), :] += part
        aff_s[pl.ds(row0, win), :] += jnp.sum(onehot * p_ref[0, :, j * blk:(j + 1) * blk],
                                              axis=1, keepdims=True)

    xe_s[...] = jnp.zeros_like(xe_s)
    aff_s[...] = jnp.zeros_like(aff_s)
    wins = [_window(st_ref[base + j], st_ref[base + j + 1], win) for j in range(nb)]
    for j, (s0, _) in enumerate(wins):
        gather(j, pl.multiple_of(s0, BF16_SUBLANES))
    for j, (s0, n_win) in enumerate(wins):
        def more(i, carry, j=j, s0=s0):
            gather(j, pl.multiple_of(s0 + i * win, BF16_SUBLANES))
            return carry
        lax.fori_loop(1, n_win, more, 0)

    xe = xe_s[0:cap, :].astype(BF16)
    gt = _dot(xe, wg_ref[0, 0].astype(BF16))
    up = _dot(xe, wu_ref[0, 0].astype(BF16))
    hid = (gt * jax.nn.sigmoid(gt) * up).astype(BF16)
    y_ref[0, 0] = (_dot(hid, wd_ref[0, 0].astype(BF16)) * aff_s[0:cap, :]).astype(BF16)


def _expert_ffn(xn, code, probs_t, starts, w_gate, w_up, w_down, layer, cap):
    bsz, s, d = xn.shape
    _, e, _, f = w_gate.shape
    blk = min(TOKEN_BLOCK, s)
    win = min(PICK_WINDOW, cap)
    kern = functools.partial(_ffn_kernel, cap=cap, blk=blk, win=win)
    return pl.pallas_call(
        kern,
        grid_spec=pltpu.PrefetchScalarGridSpec(
            num_scalar_prefetch=1,
            grid=(e, bsz),
            in_specs=[pl.BlockSpec((1, s, d), lambda k, b, st: (b, 0, 0)),
                      pl.BlockSpec((1, 1, s), lambda k, b, st: (b * e + k, 0, 0)),
                      pl.BlockSpec((1, 1, s), lambda k, b, st: (b * e + k, 0, 0)),
                      pl.BlockSpec((1, 1, d, f), lambda k, b, st: (layer, k, 0, 0)),
                      pl.BlockSpec((1, 1, d, f), lambda k, b, st: (layer, k, 0, 0)),
                      pl.BlockSpec((1, 1, f, d), lambda k, b, st: (layer, k, 0, 0))],
            out_specs=pl.BlockSpec((1, 1, cap, d), lambda k, b, st: (b, k, 0, 0)),
            scratch_shapes=[pltpu.VMEM((cap + win, d), F32), pltpu.VMEM((cap + win, 1), F32)]),
        out_shape=jax.ShapeDtypeStruct((bsz, e, cap, d), BF16),
        compiler_params=_params("parallel", "parallel"),
        name="expert_ffn",
    )(starts, xn, code.reshape(bsz * e, 1, s), probs_t.reshape(bsz * e, 1, s),
      w_gate, w_up, w_down)


def _combine_kernel(st_ref, h_ref, code_ref, y_ref, g_ref, o_ref, acc_s, *, cap, win, final_norm):
    b = pl.program_id(0)
    j = pl.program_id(1)
    nb = pl.num_programs(1)
    tb = h_ref.shape[1]
    n_exp = y_ref.shape[1]
    iota = lax.broadcasted_iota(I32, (tb, win), 1)
    code = code_ref[0]

    def window(k, lo):
        row0 = pl.multiple_of(jnp.minimum(lo, cap - win), BF16_SUBLANES)
        code_k = jnp.where(code[:, k:k + 1] >= lo, code[:, k:k + 1], -1)
        onehot = jnp.where(iota + row0 == code_k, 1.0, 0.0).astype(BF16)
        return onehot, y_ref[0, k, pl.ds(row0, win), :]

    wins = []
    for k in range(n_exp):
        base = (b * n_exp + k) * (nb + 1) + j
        wins.append(_window(st_ref[base], st_ref[base + 1], win))
    acc = h_ref[0]
    for k in range(0, n_exp, 2):
        oh_a, y_a = window(k, wins[k][0])
        oh_b, y_b = window(k + 1, wins[k + 1][0])
        acc = acc + _dot(jnp.concatenate([oh_a, oh_b], axis=1),
                         jnp.concatenate([y_a, y_b], axis=0))
    acc_s[...] = acc
    for k, (s0, n_win) in enumerate(wins):
        def more(i, carry, k=k, s0=s0):
            acc_s[...] += _dot(*window(k, s0 + i * win))
            return carry
        lax.fori_loop(1, n_win, more, 0)
    acc = acc_s[...]
    o_ref[0] = _rms(acc, g_ref[...]) if final_norm else acc


def _moe_combine(h, code_t, starts, ye, g_final, final_norm):
    bsz, s, d = h.shape
    e, cap = ye.shape[1], ye.shape[2]
    assert e % 2 == 0
    tb = min(TOKEN_BLOCK, s)
    win = min(PICK_WINDOW, cap)
    kern = functools.partial(_combine_kernel, cap=cap, win=win, final_norm=final_norm)
    return pl.pallas_call(
        kern,
        grid_spec=pltpu.PrefetchScalarGridSpec(
            num_scalar_prefetch=1,
            grid=(bsz, s // tb),
            in_specs=[pl.BlockSpec((1, tb, d), lambda b, j, st: (b, j, 0)),
                      pl.BlockSpec((1, tb, e), lambda b, j, st: (b, j, 0)),
                      pl.BlockSpec((1, e, cap, d), lambda b, j, st: (b, 0, 0, 0)),
                      pl.BlockSpec((1, d), lambda b, j, st: (0, 0))],
            out_specs=pl.BlockSpec((1, tb, d), lambda b, j, st: (b, j, 0)),
            scratch_shapes=[pltpu.VMEM((tb, d), F32)]),
        out_shape=jax.ShapeDtypeStruct((bsz, s, d), F32),
        compiler_params=_params("parallel", "parallel"),
        name="moe_combine",
    )(starts, h, code_t, ye, g_final.reshape(1, d))


def _rwkv_prep_kernel(h_ref, hp_ref, hn_ref, g_ref, mu_ref, wrkv_ref, w1_ref, a1_ref, g1_ref,
                      w2_ref, a2_ref, g2_ref, w0_ref, a0_ref, kk_ref, ka_ref, rk_ref, hs_ref,
                      r_out, v_out, kn_out, gate_out, bonus_out, lw_out, kd_out, bd_out,
                      xn_s, *, tm):
    i = pl.program_id(1)
    g = g_ref[...]
    keep_prev = jnp.where(i == 0, 0.0, 1.0)
    keep_next = jnp.where(i == pl.num_programs(1) - 1, 0.0, 1.0)
    xn_s[0:HALO, :] = _rms(hp_ref[0], g) * keep_prev
    xn_s[HALO:HALO + tm, :] = _rms(h_ref[0], g)
    xn_s[HALO + tm:, :] = _rms(hn_ref[0], g) * keep_next
    xn = xn_s[pl.ds(HALO, tm), :]
    xx = 0.5 * (xn_s[pl.ds(HALO - 1, tm), :] + xn_s[pl.ds(HALO + 1, tm), :]) - xn

    def mix(j):
        return (xn + xx * mu_ref[j:j + 1, :]).astype(BF16)

    r = _dot(mix(0), wrkv_ref[0])
    k = _dot(mix(1), wrkv_ref[1])
    v = _dot(mix(2), wrkv_ref[2])
    hw = jnp.tanh(_dot(mix(3), w1_ref[...])).astype(BF16)
    ha = _dot(mix(4), a1_ref[...]).astype(BF16)
    hg = jax.nn.sigmoid(_dot(mix(5), g1_ref[...])).astype(BF16)
    gate_out[0] = _dot(hg, g2_ref[...]).astype(BF16)

    hsum = hs_ref[...]
    kk = k * kk_ref[...]
    nrm2 = _dot((kk * kk).astype(BF16), hsum)
    kn = kk / jnp.maximum(jnp.sqrt(nrm2), 1e-12)
    r_out[0] = r.astype(BF16)
    v_out[0] = v.astype(BF16)
    kn_out[0] = kn.astype(BF16)
    ksum = None
    for n in range(2):
        w_raw = w0_ref[n:n + 1, :] + _dot(hw, w2_ref[n])
        lw_out[n, 0] = -DECAY_SCALE * jax.nn.sigmoid(w_raw)
        a = jax.nn.sigmoid(a0_ref[n:n + 1, :] + _dot(ha, a2_ref[n]))
        kd = k * (1.0 + (a - 1.0) * ka_ref[...])
        kd_out[n, 0] = kd.astype(BF16)
        bd_out[n, 0] = (kn * a).astype(BF16)
        ksum = kd if ksum is None else ksum + kd
    coef = _dot((r * ksum * rk_ref[...]).astype(BF16), hsum)
    bonus_out[0] = (coef * v).astype(BF16)


def _head_sum_matrix(d, n):
    idx = jnp.arange(d) // n
    return (idx[:, None] == idx[None, :]).astype(BF16)


def _rwkv_prep(h, g, mu, w_rkv, w0, w1, w2, a0, a1, a2, g1, g2, k_k, k_a, r_k):
    bsz, s, d = h.shape
    tm = min(RWKV_ROW_TILE, s)
    lora = w1.shape[-1]
    glora = g1.shape[-1]
    w1c = jnp.concatenate([w1[0], w1[1]], axis=1).astype(BF16)
    a1c = jnp.concatenate([a1[0], a1[1]], axis=1).astype(BF16)
    keep = (jnp.arange(2 * lora)[None, :, None] // lora) == jnp.arange(2)[:, None, None]
    w2p = jnp.where(keep, jnp.concatenate([w2, w2], axis=1), 0.0).astype(BF16)
    a2p = jnp.where(keep, jnp.concatenate([a2, a2], axis=1), 0.0).astype(BF16)
    kern = functools.partial(_rwkv_prep_kernel, tm=tm)
    tok = pl.BlockSpec((1, tm, d), lambda b, i: (b, i, 0))
    tok2 = pl.BlockSpec((2, 1, tm, d), lambda b, i: (0, b, i, 0))
    sd = jax.ShapeDtypeStruct
    return pl.pallas_call(
        kern,
        grid=(bsz, s // tm),
        in_specs=_halo_specs(tm, s, d) + [
            _full((1, d)), _full((6, d)), _full((3, d, d)),
            _full((d, 2 * lora)), _full((d, 2 * lora)), _full((d, glora)),
            _full((2, 2 * lora, d)), _full((2, 2 * lora, d)), _full((glora, d)),
            _full((2, d)), _full((2, d)), _full((1, d)), _full((1, d)), _full((1, d)),
            _full((d, d))],
        out_specs=[tok, tok, tok, tok, tok, tok2, tok2, tok2],
        out_shape=[sd((bsz, s, d), BF16)] * 5 + [sd((2, bsz, s, d), F32),
                                                 sd((2, bsz, s, d), BF16),
                                                 sd((2, bsz, s, d), BF16)],
        scratch_shapes=[pltpu.VMEM((tm + 2 * HALO, d), F32)],
        compiler_params=_params("parallel", "parallel"),
        name="rwkv_prep",
    )(h, h, h, g.reshape(1, d), mu, w_rkv.astype(BF16), w1c, a1c, g1.astype(BF16),
      w2p, a2p, g2.astype(BF16), w0, a0,
      k_k.reshape(1, d), k_a.reshape(1, d), r_k.reshape(1, d),
      _head_sum_matrix(d, RWKV_HEAD_DIM))


def _wkv_kernel(rf_ref, vf_ref, knf_ref, rr_ref, vr_ref, knr_ref,
                lwf_ref, kdf_ref, bdf_ref, lwr_ref, kdr_ref, bdr_ref,
                yf_ref, yr_ref, qf_s, qr_s, rh_s, y0_s, nm_s, q0_s, gt_s, *, c, ng):
    hd = RWKV_HEAD_DIM
    w = 2 * hd
    j = pl.program_id(2)
    n_steps = pl.num_programs(2)
    carried = (rh_s, y0_s, nm_s, q0_s, gt_s)

    @pl.when(j == 0)
    def _():
        for ref in (qf_s, qr_s) + carried:
            ref[...] = jnp.zeros_like(ref)

    ti = lax.broadcasted_iota(I32, (c, c), 0)
    si = lax.broadcasted_iota(I32, (c, c), 1)
    lane = lax.broadcasted_iota(I32, (1, w), 1)
    m_lo = lane < hd
    ti2 = lax.broadcasted_iota(I32, (c, 2 * c), 0)
    si2 = lax.broadcasted_iota(I32, (c, 2 * c), 1)
    si2 = jnp.where(si2 >= c, si2 - c, si2)
    eye2 = jnp.where(si2 == ti2, 1.0, 0.0)
    dir_masks = (
        (jnp.where(si <= ti, 1.0, 0.0).astype(BF16), si2 <= ti2, si2 < ti2),
        (jnp.where(si >= ti, 1.0, 0.0).astype(BF16), si2 >= ti2, si2 > ti2),
    )
    rr = lax.broadcasted_iota(I32, (2 * c, 2 * c), 0)
    cc = lax.broadcasted_iota(I32, (2 * c, 2 * c), 1)
    bd_mask_c = jnp.where(rr < c, 0, 1) == jnp.where(cc < c, 0, 1)
    rr = lax.broadcasted_iota(I32, (w, w), 0)
    cc = lax.broadcasted_iota(I32, (w, w), 1)
    bd_mask_h = jnp.where(rr < hd, 0, 1) == jnp.where(cc < hd, 0, 1)

    def row_stack(x):
        return jnp.concatenate([jnp.where(m_lo, x, 0.0), jnp.where(m_lo, 0.0, x)],
                               axis=0).astype(BF16)

    def block_diag(xp):
        return jnp.where(bd_mask_c, jnp.concatenate([xp, xp], axis=0), 0.0).astype(BF16)

    fwd = (0, rf_ref, vf_ref, knf_ref, lwf_ref, kdf_ref, bdf_ref)
    rev = (1, rr_ref, vr_ref, knr_ref, lwr_ref, kdr_ref, bdr_ref)
    y_refs = (yf_ref, yr_ref)
    q = [qf_s[...], qr_s[...]]

    def stages(probs, st):
        for dirn, r_ref, v_ref, kn_ref, lw_ref, kd_ref, bd_ref, ci in probs:
            tri_incl, _, _ = dir_masks[dirn]
            sl = pl.ds(ci * c, c)
            lw = lw_ref[0, 0, sl, :]
            lw_hi = lw.astype(BF16)
            lw_lo = (lw - lw_hi.astype(F32)).astype(BF16)
            l_incl = _dot(tri_incl, lw_hi) + _dot(tri_incl, lw_lo)
            st.append(dict(dirn=dirn, sl=sl, lw=lw, l_incl=l_incl,
                           r=r_ref[0, sl, :].astype(F32), v=v_ref[0, sl, :].astype(F32),
                           kn=kn_ref[0, sl, :].astype(F32), kd=kd_ref[0, 0, sl, :].astype(F32),
                           bd=bd_ref[0, 0, sl, :].astype(F32)))
        yield
        for p in st:
            stage_decay(p)
        yield
        for p in st:
            stage_masks(p)
        yield
        for _ in range(max(c.bit_length() - 3, 0)):
            for p in st:
                res = _dot(jnp.concatenate([p["pw"], p["t_p"]], axis=0).astype(BF16),
                           block_diag(p["pw"]))
                p["pw"] = res[0:c]
                p["t_p"] = p["t_p"] + res[c:2 * c]
            yield
        for p in st:
            stage_solve(p)
        yield
        for p in st:
            stage_maps(p)
        yield

    def handover(off_f, off_r):
        for i in range(ng):
            for dirn, slot, row in ((0, i, off_f + i * c), (1, ng + i, off_r + (ng - 1 - i) * c)):
                q_b = q[dirn].astype(BF16)
                y_refs[dirn][0, pl.ds(pl.multiple_of(row, c), c), :] = (
                    _dot_nt(rh_s[slot], q_b) + y0_s[slot])
                q[dirn] = q[dirn] * gt_s[slot] + _dot(q_b, nm_s[slot]) + q0_s[slot]
            yield

    def stage_decay(p):
        l_incl = p["l_incl"]
        l_tot = l_incl[0:1, :] if p["dirn"] else l_incl[c - 1:c, :]
        g_inv = jnp.exp(-l_incl)
        g_end = jnp.exp(l_tot - l_incl)
        p["g_tot"] = jnp.exp(l_tot)
        p["rt"] = p["r"] * jnp.exp(l_incl)
        p["at"] = -p["kn"] * jnp.exp(l_incl - p["lw"])
        p["v_rs"] = row_stack(p["v"])
        p["bh"] = (p["bd"] * g_end).astype(BF16)
        p["kh"] = (p["kd"] * g_end).astype(BF16)
        lhs = jnp.concatenate([p["at"], p["rt"]], axis=0).astype(BF16)
        rhs = jnp.concatenate([row_stack(p["bd"] * g_inv), row_stack(p["kd"] * g_inv)],
                              axis=0)
        p["gm"] = _dot_nt(lhs, rhs)

    def stage_masks(p):
        _, incl2, strict2 = dir_masks[p["dirn"]]
        gm = p.pop("gm")
        a_ab = jnp.where(strict2, gm[0:c, 0:2 * c], 0.0)
        a_ak = jnp.where(strict2, gm[0:c, 2 * c:4 * c], 0.0)
        a_rb = jnp.where(incl2, gm[c:2 * c, 0:2 * c], 0.0)
        a_rk = jnp.where(incl2, gm[c:2 * c, 2 * c:4 * c], 0.0)
        p["a_rb"] = a_rb.astype(BF16)
        av = _dot(jnp.concatenate([a_ak, a_rk], axis=0).astype(BF16), p["v_rs"])
        p["akv"] = av[0:c]
        p["arkv"] = av[c:2 * c]
        p["t_p"] = eye2 + a_ab
        p["pw"] = _dot(a_ab.astype(BF16), block_diag(a_ab))

    def stage_solve(p):
        t_p = p["t_p"] + _dot(p["t_p"].astype(BF16), block_diag(p["pw"]))
        wu = _dot(t_p.astype(BF16),
                  jnp.concatenate([row_stack(p["at"]), row_stack(p["akv"])], axis=1))
        p["w_m"] = wu[:, 0:w]
        p["u_t"] = wu[:, w:2 * w]

    def stage_maps(p):
        ry = _dot(p["a_rb"],
                  jnp.concatenate([row_stack(p["w_m"]), row_stack(p["u_t"])], axis=1))
        p["r_hat"] = (p["rt"] + ry[:, 0:w]).astype(BF16)
        p["y0"] = ry[:, w:2 * w] + p["arkv"]
        p["n_m"] = jnp.where(bd_mask_h, _dot_tn(p["w_m"].astype(BF16), p["bh"]),
                             0.0).astype(BF16)
        p["q0"] = jnp.where(
            bd_mask_h,
            _dot_tn(jnp.concatenate([p["u_t"], p["v"]], axis=0).astype(BF16),
                    jnp.concatenate([p["bh"], p["kh"]], axis=0)), 0.0)

    cg = ng * c
    probs = [fwd + (ci,) for ci in range(ng)] + [rev + (ci,) for ci in range(ng - 1, -1, -1)]
    st = []
    pending = handover(jnp.maximum(j - 1, 0) * cg, jnp.minimum(n_steps - j, n_steps - 1) * cg)
    for _ in stages(probs, st):
        next(pending, None)
    for _ in pending:
        pass
    for slot, p in enumerate(st):
        rh_s[slot] = p["r_hat"]
        y0_s[slot] = p["y0"]
        nm_s[slot] = p["n_m"]
        q0_s[slot] = p["q0"]
        gt_s[slot] = p["g_tot"]
    qf_s[...] = q[0]
    qr_s[...] = q[1]

    @pl.when(j == n_steps - 1)
    def _():
        q[0] = qf_s[...]
        q[1] = qr_s[...]
        for _ in handover(j * cg, 0):
            pass


def _wkv(r, v, kn, lw, kd, bd):
    bsz, s, d = r.shape
    c = min(CHUNK, s)
    ng = min(CHUNK_GROUP, s // c)
    cg = c * ng
    n_steps = s // cg
    w = 2 * RWKV_HEAD_DIM
    tok_f = pl.BlockSpec((1, cg, w), lambda b, p, j: (b, j, p))
    tok_r = pl.BlockSpec((1, cg, w), lambda b, p, j: (b, n_steps - 1 - j, p))
    dir_f = pl.BlockSpec((1, 1, cg, w), lambda b, p, j: (0, b, j, p))
    dir_r = pl.BlockSpec((1, 1, cg, w), lambda b, p, j: (1, b, n_steps - 1 - j, p))
    kern = functools.partial(_wkv_kernel, c=c, ng=ng)
    seq = pl.BlockSpec((1, s, w), lambda b, p, j: (b, 0, p))
    return pl.pallas_call(
        kern,
        grid=(bsz, d // w, n_steps),
        in_specs=[tok_f, tok_f, tok_f, tok_r, tok_r, tok_r,
                  dir_f, dir_f, dir_f, dir_r, dir_r, dir_r],
        out_specs=[seq, seq],
        out_shape=[jax.ShapeDtypeStruct((bsz, s, d), F32)] * 2,
        scratch_shapes=[pltpu.VMEM((w, w), F32), pltpu.VMEM((w, w), F32),
                        pltpu.VMEM((2 * ng, c, w), BF16), pltpu.VMEM((2 * ng, c, w), F32),
                        pltpu.VMEM((2 * ng, w, w), BF16), pltpu.VMEM((2 * ng, w, w), F32),
                        pltpu.VMEM((2 * ng, 1, w), F32)],
        compiler_params=_params("parallel", "parallel", "arbitrary"),
        name="wkv",
    )(r, v, kn, r, v, kn, lw, kd, bd, lw, kd, bd)


def _rwkv_post_kernel(h_ref, yf_ref, yr_ref, bonus_ref, gate_ref, lnw_ref, lnb_ref, hs_ref,
                      wout_ref, o_ref, *, n):
    hsum = hs_ref[...]
    y = yf_ref[0] + yr_ref[0]
    y_hi = y.astype(BF16)
    y_lo = (y - y_hi.astype(F32)).astype(BF16)
    mean = (_dot(y_hi, hsum) + _dot(y_lo, hsum)) * (1.0 / n)
    yc = y - mean
    var = _dot((yc * yc).astype(BF16), hsum) * (1.0 / n)
    yn = yc * lax.rsqrt(var + GN_EPS) * lnw_ref[...] + lnb_ref[...]
    yn = yn + bonus_ref[0].astype(F32)
    z = (yn * gate_ref[0].astype(F32)).astype(BF16)
    o_ref[0] = h_ref[0] + _dot(z, wout_ref[...])


def _rwkv_post(h, yf, yr, bonus, gate, ln_w, ln_b, w_out):
    bsz, s, d = h.shape
    tm = min(ROW_TILE, s)
    tok = pl.BlockSpec((1, tm, d), lambda b, i: (b, i, 0))
    kern = functools.partial(_rwkv_post_kernel, n=RWKV_HEAD_DIM)
    return pl.pallas_call(
        kern,
        grid=(bsz, s // tm),
        in_specs=[tok, tok, tok, tok, tok, _full((1, d)), _full((1, d)), _full((d, d)),
                  _full((d, d))],
        out_specs=tok,
        out_shape=jax.ShapeDtypeStruct((bsz, s, d), F32),
        compiler_params=_params("parallel", "parallel"),
        name="rwkv_post",
    )(h, yf, yr, bonus, gate, ln_w.reshape(1, d), ln_b.reshape(1, d),
      _head_sum_matrix(d, RWKV_HEAD_DIM), w_out.astype(BF16))


def kernel(x, mem, norm_mix, norm_xattn, norm_mem, norm_ffn, norm_final,
           conv_w_in, conv_w, conv_w_out,
           rwkv_mu, rwkv_w_rkv, rwkv_w0, rwkv_w1, rwkv_w2, rwkv_a0, rwkv_a1, rwkv_a2,
           rwkv_g1, rwkv_g2, rwkv_k_k, rwkv_k_a, rwkv_r_k, rwkv_ln_w, rwkv_ln_b, rwkv_w_out,
           xattn_w_q, xattn_w_kv, xattn_w_o,
           moe_router, moe_w_gate, moe_w_up, moe_w_down):
    depth = norm_mix.shape[0]
    bsz, s, d = x.shape
    n_exp = moe_router.shape[-1]
    cap = CAPACITY_FACTOR * s // n_exp
    h = x
    for i in range(depth):
        j = i // N_MIXERS
        if i % N_MIXERS == 0:
            h = _conv_layer(h, norm_mix[i], conv_w_in[j], conv_w[j], conv_w_out[j])
        else:
            r, v, kn, gate, bonus, lw, kd, bd = _rwkv_prep(
                h, norm_mix[i], rwkv_mu[j], rwkv_w_rkv[j], rwkv_w0[j], rwkv_w1[j], rwkv_w2[j],
                rwkv_a0[j], rwkv_a1[j], rwkv_a2[j], rwkv_g1[j], rwkv_g2[j],
                rwkv_k_k[j], rwkv_k_a[j], rwkv_r_k[j])
            yf, yr = _wkv(r, v, kn, lw, kd, bd)
            h = _rwkv_post(h, yf, yr, bonus, gate, rwkv_ln_w[j], rwkv_ln_b[j], rwkv_w_out[j])
        kv = _kv_proj(mem, norm_mem[i], xattn_w_kv[i])
        h, xn, probs_t = _xattn_layer(h, kv, norm_xattn[i], xattn_w_q[i], xattn_w_o[i],
                                      norm_ffn[i], moe_router[i])
        code, starts = _select(probs_t, cap)
        ye = _expert_ffn(xn, code, probs_t, starts, moe_w_gate, moe_w_up, moe_w_down, i, cap)
        h = _moe_combine(h, jnp.swapaxes(code, 1, 2), starts, ye, norm_final,
                         final_norm=(i == depth - 1))
    return h
```

```python
import functools

import jax
import jax.numpy as jnp
from jax import lax
from jax.experimental import pallas as pl
from jax.experimental.pallas import tpu as pltpu

F32 = jnp.float32
BF16 = jnp.bfloat16
I32 = jnp.int32

N_MIXERS = 2
RWKV_HEAD_DIM = 64
XATTN_HEADS = 4
CAPACITY_FACTOR = 2
GN_EPS = 64e-5
RMS_EPS = 1e-6
DECAY_SCALE = 0.6065306597126334

V7X_VMEM_LIMIT_BYTES = 56 * 1024 * 1024
LANES = 128
BF16_SUBLANES = 16

ROW_TILE = 512
RWKV_ROW_TILE = 512
HALO = BF16_SUBLANES
CHUNK = 64
CHUNK_GROUP = 8
TOKEN_BLOCK = 512
PICK_WINDOW = 128
GATHER_BLOCK = 256
GATHER_WINDOW = 64


def _params(*semantics):
    return pltpu.CompilerParams(dimension_semantics=semantics,
                                vmem_limit_bytes=V7X_VMEM_LIMIT_BYTES)


def _rms(x, g):
    return x * lax.rsqrt(jnp.mean(x * x, axis=-1, keepdims=True) + RMS_EPS) * g


def _dot(a, b):
    return jnp.dot(a, b, preferred_element_type=F32)


def _dot_nt(a, b):
    return lax.dot_general(a, b, (((1,), (1,)), ((), ())), preferred_element_type=F32)


def _dot_tn(a, b):
    return lax.dot_general(a, b, (((0,), (0,)), ((), ())), preferred_element_type=F32)


def _full(shape):
    n = len(shape)
    return pl.BlockSpec(shape, lambda *_: (0,) * n)


def _conv_kernel(h_ref, hp_ref, hn_ref, g_ref, win_ref, cw_ref, wout_ref, o_ref,
                 xn_s, u_s, gate_s, *, tm, d, cb):
    i = pl.program_id(1)
    g = g_ref[...]
    x = h_ref[0]
    xn_s[0:HALO, :] = _rms(hp_ref[0], g).astype(BF16)
    xn_s[HALO:HALO + tm, :] = _rms(x, g).astype(BF16)
    xn_s[HALO + tm:, :] = _rms(hn_ref[0], g).astype(BF16)
    rows = tm + 2 * HALO
    row = lax.broadcasted_iota(I32, (rows, 1), 0)
    lo = jnp.where(i == 0, HALO, 0)
    hi = jnp.where(i == pl.num_programs(1) - 1, HALO + tm, rows)
    pad = jnp.logical_or(row < lo, row >= hi)
    xa = xn_s[...]
    for c0 in range(0, d, cb):
        c_gate = _dot(xa, win_ref[:, d + c0:d + c0 + cb])
        hx = _dot(xa, win_ref[:, 2 * d + c0:2 * d + c0 + cb])
        u_s[...] = jnp.where(pad, 0.0, c_gate * hx)
        conv = (u_s[pl.ds(HALO - 1, tm), :] * cw_ref[0:1, c0:c0 + cb]
                + u_s[pl.ds(HALO, tm), :] * cw_ref[1:2, c0:c0 + cb]
                + u_s[pl.ds(HALO + 1, tm), :] * cw_ref[2:3, c0:c0 + cb])
        b_gate = _dot(xn_s[HALO:HALO + tm, :], win_ref[:, c0:c0 + cb])
        gate_s[:, c0:c0 + cb] = (b_gate * conv).astype(BF16)
    o_ref[0] = x + _dot(gate_s[...], wout_ref[...])


def _halo_specs(tm, s, d):
    nb = tm // HALO
    last = s // HALO - 1
    return [
        pl.BlockSpec((1, tm, d), lambda b, i: (b, i, 0)),
        pl.BlockSpec((1, HALO, d), lambda b, i: (b, jnp.maximum(i * nb - 1, 0), 0)),
        pl.BlockSpec((1, HALO, d), lambda b, i: (b, jnp.minimum((i + 1) * nb, last), 0)),
    ]


def _conv_layer(h, g, w_in, conv_w, w_out):
    bsz, s, d = h.shape
    tm = min(ROW_TILE, s)
    cb = 512
    kern = functools.partial(_conv_kernel, tm=tm, d=d, cb=cb)
    return pl.pallas_call(
        kern,
        grid=(bsz, s // tm),
        in_specs=_halo_specs(tm, s, d) + [
            _full((1, d)), _full((d, 3 * d)), _full((3, d)), _full((d, d))],
        out_specs=pl.BlockSpec((1, tm, d), lambda b, i: (b, i, 0)),
        out_shape=jax.ShapeDtypeStruct((bsz, s, d), F32),
        scratch_shapes=[pltpu.VMEM((tm + 2 * HALO, d), BF16),
                        pltpu.VMEM((tm + 2 * HALO, cb), F32),
                        pltpu.VMEM((tm, d), BF16)],
        compiler_params=_params("parallel", "parallel"),
        name="conv_mixer",
    )(h, h, h, g.reshape(1, d), w_in.astype(BF16), conv_w, w_out.astype(BF16))


def _kv_kernel(m_ref, g_ref, w_ref, o_ref):
    xn = _rms(m_ref[0], g_ref[...]).astype(BF16)
    o_ref[0] = _dot(xn, w_ref[...]).astype(BF16)


def _kv_proj(mem, g, w_kv):
    bsz, m, d = mem.shape
    return pl.pallas_call(
        _kv_kernel,
        grid=(bsz,),
        in_specs=[pl.BlockSpec((1, m, d), lambda b: (b, 0, 0)), _full((1, d)), _full((d, 2 * d))],
        out_specs=pl.BlockSpec((1, m, 2 * d), lambda b: (b, 0, 0)),
        out_shape=jax.ShapeDtypeStruct((bsz, m, 2 * d), BF16),
        compiler_params=_params("parallel"),
        name="kv_proj",
    )(mem, g.reshape(1, d), w_kv.astype(BF16))


def _xattn_kernel(h_ref, gx_ref, wq_ref, kv_ref, wo_ref, gf_ref, rhi_ref, rlo_ref,
                  h_out, xn_out, probs_out, o_s, *, d, heads):
    x = h_ref[0]
    xn = _rms(x, gx_ref[...]).astype(BF16)
    hd = d // heads
    q = (_dot(xn, wq_ref[...]) * (hd ** -0.5)).astype(BF16)
    kv = kv_ref[0]
    for a in range(heads):
        sc = _dot_nt(q[:, a * hd:(a + 1) * hd], kv[:, a * hd:(a + 1) * hd])
        p = jnp.exp(sc - jnp.max(sc, axis=-1, keepdims=True))
        l = jnp.sum(p, axis=-1, keepdims=True)
        o = _dot(p.astype(BF16), kv[:, d + a * hd:d + (a + 1) * hd]) / l
        o_s[:, a * hd:(a + 1) * hd] = o.astype(BF16)
    hn = x + _dot(o_s[...], wo_ref[...])
    h_out[0] = hn
    xf = _rms(hn, gf_ref[...])
    hi = xf.astype(BF16)
    xn_out[0] = hi
    lo = (xf - hi.astype(F32)).astype(BF16)
    lg = _dot_nt(rhi_ref[...], hi) + _dot_nt(rhi_ref[...], lo) + _dot_nt(rlo_ref[...], hi)
    e = jnp.exp(lg - jnp.max(lg, axis=0, keepdims=True))
    probs_out[0] = e / jnp.sum(e, axis=0, keepdims=True)


def _xattn_layer(h, kv, gx, w_q, w_o, gf, router):
    bsz, s, d = h.shape
    m = kv.shape[1]
    e = router.shape[1]
    tm = min(ROW_TILE, s)
    rt = router.T
    r_hi = rt.astype(BF16)
    r_lo = (rt - r_hi.astype(F32)).astype(BF16)
    kern = functools.partial(_xattn_kernel, d=d, heads=XATTN_HEADS)
    return pl.pallas_call(
        kern,
        grid=(bsz, s // tm),
        in_specs=[pl.BlockSpec((1, tm, d), lambda b, i: (b, i, 0)),
                  _full((1, d)), _full((d, d)),
                  pl.BlockSpec((1, m, 2 * d), lambda b, i: (b, 0, 0)),
                  _full((d, d)), _full((1, d)), _full((e, d)), _full((e, d))],
        out_specs=[pl.BlockSpec((1, tm, d), lambda b, i: (b, i, 0)),
                   pl.BlockSpec((1, tm, d), lambda b, i: (b, i, 0)),
                   pl.BlockSpec((1, e, tm), lambda b, i: (b, 0, i))],
        out_shape=[jax.ShapeDtypeStruct((bsz, s, d), F32),
                   jax.ShapeDtypeStruct((bsz, s, d), BF16),
                   jax.ShapeDtypeStruct((bsz, e, s), F32)],
        scratch_shapes=[pltpu.VMEM((tm, d), BF16)],
        compiler_params=_params("parallel", "parallel"),
        name="xattn_router",
    )(h, gx.reshape(1, d), w_q.astype(BF16), kv, w_o.astype(BF16), gf.reshape(1, d), r_hi, r_lo)


def _select_kernel(p_ref, code_ref, start_ref, *, cap, blk):
    p = p_ref[0]
    e, s = p.shape
    bits = pltpu.bitcast(p, I32)

    def count(mask):
        return jnp.sum(mask.astype(F32), axis=1, keepdims=True)

    def search(k, t):
        cand = jnp.bitwise_or(t, jnp.left_shift(jnp.int32(1), 30 - k))
        return jnp.where(count(bits >= cand) >= cap, cand, t)

    thr = lax.fori_loop(0, 31, search, jnp.zeros((e, 1), I32))
    gt = bits > thr
    eq = bits == thr
    need = cap - count(gt)

    r = lax.broadcasted_iota(I32, (blk, blk), 0)
    c = lax.broadcasted_iota(I32, (blk, blk), 1)
    tri = jnp.where(r < c, 1.0, 0.0).astype(BF16)

    def prefix(mask_f):
        out, carries = [], []
        carry = jnp.zeros((e, 1), F32)
        for j in range(0, s, blk):
            mb = mask_f[:, j:j + blk]
            carries.append(carry)
            out.append(_dot(mb.astype(BF16), tri) + carry)
            carry = carry + jnp.sum(mb, axis=1, keepdims=True)
        return out, carries + [carry]

    eq_f = eq.astype(F32)
    tie_rank, _ = prefix(eq_f)
    for j0, tr in zip(range(0, s, blk), tie_rank):
        sel_b = jnp.logical_or(gt[:, j0:j0 + blk],
                               jnp.logical_and(eq[:, j0:j0 + blk], tr < need))
        code_ref[0, :, j0:j0 + blk] = sel_b.astype(I32)
    sel_f = code_ref[0].astype(F32)
    rank, starts = prefix(sel_f)
    for j0, rk in zip(range(0, s, blk), rank):
        code_ref[0, :, j0:j0 + blk] = jnp.where(sel_f[:, j0:j0 + blk] > 0.0, rk.astype(I32), -1)
    start_ref[...] = jnp.zeros_like(start_ref)
    for j, st in enumerate(starts):
        start_ref[0, :, j:j + 1] = st.astype(I32)


def _select(probs_t, cap):
    bsz, e, s = probs_t.shape
    blk = min(GATHER_BLOCK, s)
    nb1 = s // blk + 1
    assert nb1 <= LANES
    kern = functools.partial(_select_kernel, cap=cap, blk=blk)
    code, starts = pl.pallas_call(
        kern,
        grid=(bsz,),
        in_specs=[pl.BlockSpec((1, e, s), lambda b: (b, 0, 0))],
        out_specs=[pl.BlockSpec((1, e, s), lambda b: (b, 0, 0)),
                   pl.BlockSpec((1, e, LANES), lambda b: (b, 0, 0))],
        out_shape=[jax.ShapeDtypeStruct((bsz, e, s), I32),
                   jax.ShapeDtypeStruct((bsz, e, LANES), I32)],
        compiler_params=_params("parallel"),
        name="expert_choice_select",
    )(probs_t)
    return code, starts[:, :, :nb1].reshape(-1)


def _window(s_lo, s_hi, win):
    s0 = jnp.bitwise_and(s_lo, -BF16_SUBLANES)
    n_win = jnp.where(s_hi > s_lo, lax.shift_right_logical(s_hi - s0 + (win - 1),
                                                           win.bit_length() - 1), 0)
    return s0, n_win


def _ffn_kernel(st_ref, x_ref, code_ref, p_ref, wg_ref, wu_ref, wd_ref, y_ref, xe_s, aff_s,
                *, cap, blk, win):
    k = pl.program_id(0)
    b = pl.program_id(1)
    s = x_ref.shape[1]
    nb = s // blk

    base = (b * pl.num_programs(0) + k) * (nb + 1)
    iota = lax.broadcasted_iota(I32, (win, blk), 0)

    def gather(j, row0):
        onehot = jnp.where(iota + row0 == code_ref[0, :, j * blk:(j + 1) * blk], 1.0, 0.0)
        part = _dot(onehot.astype(BF16), x_ref[0, j * blk:(j + 1) * blk, :])
        xe_s[pl.ds(row0, win), :] += part
        aff_s[pl.ds(row0, win), :] += jnp.sum(onehot * p_ref[0, :, j * blk:(j + 1) * blk],
                                              axis=1, keepdims=True)

    xe_s[...] = jnp.zeros_like(xe_s)
    aff_s[...] = jnp.zeros_like(aff_s)
    wins = [_window(st_ref[base + j], st_ref[base + j + 1], win) for j in range(nb)]
    for j, (s0, _) in enumerate(wins):
        gather(j, pl.multiple_of(s0, BF16_SUBLANES))
    for j, (s0, n_win) in enumerate(wins):
        def more(i, carry, j=j, s0=s0):
            gather(j, pl.multiple_of(s0 + i * win, BF16_SUBLANES))
            return carry
        lax.fori_loop(1, n_win, more, 0)

    xe = xe_s[0:cap, :].astype(BF16)
    gt = _dot(xe, wg_ref[0, 0].astype(BF16))
    up = _dot(xe, wu_ref[0, 0].astype(BF16))
    hid = (gt * jax.nn.sigmoid(gt) * up).astype(BF16)
    y_ref[0, 0] = (_dot(hid, wd_ref[0, 0].astype(BF16)) * aff_s[0:cap, :]).astype(BF16)


def _expert_ffn(xn, code, probs_t, starts, w_gate, w_up, w_down, layer, cap):
    bsz, s, d = xn.shape
    _, e, _, f = w_gate.shape
    blk = min(GATHER_BLOCK, s)
    win = min(GATHER_WINDOW, cap)
    kern = functools.partial(_ffn_kernel, cap=cap, blk=blk, win=win)
    return pl.pallas_call(
        kern,
        grid_spec=pltpu.PrefetchScalarGridSpec(
            num_scalar_prefetch=1,
            grid=(e, bsz),
            in_specs=[pl.BlockSpec((1, s, d), lambda k, b, st: (b, 0, 0)),
                      pl.BlockSpec((1, 1, s), lambda k, b, st: (b * e + k, 0, 0)),
                      pl.BlockSpec((1, 1, s), lambda k, b, st: (b * e + k, 0, 0)),
                      pl.BlockSpec((1, 1, d, f), lambda k, b, st: (layer, k, 0, 0)),
                      pl.BlockSpec((1, 1, d, f), lambda k, b, st: (layer, k, 0, 0)),
                      pl.BlockSpec((1, 1, f, d), lambda k, b, st: (layer, k, 0, 0))],
            out_specs=pl.BlockSpec((1, 1, cap, d), lambda k, b, st: (b, k, 0, 0)),
            scratch_shapes=[pltpu.VMEM((cap + win, d), F32), pltpu.VMEM((cap + win, 1), F32)]),
        out_shape=jax.ShapeDtypeStruct((bsz, e, cap, d), BF16),
        compiler_params=_params("parallel", "parallel"),
        name="expert_ffn",
    )(starts, xn, code.reshape(bsz * e, 1, s), probs_t.reshape(bsz * e, 1, s),
      w_gate, w_up, w_down)


def _combine_kernel(st_ref, h_ref, code_ref, y_ref, g_ref, o_ref, acc_s,
                    *, cap, win, per, final_norm):
    b = pl.program_id(0)
    j = pl.program_id(1)
    nb = pl.num_programs(1)
    tb = h_ref.shape[1]
    n_exp = y_ref.shape[1]
    iota = lax.broadcasted_iota(I32, (tb, win), 1)
    code = code_ref[0]

    def window(k, lo):
        row0 = pl.multiple_of(jnp.minimum(lo, cap - win), BF16_SUBLANES)
        code_k = jnp.where(code[:, k:k + 1] >= lo, code[:, k:k + 1], -1)
        onehot = jnp.where(iota + row0 == code_k, 1.0, 0.0).astype(BF16)
        return onehot, y_ref[0, k, pl.ds(row0, win), :]

    wins = []
    for k in range(n_exp):
        base = (b * n_exp + k) * (nb * per + 1) + j * per
        wins.append(_window(st_ref[base], st_ref[base + per], win))
    acc = h_ref[0]
    for k in range(0, n_exp, 2):
        oh_a, y_a = window(k, wins[k][0])
        oh_b, y_b = window(k + 1, wins[k + 1][0])
        acc = acc + _dot(jnp.concatenate([oh_a, oh_b], axis=1),
                         jnp.concatenate([y_a, y_b], axis=0))
    acc_s[...] = acc
    for k, (s0, n_win) in enumerate(wins):
        def more(i, carry, k=k, s0=s0):
            acc_s[...] += _dot(*window(k, s0 + i * win))
            return carry
        lax.fori_loop(1, n_win, more, 0)
    acc = acc_s[...]
    o_ref[0] = _rms(acc, g_ref[...]) if final_norm else acc


def _moe_combine(h, code_t, starts, ye, g_final, final_norm):
    bsz, s, d = h.shape
    e, cap = ye.shape[1], ye.shape[2]
    assert e % 2 == 0
    tb = min(TOKEN_BLOCK, s)
    win = min(PICK_WINDOW, cap)
    per = tb // min(GATHER_BLOCK, s)
    kern = functools.partial(_combine_kernel, cap=cap, win=win, per=per, final_norm=final_norm)
    return pl.pallas_call(
        kern,
        grid_spec=pltpu.PrefetchScalarGridSpec(
            num_scalar_prefetch=1,
            grid=(bsz, s // tb),
            in_specs=[pl.BlockSpec((1, tb, d), lambda b, j, st: (b, j, 0)),
                      pl.BlockSpec((1, tb, e), lambda b, j, st: (b, j, 0)),
                      pl.BlockSpec((1, e, cap, d), lambda b, j, st: (b, 0, 0, 0)),
                      pl.BlockSpec((1, d), lambda b, j, st: (0, 0))],
            out_specs=pl.BlockSpec((1, tb, d), lambda b, j, st: (b, j, 0)),
            scratch_shapes=[pltpu.VMEM((tb, d), F32)]),
        out_shape=jax.ShapeDtypeStruct((bsz, s, d), F32),
        compiler_params=_params("parallel", "parallel"),
        name="moe_combine",
    )(starts, h, code_t, ye, g_final.reshape(1, d))


def _rwkv_prep_kernel(h_ref, hp_ref, hn_ref, g_ref, mu_ref, wrkv_ref, w1_ref, a1_ref, g1_ref,
                      w2_ref, a2_ref, g2_ref, w0_ref, a0_ref, kk_ref, ka_ref, rk_ref, hs_ref,
                      r_out, v_out, kn_out, gate_out, bonus_out, lw_out, kd_out, bd_out,
                      xn_s, *, tm):
    i = pl.program_id(1)
    g = g_ref[...]
    keep_prev = jnp.where(i == 0, 0.0, 1.0)
    keep_next = jnp.where(i == pl.num_programs(1) - 1, 0.0, 1.0)
    xn_s[0:HALO, :] = _rms(hp_ref[0], g) * keep_prev
    xn_s[HALO:HALO + tm, :] = _rms(h_ref[0], g)
    xn_s[HALO + tm:, :] = _rms(hn_ref[0], g) * keep_next
    xn = xn_s[pl.ds(HALO, tm), :]
    xx = 0.5 * (xn_s[pl.ds(HALO - 1, tm), :] + xn_s[pl.ds(HALO + 1, tm), :]) - xn

    def mix(j):
        return (xn + xx * mu_ref[j:j + 1, :]).astype(BF16)

    r = _dot(mix(0), wrkv_ref[0])
    k = _dot(mix(1), wrkv_ref[1])
    v = _dot(mix(2), wrkv_ref[2])
    hw = jnp.tanh(_dot(mix(3), w1_ref[...])).astype(BF16)
    ha = _dot(mix(4), a1_ref[...]).astype(BF16)
    hg = jax.nn.sigmoid(_dot(mix(5), g1_ref[...])).astype(BF16)
    gate_out[0] = _dot(hg, g2_ref[...]).astype(BF16)

    hsum = hs_ref[...]
    kk = k * kk_ref[...]
    nrm2 = _head_sum(kk * kk, hsum, two_term=False)
    kn = kk / jnp.maximum(jnp.sqrt(nrm2), 1e-12)
    r_out[0] = r.astype(BF16)
    v_out[0] = v.astype(BF16)
    kn_out[0] = kn.astype(BF16)
    ksum = None
    for n in range(2):
        w_raw = w0_ref[n:n + 1, :] + _dot(hw, w2_ref[n])
        lw_out[n, 0] = -DECAY_SCALE * jax.nn.sigmoid(w_raw)
        a = jax.nn.sigmoid(a0_ref[n:n + 1, :] + _dot(ha, a2_ref[n]))
        kd = k * (1.0 + (a - 1.0) * ka_ref[...])
        kd_out[n, 0] = kd.astype(BF16)
        bd_out[n, 0] = (kn * a).astype(BF16)
        ksum = kd if ksum is None else ksum + kd
    coef = _head_sum(r * ksum * rk_ref[...], hsum, two_term=False)
    bonus_out[0] = (coef * v).astype(BF16)


def _head_sum_matrix(d, n):
    assert d // n <= LANES
    return (jnp.arange(d)[:, None] // n == jnp.arange(LANES)[None, :]).astype(BF16)


def _head_sum(x, hs, two_term):
    x_hi = x.astype(BF16)
    sums = _dot(x_hi, hs)
    if two_term:
        sums = sums + _dot((x - x_hi.astype(F32)).astype(BF16), hs)
    s_hi = sums.astype(BF16)
    out = _dot_nt(s_hi, hs)
    if two_term:
        out = out + _dot_nt((sums - s_hi.astype(F32)).astype(BF16), hs)
    return out


def _rwkv_prep(h, g, mu, w_rkv, w0, w1, w2, a0, a1, a2, g1, g2, k_k, k_a, r_k):
    bsz, s, d = h.shape
    tm = min(RWKV_ROW_TILE, s)
    lora = w1.shape[-1]
    glora = g1.shape[-1]
    w1c = jnp.concatenate([w1[0], w1[1]], axis=1).astype(BF16)
    a1c = jnp.concatenate([a1[0], a1[1]], axis=1).astype(BF16)
    keep = (jnp.arange(2 * lora)[None, :, None] // lora) == jnp.arange(2)[:, None, None]
    w2p = jnp.where(keep, jnp.concatenate([w2, w2], axis=1), 0.0).astype(BF16)
    a2p = jnp.where(keep, jnp.concatenate([a2, a2], axis=1), 0.0).astype(BF16)
    kern = functools.partial(_rwkv_prep_kernel, tm=tm)
    tok = pl.BlockSpec((1, tm, d), lambda b, i: (b, i, 0))
    tok2 = pl.BlockSpec((2, 1, tm, d), lambda b, i: (0, b, i, 0))
    sd = jax.ShapeDtypeStruct
    return pl.pallas_call(
        kern,
        grid=(bsz, s // tm),
        in_specs=_halo_specs(tm, s, d) + [
            _full((1, d)), _full((6, d)), _full((3, d, d)),
            _full((d, 2 * lora)), _full((d, 2 * lora)), _full((d, glora)),
            _full((2, 2 * lora, d)), _full((2, 2 * lora, d)), _full((glora, d)),
            _full((2, d)), _full((2, d)), _full((1, d)), _full((1, d)), _full((1, d)),
            _full((d, LANES))],
        out_specs=[tok, tok, tok, tok, tok, tok2, tok2, tok2],
        out_shape=[sd((bsz, s, d), BF16)] * 5 + [sd((2, bsz, s, d), F32),
                                                 sd((2, bsz, s, d), BF16),
                                                 sd((2, bsz, s, d), BF16)],
        scratch_shapes=[pltpu.VMEM((tm + 2 * HALO, d), F32)],
        compiler_params=_params("parallel", "parallel"),
        name="rwkv_prep",
    )(h, h, h, g.reshape(1, d), mu, w_rkv.astype(BF16), w1c, a1c, g1.astype(BF16),
      w2p, a2p, g2.astype(BF16), w0, a0,
      k_k.reshape(1, d), k_a.reshape(1, d), r_k.reshape(1, d),
      _head_sum_matrix(d, RWKV_HEAD_DIM))


def _wkv_kernel(rf_ref, vf_ref, knf_ref, rr_ref, vr_ref, knr_ref,
                lwf_ref, kdf_ref, bdf_ref, lwr_ref, kdr_ref, bdr_ref,
                yf_ref, yr_ref, qf_s, qr_s, rh_s, y0_s, nm_s, q0_s, gt_s, *, c, ng):
    hd = RWKV_HEAD_DIM
    w = 2 * hd
    j = pl.program_id(2)
    n_steps = pl.num_programs(2)
    carried = (rh_s, y0_s, nm_s, q0_s, gt_s)

    @pl.when(j == 0)
    def _():
        for ref in (qf_s, qr_s) + carried:
            ref[...] = jnp.zeros_like(ref)

    ti = lax.broadcasted_iota(I32, (c, c), 0)
    si = lax.broadcasted_iota(I32, (c, c), 1)
    lane = lax.broadcasted_iota(I32, (1, w), 1)
    m_lo = lane < hd
    ti2 = lax.broadcasted_iota(I32, (c, 2 * c), 0)
    si2 = lax.broadcasted_iota(I32, (c, 2 * c), 1)
    si2 = jnp.where(si2 >= c, si2 - c, si2)
    eye2 = jnp.where(si2 == ti2, 1.0, 0.0)
    dir_masks = (
        (jnp.where(si <= ti, 1.0, 0.0).astype(BF16), si2 <= ti2, si2 < ti2),
        (jnp.where(si >= ti, 1.0, 0.0).astype(BF16), si2 >= ti2, si2 > ti2),
    )
    rr = lax.broadcasted_iota(I32, (2 * c, 2 * c), 0)
    cc = lax.broadcasted_iota(I32, (2 * c, 2 * c), 1)
    bd_mask_c = jnp.where(rr < c, 0, 1) == jnp.where(cc < c, 0, 1)
    rr = lax.broadcasted_iota(I32, (w, w), 0)
    cc = lax.broadcasted_iota(I32, (w, w), 1)
    bd_mask_h = jnp.where(rr < hd, 0, 1) == jnp.where(cc < hd, 0, 1)

    def row_stack(x):
        return jnp.concatenate([jnp.where(m_lo, x, 0.0), jnp.where(m_lo, 0.0, x)],
                               axis=0).astype(BF16)

    def block_diag(xp):
        return jnp.where(bd_mask_c, jnp.concatenate([xp, xp], axis=0), 0.0).astype(BF16)

    fwd = (0, rf_ref, vf_ref, knf_ref, lwf_ref, kdf_ref, bdf_ref)
    rev = (1, rr_ref, vr_ref, knr_ref, lwr_ref, kdr_ref, bdr_ref)
    y_refs = (yf_ref, yr_ref)
    q = [qf_s[...], qr_s[...]]

    def stages(probs, st):
        for dirn, r_ref, v_ref, kn_ref, lw_ref, kd_ref, bd_ref, ci in probs:
            tri_incl, _, _ = dir_masks[dirn]
            sl = pl.ds(ci * c, c)
            lw = lw_ref[0, 0, sl, :]
            lw_hi = lw.astype(BF16)
            lw_lo = (lw - lw_hi.astype(F32)).astype(BF16)
            l_incl = _dot(tri_incl, lw_hi) + _dot(tri_incl, lw_lo)
            st.append(dict(dirn=dirn, sl=sl, lw=lw, l_incl=l_incl,
                           r=r_ref[0, sl, :].astype(F32), v=v_ref[0, sl, :].astype(F32),
                           kn=kn_ref[0, sl, :].astype(F32), kd=kd_ref[0, 0, sl, :].astype(F32),
                           bd=bd_ref[0, 0, sl, :].astype(F32)))
        yield
        for p in st:
            stage_decay(p)
        yield
        for p in st:
            stage_masks(p)
        yield
        for _ in range(max(c.bit_length() - 3, 0)):
            for p in st:
                res = _dot(jnp.concatenate([p["pw"], p["t_p"]], axis=0).astype(BF16),
                           block_diag(p["pw"]))
                p["pw"] = res[0:c]
                p["t_p"] = p["t_p"] + res[c:2 * c]
            yield
        for p in st:
            stage_solve(p)
        yield
        for p in st:
            stage_maps(p)
        yield

    def handover(off_f, off_r):
        for i in range(ng):
            for dirn, slot, row in ((0, i, off_f + i * c), (1, ng + i, off_r + (ng - 1 - i) * c)):
                q_b = q[dirn].astype(BF16)
                y_refs[dirn][0, pl.ds(pl.multiple_of(row, c), c), :] = (
                    _dot_nt(rh_s[slot], q_b) + y0_s[slot])
                q[dirn] = q[dirn] * gt_s[slot] + _dot(q_b, nm_s[slot]) + q0_s[slot]
            yield

    def stage_decay(p):
        l_incl = p["l_incl"]
        l_tot = l_incl[0:1, :] if p["dirn"] else l_incl[c - 1:c, :]
        g_inv = jnp.exp(-l_incl)
        g_end = jnp.exp(l_tot - l_incl)
        p["g_tot"] = jnp.exp(l_tot)
        p["rt"] = p["r"] * jnp.exp(l_incl)
        p["at"] = -p["kn"] * jnp.exp(l_incl - p["lw"])
        p["v_rs"] = row_stack(p["v"])
        p["bh"] = (p["bd"] * g_end).astype(BF16)
        p["kh"] = (p["kd"] * g_end).astype(BF16)
        lhs = jnp.concatenate([p["at"], p["rt"]], axis=0).astype(BF16)
        rhs = jnp.concatenate([row_stack(p["bd"] * g_inv), row_stack(p["kd"] * g_inv)],
                              axis=0)
        p["gm"] = _dot_nt(lhs, rhs)

    def stage_masks(p):
        _, incl2, strict2 = dir_masks[p["dirn"]]
        gm = p.pop("gm")
        a_ab = jnp.where(strict2, gm[0:c, 0:2 * c], 0.0)
        a_ak = jnp.where(strict2, gm[0:c, 2 * c:4 * c], 0.0)
        a_rb = jnp.where(incl2, gm[c:2 * c, 0:2 * c], 0.0)
        a_rk = jnp.where(incl2, gm[c:2 * c, 2 * c:4 * c], 0.0)
        p["a_rb"] = a_rb.astype(BF16)
        av = _dot(jnp.concatenate([a_ak, a_rk], axis=0).astype(BF16), p["v_rs"])
        p["akv"] = av[0:c]
        p["arkv"] = av[c:2 * c]
        p["t_p"] = eye2 + a_ab
        p["pw"] = _dot(a_ab.astype(BF16), block_diag(a_ab))

    def stage_solve(p):
        t_p = p["t_p"] + _dot(p["t_p"].astype(BF16), block_diag(p["pw"]))
        wu = _dot(t_p.astype(BF16),
                  jnp.concatenate([row_stack(p["at"]), row_stack(p["akv"])], axis=1))
        p["w_m"] = wu[:, 0:w]
        p["u_t"] = wu[:, w:2 * w]

    def stage_maps(p):
        ry = _dot(p["a_rb"],
                  jnp.concatenate([row_stack(p["w_m"]), row_stack(p["u_t"])], axis=1))
        p["r_hat"] = (p["rt"] + ry[:, 0:w]).astype(BF16)
        p["y0"] = ry[:, w:2 * w] + p["arkv"]
        p["n_m"] = jnp.where(bd_mask_h, _dot_tn(p["w_m"].astype(BF16), p["bh"]),
                             0.0).astype(BF16)
        p["q0"] = jnp.where(
            bd_mask_h,
            _dot_tn(jnp.concatenate([p["u_t"], p["v"]], axis=0).astype(BF16),
                    jnp.concatenate([p["bh"], p["kh"]], axis=0)), 0.0)

    cg = ng * c
    probs = [fwd + (ci,) for ci in range(ng)] + [rev + (ci,) for ci in range(ng - 1, -1, -1)]
    st = []
    pending = handover(jnp.maximum(j - 1, 0) * cg, jnp.minimum(n_steps - j, n_steps - 1) * cg)
    for _ in stages(probs, st):
        next(pending, None)
    for _ in pending:
        pass
    for slot, p in enumerate(st):
        rh_s[slot] = p["r_hat"]
        y0_s[slot] = p["y0"]
        nm_s[slot] = p["n_m"]
        q0_s[slot] = p["q0"]
        gt_s[slot] = p["g_tot"]
    qf_s[...] = q[0]
    qr_s[...] = q[1]

    @pl.when(j == n_steps - 1)
    def _():
        q[0] = qf_s[...]
        q[1] = qr_s[...]
        for _ in handover(j * cg, 0):
            pass


def _wkv(r, v, kn, lw, kd, bd):
    bsz, s, d = r.shape
    c = min(CHUNK, s)
    ng = min(CHUNK_GROUP, s // c)
    cg = c * ng
    n_steps = s // cg
    w = 2 * RWKV_HEAD_DIM
    tok_f = pl.BlockSpec((1, cg, w), lambda b, p, j: (b, j, p))
    tok_r = pl.BlockSpec((1, cg, w), lambda b, p, j: (b, n_steps - 1 - j, p))
    dir_f = pl.BlockSpec((1, 1, cg, w), lambda b, p, j: (0, b, j, p))
    dir_r = pl.BlockSpec((1, 1, cg, w), lambda b, p, j: (1, b, n_steps - 1 - j, p))
    kern = functools.partial(_wkv_kernel, c=c, ng=ng)
    seq = pl.BlockSpec((1, s, w), lambda b, p, j: (b, 0, p))
    return pl.pallas_call(
        kern,
        grid=(bsz, d // w, n_steps),
        in_specs=[tok_f, tok_f, tok_f, tok_r, tok_r, tok_r,
                  dir_f, dir_f, dir_f, dir_r, dir_r, dir_r],
        out_specs=[seq, seq],
        out_shape=[jax.ShapeDtypeStruct((bsz, s, d), F32)] * 2,
        scratch_shapes=[pltpu.VMEM((w, w), F32), pltpu.VMEM((w, w), F32),
                        pltpu.VMEM((2 * ng, c, w), BF16), pltpu.VMEM((2 * ng, c, w), F32),
                        pltpu.VMEM((2 * ng, w, w), BF16), pltpu.VMEM((2 * ng, w, w), F32),
                        pltpu.VMEM((2 * ng, 1, w), F32)],
        compiler_params=_params("parallel", "parallel", "arbitrary"),
        name="wkv",
    )(r, v, kn, r, v, kn, lw, kd, bd, lw, kd, bd)


def _rwkv_post_kernel(h_ref, yf_ref, yr_ref, bonus_ref, gate_ref, lnw_ref, lnb_ref, hs_ref,
                      wout_ref, o_ref, *, n):
    hsum = hs_ref[...]
    y = yf_ref[0] + yr_ref[0]
    mean = _head_sum(y, hsum, two_term=True) * (1.0 / n)
    yc = y - mean
    var = _head_sum(yc * yc, hsum, two_term=False) * (1.0 / n)
    yn = yc * lax.rsqrt(var + GN_EPS) * lnw_ref[...] + lnb_ref[...]
    yn = yn + bonus_ref[0].astype(F32)
    z = (yn * gate_ref[0].astype(F32)).astype(BF16)
    o_ref[0] = h_ref[0] + _dot(z, wout_ref[...])


def _rwkv_post(h, yf, yr, bonus, gate, ln_w, ln_b, w_out):
    bsz, s, d = h.shape
    tm = min(ROW_TILE, s)
    tok = pl.BlockSpec((1, tm, d), lambda b, i: (b, i, 0))
    kern = functools.partial(_rwkv_post_kernel, n=RWKV_HEAD_DIM)
    return pl.pallas_call(
        kern,
        grid=(bsz, s // tm),
        in_specs=[tok, tok, tok, tok, tok, _full((1, d)), _full((1, d)), _full((d, LANES)),
                  _full((d, d))],
        out_specs=tok,
        out_shape=jax.ShapeDtypeStruct((bsz, s, d), F32),
        compiler_params=_params("parallel", "parallel"),
        name="rwkv_post",
    )(h, yf, yr, bonus, gate, ln_w.reshape(1, d), ln_b.reshape(1, d),
      _head_sum_matrix(d, RWKV_HEAD_DIM), w_out.astype(BF16))


def kernel(x, mem, norm_mix, norm_xattn, norm_mem, norm_ffn, norm_final,
           conv_w_in, conv_w, conv_w_out,
           rwkv_mu, rwkv_w_rkv, rwkv_w0, rwkv_w1, rwkv_w2, rwkv_a0, rwkv_a1, rwkv_a2,
           rwkv_g1, rwkv_g2, rwkv_k_k, rwkv_k_a, rwkv_r_k, rwkv_ln_w, rwkv_ln_b, rwkv_w_out,
           xattn_w_q, xattn_w_kv, xattn_w_o,
           moe_router, moe_w_gate, moe_w_up, moe_w_down):
    depth = norm_mix.shape[0]
    bsz, s, d = x.shape
    n_exp = moe_router.shape[-1]
    cap = CAPACITY_FACTOR * s // n_exp
    h = x
    for i in range(depth):
        j = i // N_MIXERS
        if i % N_MIXERS == 0:
            h = _conv_layer(h, norm_mix[i], conv_w_in[j], conv_w[j], conv_w_out[j])
        else:
            r, v, kn, gate, bonus, lw, kd, bd = _rwkv_prep(
                h, norm_mix[i], rwkv_mu[j], rwkv_w_rkv[j], rwkv_w0[j], rwkv_w1[j], rwkv_w2[j],
                rwkv_a0[j], rwkv_a1[j], rwkv_a2[j], rwkv_g1[j], rwkv_g2[j],
                rwkv_k_k[j], rwkv_k_a[j], rwkv_r_k[j])
            yf, yr = _wkv(r, v, kn, lw, kd, bd)
            h = _rwkv_post(h, yf, yr, bonus, gate, rwkv_ln_w[j], rwkv_ln_b[j], rwkv_w_out[j])
        kv = _kv_proj(mem, norm_mem[i], xattn_w_kv[i])
        h, xn, probs_t = _xattn_layer(h, kv, norm_xattn[i], xattn_w_q[i], xattn_w_o[i],
                                      norm_ffn[i], moe_router[i])
        code, starts = _select(probs_t, cap)
        ye = _expert_ffn(xn, code, probs_t, starts, moe_w_gate, moe_w_up, moe_w_down, i, cap)
        h = _moe_combine(h, jnp.swapaxes(code, 1, 2), starts, ye, norm_final,
                         final_norm=(i == depth - 1))
    return h
```

```python
import functools

import jax
import jax.numpy as jnp
from jax import lax
from jax.experimental import pallas as pl
from jax.experimental.pallas import tpu as pltpu

F32 = jnp.float32
BF16 = jnp.bfloat16
I32 = jnp.int32

N_MIXERS = 2
RWKV_HEAD_DIM = 64
XATTN_HEADS = 4
CAPACITY_FACTOR = 2
GN_EPS = 64e-5
RMS_EPS = 1e-6
DECAY_SCALE = 0.6065306597126334

V7X_VMEM_LIMIT_BYTES = 56 * 1024 * 1024
LANES = 128
BF16_SUBLANES = 16

ROW_TILE = 512
RWKV_ROW_TILE = 512
HALO = BF16_SUBLANES
CHUNK = 64
CHUNK_GROUP = 8
TOKEN_BLOCK = 512
PICK_WINDOW = 128
GATHER_BLOCK = 256
GATHER_WINDOW = 64


def _params(*semantics):
    return pltpu.CompilerParams(dimension_semantics=semantics,
                                vmem_limit_bytes=V7X_VMEM_LIMIT_BYTES)


def _rms(x, g):
    return x * lax.rsqrt(jnp.mean(x * x, axis=-1, keepdims=True) + RMS_EPS) * g


def _dot(a, b):
    return jnp.dot(a, b, preferred_element_type=F32)


def _dot_nt(a, b):
    return lax.dot_general(a, b, (((1,), (1,)), ((), ())), preferred_element_type=F32)


def _dot_tn(a, b):
    return lax.dot_general(a, b, (((0,), (0,)), ((), ())), preferred_element_type=F32)


def _full(shape):
    n = len(shape)
    return pl.BlockSpec(shape, lambda *_: (0,) * n)


def _conv_kernel(h_ref, hp_ref, hn_ref, g_ref, win_ref, cw_ref, wout_ref, o_ref,
                 xn_s, u_s, gate_s, *, tm, d, cb):
    i = pl.program_id(1)
    g = g_ref[...]
    x = h_ref[0]
    xn_s[0:HALO, :] = _rms(hp_ref[0], g).astype(BF16)
    xn_s[HALO:HALO + tm, :] = _rms(x, g).astype(BF16)
    xn_s[HALO + tm:, :] = _rms(hn_ref[0], g).astype(BF16)
    rows = tm + 2 * HALO
    row = lax.broadcasted_iota(I32, (rows, 1), 0)
    lo = jnp.where(i == 0, HALO, 0)
    hi = jnp.where(i == pl.num_programs(1) - 1, HALO + tm, rows)
    pad = jnp.logical_or(row < lo, row >= hi)
    xa = xn_s[...]
    for c0 in range(0, d, cb):
        c_gate = _dot(xa, win_ref[:, d + c0:d + c0 + cb])
        hx = _dot(xa, win_ref[:, 2 * d + c0:2 * d + c0 + cb])
        u_s[...] = jnp.where(pad, 0.0, c_gate * hx)
        conv = (u_s[pl.ds(HALO - 1, tm), :] * cw_ref[0:1, c0:c0 + cb]
                + u_s[pl.ds(HALO, tm), :] * cw_ref[1:2, c0:c0 + cb]
                + u_s[pl.ds(HALO + 1, tm), :] * cw_ref[2:3, c0:c0 + cb])
        b_gate = _dot(xn_s[HALO:HALO + tm, :], win_ref[:, c0:c0 + cb])
        gate_s[:, c0:c0 + cb] = (b_gate * conv).astype(BF16)
    o_ref[0] = x + _dot(gate_s[...], wout_ref[...])


def _halo_specs(tm, s, d):
    nb = tm // HALO
    last = s // HALO - 1
    return [
        pl.BlockSpec((1, tm, d), lambda b, i: (b, i, 0)),
        pl.BlockSpec((1, HALO, d), lambda b, i: (b, jnp.maximum(i * nb - 1, 0), 0)),
        pl.BlockSpec((1, HALO, d), lambda b, i: (b, jnp.minimum((i + 1) * nb, last), 0)),
    ]


def _conv_layer(h, g, w_in, conv_w, w_out):
    bsz, s, d = h.shape
    tm = min(ROW_TILE, s)
    cb = 512
    kern = functools.partial(_conv_kernel, tm=tm, d=d, cb=cb)
    return pl.pallas_call(
        kern,
        grid=(bsz, s // tm),
        in_specs=_halo_specs(tm, s, d) + [
            _full((1, d)), _full((d, 3 * d)), _full((3, d)), _full((d, d))],
        out_specs=pl.BlockSpec((1, tm, d), lambda b, i: (b, i, 0)),
        out_shape=jax.ShapeDtypeStruct((bsz, s, d), F32),
        scratch_shapes=[pltpu.VMEM((tm + 2 * HALO, d), BF16),
                        pltpu.VMEM((tm + 2 * HALO, cb), F32),
                        pltpu.VMEM((tm, d), BF16)],
        compiler_params=_params("parallel", "parallel"),
        name="conv_mixer",
    )(h, h, h, g.reshape(1, d), w_in.astype(BF16), conv_w, w_out.astype(BF16))


def _kv_kernel(m_ref, g_ref, w_ref, o_ref):
    xn = _rms(m_ref[0], g_ref[...]).astype(BF16)
    o_ref[0] = _dot(xn, w_ref[...]).astype(BF16)


def _kv_proj(mem, g, w_kv):
    bsz, m, d = mem.shape
    return pl.pallas_call(
        _kv_kernel,
        grid=(bsz,),
        in_specs=[pl.BlockSpec((1, m, d), lambda b: (b, 0, 0)), _full((1, d)), _full((d, 2 * d))],
        out_specs=pl.BlockSpec((1, m, 2 * d), lambda b: (b, 0, 0)),
        out_shape=jax.ShapeDtypeStruct((bsz, m, 2 * d), BF16),
        compiler_params=_params("parallel"),
        name="kv_proj",
    )(mem, g.reshape(1, d), w_kv.astype(BF16))


def _xattn_kernel(h_ref, gx_ref, wq_ref, kv_ref, wo_ref, gf_ref, rhi_ref, rlo_ref,
                  h_out, xn_out, probs_out, o_s, *, d, heads):
    x = h_ref[0]
    xn = _rms(x, gx_ref[...]).astype(BF16)
    hd = d // heads
    q = (_dot(xn, wq_ref[...]) * (hd ** -0.5)).astype(BF16)
    kv = kv_ref[0]
    for a in range(heads):
        sc = _dot_nt(q[:, a * hd:(a + 1) * hd], kv[:, a * hd:(a + 1) * hd])
        p = jnp.exp(sc - jnp.max(sc, axis=-1, keepdims=True))
        l = jnp.sum(p, axis=-1, keepdims=True)
        o = _dot(p.astype(BF16), kv[:, d + a * hd:d + (a + 1) * hd]) / l
        o_s[:, a * hd:(a + 1) * hd] = o.astype(BF16)
    hn = x + _dot(o_s[...], wo_ref[...])
    h_out[0] = hn
    xf = _rms(hn, gf_ref[...])
    hi = xf.astype(BF16)
    xn_out[0] = hi
    lo = (xf - hi.astype(F32)).astype(BF16)
    lg = _dot_nt(rhi_ref[...], hi) + _dot_nt(rhi_ref[...], lo) + _dot_nt(rlo_ref[...], hi)
    e = jnp.exp(lg - jnp.max(lg, axis=0, keepdims=True))
    probs_out[0] = e / jnp.sum(e, axis=0, keepdims=True)


def _xattn_layer(h, kv, gx, w_q, w_o, gf, router):
    bsz, s, d = h.shape
    m = kv.shape[1]
    e = router.shape[1]
    tm = min(ROW_TILE, s)
    rt = router.T
    r_hi = rt.astype(BF16)
    r_lo = (rt - r_hi.astype(F32)).astype(BF16)
    kern = functools.partial(_xattn_kernel, d=d, heads=XATTN_HEADS)
    return pl.pallas_call(
        kern,
        grid=(bsz, s // tm),
        in_specs=[pl.BlockSpec((1, tm, d), lambda b, i: (b, i, 0)),
                  _full((1, d)), _full((d, d)),
                  pl.BlockSpec((1, m, 2 * d), lambda b, i: (b, 0, 0)),
                  _full((d, d)), _full((1, d)), _full((e, d)), _full((e, d))],
        out_specs=[pl.BlockSpec((1, tm, d), lambda b, i: (b, i, 0)),
                   pl.BlockSpec((1, tm, d), lambda b, i: (b, i, 0)),
                   pl.BlockSpec((1, e, tm), lambda b, i: (b, 0, i))],
        out_shape=[jax.ShapeDtypeStruct((bsz, s, d), F32),
                   jax.ShapeDtypeStruct((bsz, s, d), BF16),
                   jax.ShapeDtypeStruct((bsz, e, s), F32)],
        scratch_shapes=[pltpu.VMEM((tm, d), BF16)],
        compiler_params=_params("parallel", "parallel"),
        name="xattn_router",
    )(h, gx.reshape(1, d), w_q.astype(BF16), kv, w_o.astype(BF16), gf.reshape(1, d), r_hi, r_lo)


def _select_kernel(p_ref, code_ref, start_ref, *, cap, blk):
    p = p_ref[0]
    e, s = p.shape
    bits = pltpu.bitcast(p, I32)

    def count(mask):
        return jnp.sum(mask.astype(F32), axis=1, keepdims=True)

    def search(k, t):
        cand = jnp.bitwise_or(t, jnp.left_shift(jnp.int32(1), 30 - k))
        return jnp.where(count(bits >= cand) >= cap, cand, t)

    thr = lax.fori_loop(0, 31, search, jnp.zeros((e, 1), I32))
    gt = bits > thr
    eq = bits == thr
    need = cap - count(gt)

    r = lax.broadcasted_iota(I32, (blk, blk), 0)
    c = lax.broadcasted_iota(I32, (blk, blk), 1)
    tri = jnp.where(r < c, 1.0, 0.0).astype(BF16)

    def prefix(mask_f):
        out, carries = [], []
        carry = jnp.zeros((e, 1), F32)
        for j in range(0, s, blk):
            mb = mask_f[:, j:j + blk]
            carries.append(carry)
            out.append(_dot(mb.astype(BF16), tri) + carry)
            carry = carry + jnp.sum(mb, axis=1, keepdims=True)
        return out, carries + [carry]

    eq_f = eq.astype(F32)
    tie_rank, _ = prefix(eq_f)
    for j0, tr in zip(range(0, s, blk), tie_rank):
        sel_b = jnp.logical_or(gt[:, j0:j0 + blk],
                               jnp.logical_and(eq[:, j0:j0 + blk], tr < need))
        code_ref[0, :, j0:j0 + blk] = sel_b.astype(I32)
    sel_f = code_ref[0].astype(F32)
    rank, starts = prefix(sel_f)
    for j0, rk in zip(range(0, s, blk), rank):
        code_ref[0, :, j0:j0 + blk] = jnp.where(sel_f[:, j0:j0 + blk] > 0.0, rk.astype(I32), -1)
    start_ref[...] = jnp.zeros_like(start_ref)
    for j, st in enumerate(starts):
        start_ref[0, :, j:j + 1] = st.astype(I32)


def _select(probs_t, cap):
    bsz, e, s = probs_t.shape
    blk = min(GATHER_BLOCK, s)
    nb1 = s // blk + 1
    assert nb1 <= LANES
    kern = functools.partial(_select_kernel, cap=cap, blk=blk)
    code, starts = pl.pallas_call(
        kern,
        grid=(bsz,),
        in_specs=[pl.BlockSpec((1, e, s), lambda b: (b, 0, 0))],
        out_specs=[pl.BlockSpec((1, e, s), lambda b: (b, 0, 0)),
                   pl.BlockSpec((1, e, LANES), lambda b: (b, 0, 0))],
        out_shape=[jax.ShapeDtypeStruct((bsz, e, s), I32),
                   jax.ShapeDtypeStruct((bsz, e, LANES), I32)],
        compiler_params=_params("parallel"),
        name="expert_choice_select",
    )(probs_t)
    return code, starts[:, :, :nb1].reshape(-1)


def _window(s_lo, s_hi, win):
    s0 = jnp.bitwise_and(s_lo, -BF16_SUBLANES)
    n_win = jnp.where(s_hi > s_lo, lax.shift_right_logical(s_hi - s0 + (win - 1),
                                                           win.bit_length() - 1), 0)
    return s0, n_win


def _gather_kernel(st_ref, x_ref, code_ref, p_ref, xe_ref, aff_ref, xe_s, aff_s, *, cap, blk, win):
    b = pl.program_id(0)
    k = pl.program_id(1)
    s = x_ref.shape[1]
    nb = s // blk

    base = (b * pl.num_programs(1) + k) * (nb + 1)
    iota = lax.broadcasted_iota(I32, (win, blk), 0)

    def gather(j, row0):
        onehot = jnp.where(iota + row0 == code_ref[0, :, j * blk:(j + 1) * blk], 1.0, 0.0)
        part = _dot(onehot.astype(BF16), x_ref[0, j * blk:(j + 1) * blk, :])
        xe_s[pl.ds(row0, win), :] += part
        aff_s[pl.ds(row0, win), :] += jnp.sum(onehot * p_ref[0, :, j * blk:(j + 1) * blk],
                                              axis=1, keepdims=True)

    xe_s[...] = jnp.zeros_like(xe_s)
    aff_s[...] = jnp.zeros_like(aff_s)
    wins = [_window(st_ref[base + j], st_ref[base + j + 1], win) for j in range(nb)]
    for j, (s0, _) in enumerate(wins):
        gather(j, pl.multiple_of(s0, BF16_SUBLANES))
    for j, (s0, n_win) in enumerate(wins):
        def more(i, carry, j=j, s0=s0):
            gather(j, pl.multiple_of(s0 + i * win, BF16_SUBLANES))
            return carry
        lax.fori_loop(1, n_win, more, 0)

    xe_ref[0, 0] = xe_s[0:cap, :].astype(BF16)
    aff_ref[0, 0] = aff_s[0:cap, :]


def _expert_gather(xn, code, probs_t, starts, cap):
    bsz, s, d = xn.shape
    e = code.shape[1]
    blk = min(GATHER_BLOCK, s)
    win = min(GATHER_WINDOW, cap)
    kern = functools.partial(_gather_kernel, cap=cap, blk=blk, win=win)
    row = pl.BlockSpec((1, 1, s), lambda b, k, st: (b * e + k, 0, 0))
    return pl.pallas_call(
        kern,
        grid_spec=pltpu.PrefetchScalarGridSpec(
            num_scalar_prefetch=1,
            grid=(bsz, e),
            in_specs=[pl.BlockSpec((1, s, d), lambda b, k, st: (b, 0, 0)), row, row],
            out_specs=[pl.BlockSpec((1, 1, cap, d), lambda b, k, st: (b, k, 0, 0)),
                       pl.BlockSpec((1, 1, cap, 1), lambda b, k, st: (b, k, 0, 0))],
            scratch_shapes=[pltpu.VMEM((cap + win, d), F32), pltpu.VMEM((cap + win, 1), F32)]),
        out_shape=[jax.ShapeDtypeStruct((bsz, e, cap, d), BF16),
                   jax.ShapeDtypeStruct((bsz, e, cap, 1), F32)],
        compiler_params=_params("parallel", "parallel"),
        name="expert_gather",
    )(starts, xn, code.reshape(bsz * e, 1, s), probs_t.reshape(bsz * e, 1, s))


def _ffn_kernel(xe_ref, aff_ref, wg_ref, wu_ref, wd_ref, y_ref, wg_s, wu_s, wd_s):
    @pl.when(pl.program_id(1) == 0)
    def _():
        wg_s[...] = wg_ref[0, 0].astype(BF16)
        wu_s[...] = wu_ref[0, 0].astype(BF16)
        wd_s[...] = wd_ref[0, 0].astype(BF16)

    xe = xe_ref[0, 0]
    gt = _dot(xe, wg_s[...])
    up = _dot(xe, wu_s[...])
    hid = (gt * jax.nn.sigmoid(gt) * up).astype(BF16)
    y_ref[0, 0] = (_dot(hid, wd_s[...]) * aff_ref[0, 0]).astype(BF16)


def _expert_ffn(xe, aff, w_gate, w_up, w_down, layer):
    bsz, e, cap, d = xe.shape
    f = w_gate.shape[-1]
    return pl.pallas_call(
        _ffn_kernel,
        grid=(e, bsz),
        in_specs=[pl.BlockSpec((1, 1, cap, d), lambda k, b: (b, k, 0, 0)),
                  pl.BlockSpec((1, 1, cap, 1), lambda k, b: (b, k, 0, 0)),
                  pl.BlockSpec((1, 1, d, f), lambda k, b: (layer, k, 0, 0)),
                  pl.BlockSpec((1, 1, d, f), lambda k, b: (layer, k, 0, 0)),
                  pl.BlockSpec((1, 1, f, d), lambda k, b: (layer, k, 0, 0))],
        out_specs=pl.BlockSpec((1, 1, cap, d), lambda k, b: (b, k, 0, 0)),
        out_shape=jax.ShapeDtypeStruct((bsz, e, cap, d), BF16),
        scratch_shapes=[pltpu.VMEM((d, f), BF16), pltpu.VMEM((d, f), BF16),
                        pltpu.VMEM((f, d), BF16)],
        compiler_params=_params("arbitrary", "arbitrary"),
        name="expert_ffn",
    )(xe, aff, w_gate, w_up, w_down)


def _combine_kernel(st_ref, h_ref, code_ref, y_ref, g_ref, o_ref, acc_s,
                    *, cap, win, per, final_norm):
    b = pl.program_id(0)
    j = pl.program_id(1)
    nb = pl.num_programs(1)
    tb = h_ref.shape[1]
    n_exp = y_ref.shape[1]
    iota = lax.broadcasted_iota(I32, (tb, win), 1)
    code = code_ref[0]

    def window(k, lo):
        row0 = pl.multiple_of(jnp.minimum(lo, cap - win), BF16_SUBLANES)
        code_k = jnp.where(code[:, k:k + 1] >= lo, code[:, k:k + 1], -1)
        onehot = jnp.where(iota + row0 == code_k, 1.0, 0.0).astype(BF16)
        return onehot, y_ref[0, k, pl.ds(row0, win), :]

    wins = []
    for k in range(n_exp):
        base = (b * n_exp + k) * (nb * per + 1) + j * per
        wins.append(_window(st_ref[base], st_ref[base + per], win))
    acc = h_ref[0]
    for k in range(0, n_exp, 2):
        oh_a, y_a = window(k, wins[k][0])
        oh_b, y_b = window(k + 1, wins[k + 1][0])
        acc = acc + _dot(jnp.concatenate([oh_a, oh_b], axis=1),
                         jnp.concatenate([y_a, y_b], axis=0))
    acc_s[...] = acc
    for k, (s0, n_win) in enumerate(wins):
        def more(i, carry, k=k, s0=s0):
            acc_s[...] += _dot(*window(k, s0 + i * win))
            return carry
        lax.fori_loop(1, n_win, more, 0)
    acc = acc_s[...]
    o_ref[0] = _rms(acc, g_ref[...]) if final_norm else acc


def _moe_combine(h, code_t, starts, ye, g_final, final_norm):
    bsz, s, d = h.shape
    e, cap = ye.shape[1], ye.shape[2]
    assert e % 2 == 0
    tb = min(TOKEN_BLOCK, s)
    win = min(PICK_WINDOW, cap)
    per = tb // min(GATHER_BLOCK, s)
    kern = functools.partial(_combine_kernel, cap=cap, win=win, per=per, final_norm=final_norm)
    return pl.pallas_call(
        kern,
        grid_spec=pltpu.PrefetchScalarGridSpec(
            num_scalar_prefetch=1,
            grid=(bsz, s // tb),
            in_specs=[pl.BlockSpec((1, tb, d), lambda b, j, st: (b, j, 0)),
                      pl.BlockSpec((1, tb, e), lambda b, j, st: (b, j, 0)),
                      pl.BlockSpec((1, e, cap, d), lambda b, j, st: (b, 0, 0, 0)),
                      pl.BlockSpec((1, d), lambda b, j, st: (0, 0))],
            out_specs=pl.BlockSpec((1, tb, d), lambda b, j, st: (b, j, 0)),
            scratch_shapes=[pltpu.VMEM((tb, d), F32)]),
        out_shape=jax.ShapeDtypeStruct((bsz, s, d), F32),
        compiler_params=_params("parallel", "parallel"),
        name="moe_combine",
    )(starts, h, code_t, ye, g_final.reshape(1, d))


def _rwkv_prep_kernel(h_ref, hp_ref, hn_ref, g_ref, mu_ref, wrkv_ref, w1_ref, a1_ref, g1_ref,
                      w2_ref, a2_ref, g2_ref, w0_ref, a0_ref, kk_ref, ka_ref, rk_ref, hs_ref,
                      r_out, v_out, kn_out, gate_out, bonus_out, lw_out, kd_out, bd_out,
                      xn_s, *, tm):
    i = pl.program_id(1)
    g = g_ref[...]
    keep_prev = jnp.where(i == 0, 0.0, 1.0)
    keep_next = jnp.where(i == pl.num_programs(1) - 1, 0.0, 1.0)
    xn_s[0:HALO, :] = _rms(hp_ref[0], g) * keep_prev
    xn_s[HALO:HALO + tm, :] = _rms(h_ref[0], g)
    xn_s[HALO + tm:, :] = _rms(hn_ref[0], g) * keep_next
    xn = xn_s[pl.ds(HALO, tm), :]
    xx = 0.5 * (xn_s[pl.ds(HALO - 1, tm), :] + xn_s[pl.ds(HALO + 1, tm), :]) - xn

    def mix(j):
        return (xn + xx * mu_ref[j:j + 1, :]).astype(BF16)

    r = _dot(mix(0), wrkv_ref[0])
    k = _dot(mix(1), wrkv_ref[1])
    v = _dot(mix(2), wrkv_ref[2])
    hw = jnp.tanh(_dot(mix(3), w1_ref[...])).astype(BF16)
    ha = _dot(mix(4), a1_ref[...]).astype(BF16)
    hg = jax.nn.sigmoid(_dot(mix(5), g1_ref[...])).astype(BF16)
    gate_out[0] = _dot(hg, g2_ref[...]).astype(BF16)

    hsum = hs_ref[...]
    kk = k * kk_ref[...]
    nrm2 = _head_sum(kk * kk, hsum, two_term=False)
    kn = kk / jnp.maximum(jnp.sqrt(nrm2), 1e-12)
    r_out[0] = r.astype(BF16)
    v_out[0] = v.astype(BF16)
    kn_out[0] = kn.astype(BF16)
    ksum = None
    for n in range(2):
        w_raw = w0_ref[n:n + 1, :] + _dot(hw, w2_ref[n])
        lw_out[n, 0] = -DECAY_SCALE * jax.nn.sigmoid(w_raw)
        a = jax.nn.sigmoid(a0_ref[n:n + 1, :] + _dot(ha, a2_ref[n]))
        kd = k * (1.0 + (a - 1.0) * ka_ref[...])
        kd_out[n, 0] = kd.astype(BF16)
        bd_out[n, 0] = (kn * a).astype(BF16)
        ksum = kd if ksum is None else ksum + kd
    coef = _head_sum(r * ksum * rk_ref[...], hsum, two_term=False)
    bonus_out[0] = (coef * v).astype(BF16)


def _head_sum_matrix(d, n):
    assert d // n <= LANES
    return (jnp.arange(d)[:, None] // n == jnp.arange(LANES)[None, :]).astype(BF16)


def _head_sum(x, hs, two_term):
    x_hi = x.astype(BF16)
    sums = _dot(x_hi, hs)
    if two_term:
        sums = sums + _dot((x - x_hi.astype(F32)).astype(BF16), hs)
    s_hi = sums.astype(BF16)
    out = _dot_nt(s_hi, hs)
    if two_term:
        out = out + _dot_nt((sums - s_hi.astype(F32)).astype(BF16), hs)
    return out


def _rwkv_prep(h, g, mu, w_rkv, w0, w1, w2, a0, a1, a2, g1, g2, k_k, k_a, r_k):
    bsz, s, d = h.shape
    tm = min(RWKV_ROW_TILE, s)
    lora = w1.shape[-1]
    glora = g1.shape[-1]
    w1c = jnp.concatenate([w1[0], w1[1]], axis=1).astype(BF16)
    a1c = jnp.concatenate([a1[0], a1[1]], axis=1).astype(BF16)
    keep = (jnp.arange(2 * lora)[None, :, None] // lora) == jnp.arange(2)[:, None, None]
    w2p = jnp.where(keep, jnp.concatenate([w2, w2], axis=1), 0.0).astype(BF16)
    a2p = jnp.where(keep, jnp.concatenate([a2, a2], axis=1), 0.0).astype(BF16)
    kern = functools.partial(_rwkv_prep_kernel, tm=tm)
    tok = pl.BlockSpec((1, tm, d), lambda b, i: (b, i, 0))
    tok2 = pl.BlockSpec((2, 1, tm, d), lambda b, i: (0, b, i, 0))
    sd = jax.ShapeDtypeStruct
    return pl.pallas_call(
        kern,
        grid=(bsz, s // tm),
        in_specs=_halo_specs(tm, s, d) + [
            _full((1, d)), _full((6, d)), _full((3, d, d)),
            _full((d, 2 * lora)), _full((d, 2 * lora)), _full((d, glora)),
            _full((2, 2 * lora, d)), _full((2, 2 * lora, d)), _full((glora, d)),
            _full((2, d)), _full((2, d)), _full((1, d)), _full((1, d)), _full((1, d)),
            _full((d, LANES))],
        out_specs=[tok, tok, tok, tok, tok, tok2, tok2, tok2],
        out_shape=[sd((bsz, s, d), BF16)] * 5 + [sd((2, bsz, s, d), F32),
                                                 sd((2, bsz, s, d), BF16),
                                                 sd((2, bsz, s, d), BF16)],
        scratch_shapes=[pltpu.VMEM((tm + 2 * HALO, d), F32)],
        compiler_params=_params("parallel", "parallel"),
        name="rwkv_prep",
    )(h, h, h, g.reshape(1, d), mu, w_rkv.astype(BF16), w1c, a1c, g1.astype(BF16),
      w2p, a2p, g2.astype(BF16), w0, a0,
      k_k.reshape(1, d), k_a.reshape(1, d), r_k.reshape(1, d),
      _head_sum_matrix(d, RWKV_HEAD_DIM))


def _wkv_kernel(rf_ref, vf_ref, knf_ref, rr_ref, vr_ref, knr_ref,
                lwf_ref, kdf_ref, bdf_ref, lwr_ref, kdr_ref, bdr_ref,
                yf_ref, yr_ref, qf_s, qr_s, rh_s, y0_s, nm_s, q0_s, gt_s, *, c, ng):
    hd = RWKV_HEAD_DIM
    w = 2 * hd
    j = pl.program_id(2)
    n_steps = pl.num_programs(2)
    carried = (rh_s, y0_s, nm_s, q0_s, gt_s)

    @pl.when(j == 0)
    def _():
        for ref in (qf_s, qr_s) + carried:
            ref[...] = jnp.zeros_like(ref)

    ti = lax.broadcasted_iota(I32, (c, c), 0)
    si = lax.broadcasted_iota(I32, (c, c), 1)
    lane = lax.broadcasted_iota(I32, (1, w), 1)
    m_lo = lane < hd
    ti2 = lax.broadcasted_iota(I32, (c, 2 * c), 0)
    si2 = lax.broadcasted_iota(I32, (c, 2 * c), 1)
    si2 = jnp.where(si2 >= c, si2 - c, si2)
    eye2 = jnp.where(si2 == ti2, 1.0, 0.0)
    dir_masks = (
        (jnp.where(si <= ti, 1.0, 0.0).astype(BF16), si2 <= ti2, si2 < ti2),
        (jnp.where(si >= ti, 1.0, 0.0).astype(BF16), si2 >= ti2, si2 > ti2),
    )
    rr = lax.broadcasted_iota(I32, (2 * c, 2 * c), 0)
    cc = lax.broadcasted_iota(I32, (2 * c, 2 * c), 1)
    bd_mask_c = jnp.where(rr < c, 0, 1) == jnp.where(cc < c, 0, 1)
    rr = lax.broadcasted_iota(I32, (w, w), 0)
    cc = lax.broadcasted_iota(I32, (w, w), 1)
    bd_mask_h = jnp.where(rr < hd, 0, 1) == jnp.where(cc < hd, 0, 1)

    def row_stack(x):
        return jnp.concatenate([jnp.where(m_lo, x, 0.0), jnp.where(m_lo, 0.0, x)],
                               axis=0).astype(BF16)

    def block_diag(xp):
        return jnp.where(bd_mask_c, jnp.concatenate([xp, xp], axis=0), 0.0).astype(BF16)

    fwd = (0, rf_ref, vf_ref, knf_ref, lwf_ref, kdf_ref, bdf_ref)
    rev = (1, rr_ref, vr_ref, knr_ref, lwr_ref, kdr_ref, bdr_ref)
    y_refs = (yf_ref, yr_ref)
    q = [qf_s[...], qr_s[...]]

    def stages(probs, st):
        for dirn, r_ref, v_ref, kn_ref, lw_ref, kd_ref, bd_ref, ci in probs:
            tri_incl, _, _ = dir_masks[dirn]
            sl = pl.ds(ci * c, c)
            lw = lw_ref[0, 0, sl, :]
            lw_hi = lw.astype(BF16)
            lw_lo = (lw - lw_hi.astype(F32)).astype(BF16)
            l_incl = _dot(tri_incl, lw_hi) + _dot(tri_incl, lw_lo)
            st.append(dict(dirn=dirn, sl=sl, lw=lw, l_incl=l_incl,
                           r=r_ref[0, sl, :].astype(F32), v=v_ref[0, sl, :].astype(F32),
                           kn=kn_ref[0, sl, :].astype(F32), kd=kd_ref[0, 0, sl, :].astype(F32),
                           bd=bd_ref[0, 0, sl, :].astype(F32)))
        yield
        for p in st:
            stage_decay(p)
        yield
        for p in st:
            stage_masks(p)
        yield
        for _ in range(max(c.bit_length() - 3, 0)):
            for p in st:
                res = _dot(jnp.concatenate([p["pw"], p["t_p"]], axis=0).astype(BF16),
                           block_diag(p["pw"]))
                p["pw"] = res[0:c]
                p["t_p"] = p["t_p"] + res[c:2 * c]
            yield
        for p in st:
            stage_solve(p)
        yield
        for p in st:
            stage_maps(p)
        yield

    def handover(off_f, off_r):
        for i in range(ng):
            for dirn, slot, row in ((0, i, off_f + i * c), (1, ng + i, off_r + (ng - 1 - i) * c)):
                q_b = q[dirn].astype(BF16)
                y_refs[dirn][0, pl.ds(pl.multiple_of(row, c), c), :] = (
                    _dot_nt(rh_s[slot], q_b) + y0_s[slot])
                q[dirn] = q[dirn] * gt_s[slot] + _dot(q_b, nm_s[slot]) + q0_s[slot]
            yield

    def stage_decay(p):
        l_incl = p["l_incl"]
        l_tot = l_incl[0:1, :] if p["dirn"] else l_incl[c - 1:c, :]
        g_inv = jnp.exp(-l_incl)
        g_end = jnp.exp(l_tot - l_incl)
        p["g_tot"] = jnp.exp(l_tot)
        p["rt"] = p["r"] * jnp.exp(l_incl)
        p["at"] = -p["kn"] * jnp.exp(l_incl - p["lw"])
        p["v_rs"] = row_stack(p["v"])
        p["bh"] = (p["bd"] * g_end).astype(BF16)
        p["kh"] = (p["kd"] * g_end).astype(BF16)
        lhs = jnp.concatenate([p["at"], p["rt"]], axis=0).astype(BF16)
        rhs = jnp.concatenate([row_stack(p["bd"] * g_inv), row_stack(p["kd"] * g_inv)],
                              axis=0)
        p["gm"] = _dot_nt(lhs, rhs)

    def stage_masks(p):
        _, incl2, strict2 = dir_masks[p["dirn"]]
        gm = p.pop("gm")
        a_ab = jnp.where(strict2, gm[0:c, 0:2 * c], 0.0)
        a_ak = jnp.where(strict2, gm[0:c, 2 * c:4 * c], 0.0)
        a_rb = jnp.where(incl2, gm[c:2 * c, 0:2 * c], 0.0)
        a_rk = jnp.where(incl2, gm[c:2 * c, 2 * c:4 * c], 0.0)
        p["a_rb"] = a_rb.astype(BF16)
        av = _dot(jnp.concatenate([a_ak, a_rk], axis=0).astype(BF16), p["v_rs"])
        p["akv"] = av[0:c]
        p["arkv"] = av[c:2 * c]
        p["t_p"] = eye2 + a_ab
        p["pw"] = _dot(a_ab.astype(BF16), block_diag(a_ab))

    def stage_solve(p):
        t_p = p["t_p"] + _dot(p["t_p"].astype(BF16), block_diag(p["pw"]))
        wu = _dot(t_p.astype(BF16),
                  jnp.concatenate([row_stack(p["at"]), row_stack(p["akv"])], axis=1))
        p["w_m"] = wu[:, 0:w]
        p["u_t"] = wu[:, w:2 * w]

    def stage_maps(p):
        ry = _dot(p["a_rb"],
                  jnp.concatenate([row_stack(p["w_m"]), row_stack(p["u_t"])], axis=1))
        p["r_hat"] = (p["rt"] + ry[:, 0:w]).astype(BF16)
        p["y0"] = ry[:, w:2 * w] + p["arkv"]
        p["n_m"] = jnp.where(bd_mask_h, _dot_tn(p["w_m"].astype(BF16), p["bh"]),
                             0.0).astype(BF16)
        p["q0"] = jnp.where(
            bd_mask_h,
            _dot_tn(jnp.concatenate([p["u_t"], p["v"]], axis=0).astype(BF16),
                    jnp.concatenate([p["bh"], p["kh"]], axis=0)), 0.0)

    cg = ng * c
    probs = [fwd + (ci,) for ci in range(ng)] + [rev + (ci,) for ci in range(ng - 1, -1, -1)]
    st = []
    pending = handover(jnp.maximum(j - 1, 0) * cg, jnp.minimum(n_steps - j, n_steps - 1) * cg)
    for _ in stages(probs, st):
        next(pending, None)
    for _ in pending:
        pass
    for slot, p in enumerate(st):
        rh_s[slot] = p["r_hat"]
        y0_s[slot] = p["y0"]
        nm_s[slot] = p["n_m"]
        q0_s[slot] = p["q0"]
        gt_s[slot] = p["g_tot"]
    qf_s[...] = q[0]
    qr_s[...] = q[1]

    @pl.when(j == n_steps - 1)
    def _():
        q[0] = qf_s[...]
        q[1] = qr_s[...]
        for _ in handover(j * cg, 0):
            pass


def _wkv(r, v, kn, lw, kd, bd):
    bsz, s, d = r.shape
    c = min(CHUNK, s)
    ng = min(CHUNK_GROUP, s // c)
    cg = c * ng
    n_steps = s // cg
    w = 2 * RWKV_HEAD_DIM
    tok_f = pl.BlockSpec((1, cg, w), lambda b, p, j: (b, j, p))
    tok_r = pl.BlockSpec((1, cg, w), lambda b, p, j: (b, n_steps - 1 - j, p))
    dir_f = pl.BlockSpec((1, 1, cg, w), lambda b, p, j: (0, b, j, p))
    dir_r = pl.BlockSpec((1, 1, cg, w), lambda b, p, j: (1, b, n_steps - 1 - j, p))
    kern = functools.partial(_wkv_kernel, c=c, ng=ng)
    seq = pl.BlockSpec((1, s, w), lambda b, p, j: (b, 0, p))
    return pl.pallas_call(
        kern,
        grid=(bsz, d // w, n_steps),
        in_specs=[tok_f, tok_f, tok_f, tok_r, tok_r, tok_r,
                  dir_f, dir_f, dir_f, dir_r, dir_r, dir_r],
        out_specs=[seq, seq],
        out_shape=[jax.ShapeDtypeStruct((bsz, s, d), F32)] * 2,
        scratch_shapes=[pltpu.VMEM((w, w), F32), pltpu.VMEM((w, w), F32),
                        pltpu.VMEM((2 * ng, c, w), BF16), pltpu.VMEM((2 * ng, c, w), F32),
                        pltpu.VMEM((2 * ng, w, w), BF16), pltpu.VMEM((2 * ng, w, w), F32),
                        pltpu.VMEM((2 * ng, 1, w), F32)],
        compiler_params=_params("parallel", "parallel", "arbitrary"),
        name="wkv",
    )(r, v, kn, r, v, kn, lw, kd, bd, lw, kd, bd)


def _rwkv_post_kernel(h_ref, yf_ref, yr_ref, bonus_ref, gate_ref, lnw_ref, lnb_ref, hs_ref,
                      wout_ref, o_ref, *, n):
    hsum = hs_ref[...]
    y = yf_ref[0] + yr_ref[0]
    mean = _head_sum(y, hsum, two_term=True) * (1.0 / n)
    yc = y - mean
    var = _head_sum(yc * yc, hsum, two_term=False) * (1.0 / n)
    yn = yc * lax.rsqrt(var + GN_EPS) * lnw_ref[...] + lnb_ref[...]
    yn = yn + bonus_ref[0].astype(F32)
    z = (yn * gate_ref[0].astype(F32)).astype(BF16)
    o_ref[0] = h_ref[0] + _dot(z, wout_ref[...])


def _rwkv_post(h, yf, yr, bonus, gate, ln_w, ln_b, w_out):
    bsz, s, d = h.shape
    tm = min(ROW_TILE, s)
    tok = pl.BlockSpec((1, tm, d), lambda b, i: (b, i, 0))
    kern = functools.partial(_rwkv_post_kernel, n=RWKV_HEAD_DIM)
    return pl.pallas_call(
        kern,
        grid=(bsz, s // tm),
        in_specs=[tok, tok, tok, tok, tok, _full((1, d)), _full((1, d)), _full((d, LANES)),
                  _full((d, d))],
        out_specs=tok,
        out_shape=jax.ShapeDtypeStruct((bsz, s, d), F32),
        compiler_params=_params("parallel", "parallel"),
        name="rwkv_post",
    )(h, yf, yr, bonus, gate, ln_w.reshape(1, d), ln_b.reshape(1, d),
      _head_sum_matrix(d, RWKV_HEAD_DIM), w_out.astype(BF16))


def kernel(x, mem, norm_mix, norm_xattn, norm_mem, norm_ffn, norm_final,
           conv_w_in, conv_w, conv_w_out,
           rwkv_mu, rwkv_w_rkv, rwkv_w0, rwkv_w1, rwkv_w2, rwkv_a0, rwkv_a1, rwkv_a2,
           rwkv_g1, rwkv_g2, rwkv_k_k, rwkv_k_a, rwkv_r_k, rwkv_ln_w, rwkv_ln_b, rwkv_w_out,
           xattn_w_q, xattn_w_kv, xattn_w_o,
           moe_router, moe_w_gate, moe_w_up, moe_w_down):
    depth = norm_mix.shape[0]
    bsz, s, d = x.shape
    n_exp = moe_router.shape[-1]
    cap = CAPACITY_FACTOR * s // n_exp
    h = x
    for i in range(depth):
        j = i // N_MIXERS
        if i % N_MIXERS == 0:
            h = _conv_layer(h, norm_mix[i], conv_w_in[j], conv_w[j], conv_w_out[j])
        else:
            r, v, kn, gate, bonus, lw, kd, bd = _rwkv_prep(
                h, norm_mix[i], rwkv_mu[j], rwkv_w_rkv[j], rwkv_w0[j], rwkv_w1[j], rwkv_w2[j],
                rwkv_a0[j], rwkv_a1[j], rwkv_a2[j], rwkv_g1[j], rwkv_g2[j],
                rwkv_k_k[j], rwkv_k_a[j], rwkv_r_k[j])
            yf, yr = _wkv(r, v, kn, lw, kd, bd)
            h = _rwkv_post(h, yf, yr, bonus, gate, rwkv_ln_w[j], rwkv_ln_b[j], rwkv_w_out[j])
        kv = _kv_proj(mem, norm_mem[i], xattn_w_kv[i])
        h, xn, probs_t = _xattn_layer(h, kv, norm_xattn[i], xattn_w_q[i], xattn_w_o[i],
                                      norm_ffn[i], moe_router[i])
        code, starts = _select(probs_t, cap)
        xe, aff = _expert_gather(xn, code, probs_t, starts, cap)
        ye = _expert_ffn(xe, aff, moe_w_gate, moe_w_up, moe_w_down, i)
        h = _moe_combine(h, jnp.swapaxes(code, 1, 2), starts, ye, norm_final,
                         final_norm=(i == depth - 1))
    return h
```

```python
import functools

import jax
import jax.numpy as jnp
from jax import lax
from jax.experimental import pallas as pl
from jax.experimental.pallas import tpu as pltpu

F32 = jnp.float32
BF16 = jnp.bfloat16
I32 = jnp.int32

N_MIXERS = 2
RWKV_HEAD_DIM = 64
XATTN_HEADS = 4
CAPACITY_FACTOR = 2
GN_EPS = 64e-5
RMS_EPS = 1e-6
DECAY_SCALE = 0.6065306597126334

V7X_VMEM_LIMIT_BYTES = 56 * 1024 * 1024
LANES = 128
BF16_SUBLANES = 16

ROW_TILE = 512
RWKV_ROW_TILE = 512
HALO = BF16_SUBLANES
POST_ROW_GROUPS = 2
CHUNK = 64
CHUNK_GROUP = 8
TOKEN_BLOCK = 512
PICK_WINDOW = 128
GATHER_BLOCK = 512
GATHER_WINDOW = 128


def _params(*semantics):
    return pltpu.CompilerParams(dimension_semantics=semantics,
                                vmem_limit_bytes=V7X_VMEM_LIMIT_BYTES)


def _rms(x, g):
    return x * lax.rsqrt(jnp.mean(x * x, axis=-1, keepdims=True) + RMS_EPS) * g


def _dot(a, b):
    return jnp.dot(a, b, preferred_element_type=F32)


def _dot_nt(a, b):
    return lax.dot_general(a, b, (((1,), (1,)), ((), ())), preferred_element_type=F32)


def _dot_tn(a, b):
    return lax.dot_general(a, b, (((0,), (0,)), ((), ())), preferred_element_type=F32)


def _full(shape):
    n = len(shape)
    return pl.BlockSpec(shape, lambda *_: (0,) * n)


def _conv_kernel(h_ref, hp_ref, hn_ref, g_ref, win_ref, cw_ref, wout_ref, o_ref,
                 xn_s, u_s, gate_s, *, tm, d, cb):
    i = pl.program_id(1)
    g = g_ref[...]
    x = h_ref[0]
    xn_s[0:HALO, :] = _rms(hp_ref[0], g).astype(BF16)
    xn_s[HALO:HALO + tm, :] = _rms(x, g).astype(BF16)
    xn_s[HALO + tm:, :] = _rms(hn_ref[0], g).astype(BF16)
    rows = tm + 2 * HALO
    row = lax.broadcasted_iota(I32, (rows, 1), 0)
    lo = jnp.where(i == 0, HALO, 0)
    hi = jnp.where(i == pl.num_programs(1) - 1, HALO + tm, rows)
    pad = jnp.logical_or(row < lo, row >= hi)
    xa = xn_s[...]
    for c0 in range(0, d, cb):
        c_gate = _dot(xa, win_ref[:, d + c0:d + c0 + cb])
        hx = _dot(xa, win_ref[:, 2 * d + c0:2 * d + c0 + cb])
        u_s[...] = jnp.where(pad, 0.0, c_gate * hx)
        conv = (u_s[pl.ds(HALO - 1, tm), :] * cw_ref[0:1, c0:c0 + cb]
                + u_s[pl.ds(HALO, tm), :] * cw_ref[1:2, c0:c0 + cb]
                + u_s[pl.ds(HALO + 1, tm), :] * cw_ref[2:3, c0:c0 + cb])
        b_gate = _dot(xn_s[HALO:HALO + tm, :], win_ref[:, c0:c0 + cb])
        gate_s[:, c0:c0 + cb] = (b_gate * conv).astype(BF16)
    o_ref[0] = x + _dot(gate_s[...], wout_ref[...])


def _halo_specs(tm, s, d):
    nb = tm // HALO
    last = s // HALO - 1
    return [
        pl.BlockSpec((1, tm, d), lambda b, i: (b, i, 0)),
        pl.BlockSpec((1, HALO, d), lambda b, i: (b, jnp.maximum(i * nb - 1, 0), 0)),
        pl.BlockSpec((1, HALO, d), lambda b, i: (b, jnp.minimum((i + 1) * nb, last), 0)),
    ]


def _conv_layer(h, g, w_in, conv_w, w_out):
    bsz, s, d = h.shape
    tm = min(ROW_TILE, s)
    cb = 512
    kern = functools.partial(_conv_kernel, tm=tm, d=d, cb=cb)
    return pl.pallas_call(
        kern,
        grid=(bsz, s // tm),
        in_specs=_halo_specs(tm, s, d) + [
            _full((1, d)), _full((d, 3 * d)), _full((3, d)), _full((d, d))],
        out_specs=pl.BlockSpec((1, tm, d), lambda b, i: (b, i, 0)),
        out_shape=jax.ShapeDtypeStruct((bsz, s, d), F32),
        scratch_shapes=[pltpu.VMEM((tm + 2 * HALO, d), BF16),
                        pltpu.VMEM((tm + 2 * HALO, cb), F32),
                        pltpu.VMEM((tm, d), BF16)],
        compiler_params=_params("parallel", "parallel"),
        name="conv_mixer",
    )(h, h, h, g.reshape(1, d), w_in.astype(BF16), conv_w, w_out.astype(BF16))


def _kv_kernel(m_ref, g_ref, w_ref, o_ref):
    xn = _rms(m_ref[0], g_ref[...]).astype(BF16)
    o_ref[0] = _dot(xn, w_ref[...]).astype(BF16)


def _kv_proj(mem, g, w_kv):
    bsz, m, d = mem.shape
    return pl.pallas_call(
        _kv_kernel,
        grid=(bsz,),
        in_specs=[pl.BlockSpec((1, m, d), lambda b: (b, 0, 0)), _full((1, d)), _full((d, 2 * d))],
        out_specs=pl.BlockSpec((1, m, 2 * d), lambda b: (b, 0, 0)),
        out_shape=jax.ShapeDtypeStruct((bsz, m, 2 * d), BF16),
        compiler_params=_params("parallel"),
        name="kv_proj",
    )(mem, g.reshape(1, d), w_kv.astype(BF16))


def _xattn_kernel(h_ref, gx_ref, wq_ref, kv_ref, wo_ref, gf_ref, rhi_ref, rlo_ref,
                  h_out, xn_out, probs_out, o_s, *, d, heads):
    x = h_ref[0]
    xn = _rms(x, gx_ref[...]).astype(BF16)
    hd = d // heads
    q = (_dot(xn, wq_ref[...]) * (hd ** -0.5)).astype(BF16)
    kv = kv_ref[0]
    scs = [_dot_nt(q[:, a * hd:(a + 1) * hd], kv[:, a * hd:(a + 1) * hd]) for a in range(heads)]
    ps = [jnp.exp(sc - jnp.max(sc, axis=-1, keepdims=True)) for sc in scs]
    for a, p in enumerate(ps):
        l = jnp.sum(p, axis=-1, keepdims=True)
        o = _dot(p.astype(BF16), kv[:, d + a * hd:d + (a + 1) * hd]) / l
        o_s[:, a * hd:(a + 1) * hd] = o.astype(BF16)
    hn = x + _dot(o_s[...], wo_ref[...])
    h_out[0] = hn
    xf = _rms(hn, gf_ref[...])
    hi = xf.astype(BF16)
    xn_out[0] = hi
    lo = (xf - hi.astype(F32)).astype(BF16)
    lg = _dot_nt(rhi_ref[...], hi) + _dot_nt(rhi_ref[...], lo) + _dot_nt(rlo_ref[...], hi)
    e = jnp.exp(lg - jnp.max(lg, axis=0, keepdims=True))
    probs_out[0] = e / jnp.sum(e, axis=0, keepdims=True)


def _xattn_layer(h, kv, gx, w_q, w_o, gf, router):
    bsz, s, d = h.shape
    m = kv.shape[1]
    e = router.shape[1]
    tm = min(ROW_TILE, s)
    rt = router.T
    r_hi = rt.astype(BF16)
    r_lo = (rt - r_hi.astype(F32)).astype(BF16)
    kern = functools.partial(_xattn_kernel, d=d, heads=XATTN_HEADS)
    return pl.pallas_call(
        kern,
        grid=(bsz, s // tm),
        in_specs=[pl.BlockSpec((1, tm, d), lambda b, i: (b, i, 0)),
                  _full((1, d)), _full((d, d)),
                  pl.BlockSpec((1, m, 2 * d), lambda b, i: (b, 0, 0)),
                  _full((d, d)), _full((1, d)), _full((e, d)), _full((e, d))],
        out_specs=[pl.BlockSpec((1, tm, d), lambda b, i: (b, i, 0)),
                   pl.BlockSpec((1, tm, d), lambda b, i: (b, i, 0)),
                   pl.BlockSpec((1, e, tm), lambda b, i: (b, 0, i))],
        out_shape=[jax.ShapeDtypeStruct((bsz, s, d), F32),
                   jax.ShapeDtypeStruct((bsz, s, d), BF16),
                   jax.ShapeDtypeStruct((bsz, e, s), F32)],
        scratch_shapes=[pltpu.VMEM((tm, d), BF16)],
        compiler_params=_params("parallel", "parallel"),
        name="xattn_router",
    )(h, gx.reshape(1, d), w_q.astype(BF16), kv, w_o.astype(BF16), gf.reshape(1, d), r_hi, r_lo)


def _select_kernel(p_ref, code_ref, start_ref, *, cap, blk):
    p = p_ref[0]
    e, s = p.shape
    bits = pltpu.bitcast(p, I32)

    def count(mask):
        return jnp.sum(mask.astype(F32), axis=1, keepdims=True)

    def search(k, t):
        cand = jnp.bitwise_or(t, jnp.left_shift(jnp.int32(1), 30 - k))
        return jnp.where(count(bits >= cand) >= cap, cand, t)

    thr = lax.fori_loop(0, 31, search, jnp.zeros((e, 1), I32))
    gt = bits > thr
    eq = bits == thr
    need = cap - count(gt)

    r = lax.broadcasted_iota(I32, (blk, blk), 0)
    c = lax.broadcasted_iota(I32, (blk, blk), 1)
    tri = jnp.where(r < c, 1.0, 0.0).astype(BF16)

    def prefix(mask_f):
        out, carries = [], []
        carry = jnp.zeros((e, 1), F32)
        for j in range(0, s, blk):
            mb = mask_f[:, j:j + blk]
            carries.append(carry)
            out.append(_dot(mb.astype(BF16), tri) + carry)
            carry = carry + jnp.sum(mb, axis=1, keepdims=True)
        return out, carries + [carry]

    eq_f = eq.astype(F32)
    tie_rank, _ = prefix(eq_f)
    for j0, tr in zip(range(0, s, blk), tie_rank):
        sel_b = jnp.logical_or(gt[:, j0:j0 + blk],
                               jnp.logical_and(eq[:, j0:j0 + blk], tr < need))
        code_ref[0, :, j0:j0 + blk] = sel_b.astype(I32)
    sel_f = code_ref[0].astype(F32)
    rank, starts = prefix(sel_f)
    for j0, rk in zip(range(0, s, blk), rank):
        code_ref[0, :, j0:j0 + blk] = jnp.where(sel_f[:, j0:j0 + blk] > 0.0, rk.astype(I32), -1)
    start_ref[...] = jnp.zeros_like(start_ref)
    for j, st in enumerate(starts):
        start_ref[0, :, j:j + 1] = st.astype(I32)


def _select(probs_t, cap):
    bsz, e, s = probs_t.shape
    blk = min(GATHER_BLOCK, s)
    nb1 = s // blk + 1
    assert nb1 <= LANES
    kern = functools.partial(_select_kernel, cap=cap, blk=blk)
    code, starts = pl.pallas_call(
        kern,
        grid=(bsz,),
        in_specs=[pl.BlockSpec((1, e, s), lambda b: (b, 0, 0))],
        out_specs=[pl.BlockSpec((1, e, s), lambda b: (b, 0, 0)),
                   pl.BlockSpec((1, e, LANES), lambda b: (b, 0, 0))],
        out_shape=[jax.ShapeDtypeStruct((bsz, e, s), I32),
                   jax.ShapeDtypeStruct((bsz, e, LANES), I32)],
        compiler_params=_params("parallel"),
        name="expert_choice_select",
    )(probs_t)
    return code, starts[:, :, :nb1].reshape(-1)


def _window(s_lo, s_hi, win):
    s0 = jnp.bitwise_and(s_lo, -BF16_SUBLANES)
    n_win = jnp.where(s_hi > s_lo, lax.shift_right_logical(s_hi - s0 + (win - 1),
                                                           win.bit_length() - 1), 0)
    return s0, n_win


def _gather_kernel(st_ref, x_ref, code_ref, p_ref, xe_ref, aff_ref, xe_s, aff_s, *, cap, blk, win):
    b = pl.program_id(0)
    k = pl.program_id(1)
    s = x_ref.shape[1]
    nb = s // blk

    base = (b * pl.num_programs(1) + k) * (nb + 1)
    iota = lax.broadcasted_iota(I32, (win, blk), 0)

    def gather(j, row0):
        onehot = jnp.where(iota + row0 == code_ref[0, :, j * blk:(j + 1) * blk], 1.0, 0.0)
        part = _dot(onehot.astype(BF16), x_ref[0, j * blk:(j + 1) * blk, :])
        xe_s[pl.ds(row0, win), :] += part
        aff_s[pl.ds(row0, win), :] += jnp.sum(onehot * p_ref[0, :, j * blk:(j + 1) * blk],
                                              axis=1, keepdims=True)

    xe_s[...] = jnp.zeros_like(xe_s)
    aff_s[...] = jnp.zeros_like(aff_s)
    wins = [_window(st_ref[base + j], st_ref[base + j + 1], win) for j in range(nb)]
    for j, (s0, _) in enumerate(wins):
        gather(j, pl.multiple_of(s0, BF16_SUBLANES))
    for j, (s0, n_win) in enumerate(wins):
        def more(i, carry, j=j, s0=s0):
            gather(j, pl.multiple_of(s0 + i * win, BF16_SUBLANES))
            return carry
        lax.fori_loop(1, n_win, more, 0)

    xe_ref[0, 0] = xe_s[0:cap, :].astype(BF16)
    aff_ref[0, 0] = jnp.broadcast_to(aff_s[0:cap, :], (cap, LANES))


def _expert_gather(xn, code, probs_t, starts, cap):
    bsz, s, d = xn.shape
    e = code.shape[1]
    blk = min(GATHER_BLOCK, s)
    win = min(GATHER_WINDOW, cap)
    kern = functools.partial(_gather_kernel, cap=cap, blk=blk, win=win)
    row = pl.BlockSpec((1, 1, s), lambda b, k, st: (b * e + k, 0, 0))
    return pl.pallas_call(
        kern,
        grid_spec=pltpu.PrefetchScalarGridSpec(
            num_scalar_prefetch=1,
            grid=(bsz, e),
            in_specs=[pl.BlockSpec((1, s, d), lambda b, k, st: (b, 0, 0)), row, row],
            out_specs=[pl.BlockSpec((1, 1, cap, d), lambda b, k, st: (b, k, 0, 0)),
                       pl.BlockSpec((1, 1, cap, LANES), lambda b, k, st: (b, k, 0, 0))],
            scratch_shapes=[pltpu.VMEM((cap + win, d), F32), pltpu.VMEM((cap + win, 1), F32)]),
        out_shape=[jax.ShapeDtypeStruct((bsz, e, cap, d), BF16),
                   jax.ShapeDtypeStruct((bsz, e, cap, LANES), F32)],
        compiler_params=_params("parallel", "parallel"),
        name="expert_gather",
    )(starts, xn, code.reshape(bsz * e, 1, s), probs_t.reshape(bsz * e, 1, s))


def _ffn_kernel(xe_ref, aff_ref, wg_ref, wu_ref, wd_ref, y_ref, wg_s, wu_s, wd_s):
    @pl.when(pl.program_id(1) == 0)
    def _():
        wg_s[...] = wg_ref[0, 0].astype(BF16)
        wu_s[...] = wu_ref[0, 0].astype(BF16)
        wd_s[...] = wd_ref[0, 0].astype(BF16)

    xe = xe_ref[0, 0]
    fh = wg_s.shape[1] // 2
    halves = [(_dot(xe, wg_s[:, c0:c0 + fh]), _dot(xe, wu_s[:, c0:c0 + fh])) for c0 in (0, fh)]
    y = None
    for i, (gt, up) in enumerate(halves):
        hid = (gt * jax.nn.sigmoid(gt) * up).astype(BF16)
        part = _dot(hid, wd_s[i * fh:(i + 1) * fh, :])
        y = part if y is None else y + part
    aff = aff_ref[0, 0]
    y_ref[0, 0] = (y * jnp.concatenate([aff] * (y.shape[1] // LANES), axis=1)).astype(BF16)


def _expert_ffn(xe, aff, w_gate, w_up, w_down, layer):
    bsz, e, cap, d = xe.shape
    f = w_gate.shape[-1]
    return pl.pallas_call(
        _ffn_kernel,
        grid=(e, bsz),
        in_specs=[pl.BlockSpec((1, 1, cap, d), lambda k, b: (b, k, 0, 0)),
                  pl.BlockSpec((1, 1, cap, LANES), lambda k, b: (b, k, 0, 0)),
                  pl.BlockSpec((1, 1, d, f), lambda k, b: (layer, k, 0, 0)),
                  pl.BlockSpec((1, 1, d, f), lambda k, b: (layer, k, 0, 0)),
                  pl.BlockSpec((1, 1, f, d), lambda k, b: (layer, k, 0, 0))],
        out_specs=pl.BlockSpec((1, 1, cap, d), lambda k, b: (b, k, 0, 0)),
        out_shape=jax.ShapeDtypeStruct((bsz, e, cap, d), BF16),
        scratch_shapes=[pltpu.VMEM((d, f), BF16), pltpu.VMEM((d, f), BF16),
                        pltpu.VMEM((f, d), BF16)],
        compiler_params=_params("arbitrary", "arbitrary"),
        name="expert_ffn",
    )(xe, aff, w_gate, w_up, w_down)


def _combine_kernel(st_ref, h_ref, code_ref, y_ref, g_ref, o_ref, acc_s,
                    *, cap, win, per, final_norm):
    b = pl.program_id(0)
    j = pl.program_id(1)
    nb = pl.num_programs(1)
    tb = h_ref.shape[1]
    n_exp = y_ref.shape[1]
    iota = lax.broadcasted_iota(I32, (tb, win), 1)
    code = code_ref[0]

    def window(k, lo):
        row0 = pl.multiple_of(jnp.minimum(lo, cap - win), BF16_SUBLANES)
        code_k = jnp.where(code[:, k:k + 1] >= lo, code[:, k:k + 1], -1)
        onehot = jnp.where(iota + row0 == code_k, 1.0, 0.0).astype(BF16)
        return onehot, y_ref[0, k, pl.ds(row0, win), :]

    wins = []
    for k in range(n_exp):
        base = (b * n_exp + k) * (nb * per + 1) + j * per
        wins.append(_window(st_ref[base], st_ref[base + per], win))
    acc = h_ref[0]
    for k in range(0, n_exp, 2):
        oh_a, y_a = window(k, wins[k][0])
        oh_b, y_b = window(k + 1, wins[k + 1][0])
        acc = acc + _dot(jnp.concatenate([oh_a, oh_b], axis=1),
                         jnp.concatenate([y_a, y_b], axis=0))
    acc_s[...] = acc
    for k, (s0, n_win) in enumerate(wins):
        def more(i, carry, k=k, s0=s0):
            acc_s[...] += _dot(*window(k, s0 + i * win))
            return carry
        lax.fori_loop(1, n_win, more, 0)
    acc = acc_s[...]
    o_ref[0] = _rms(acc, g_ref[...]) if final_norm else acc


def _moe_combine(h, code_t, starts, ye, g_final, final_norm):
    bsz, s, d = h.shape
    e, cap = ye.shape[1], ye.shape[2]
    assert e % 2 == 0
    tb = min(TOKEN_BLOCK, s)
    win = min(PICK_WINDOW, cap)
    per = tb // min(GATHER_BLOCK, s)
    kern = functools.partial(_combine_kernel, cap=cap, win=win, per=per, final_norm=final_norm)
    return pl.pallas_call(
        kern,
        grid_spec=pltpu.PrefetchScalarGridSpec(
            num_scalar_prefetch=1,
            grid=(bsz, s // tb),
            in_specs=[pl.BlockSpec((1, tb, d), lambda b, j, st: (b, j, 0)),
                      pl.BlockSpec((1, tb, e), lambda b, j, st: (b, j, 0)),
                      pl.BlockSpec((1, e, cap, d), lambda b, j, st: (b, 0, 0, 0)),
                      pl.BlockSpec((1, d), lambda b, j, st: (0, 0))],
            out_specs=pl.BlockSpec((1, tb, d), lambda b, j, st: (b, j, 0)),
            scratch_shapes=[pltpu.VMEM((tb, d), F32)]),
        out_shape=jax.ShapeDtypeStruct((bsz, s, d), F32),
        compiler_params=_params("parallel", "parallel"),
        name="moe_combine",
    )(starts, h, code_t, ye, g_final.reshape(1, d))


def _rwkv_prep_kernel(h_ref, hp_ref, hn_ref, g_ref, mu_ref, wrkv_ref, w1_ref, a1_ref, g1_ref,
                      w2_ref, a2_ref, g2_ref, w0_ref, a0_ref, kk_ref, ka_ref, rk_ref, hs_ref,
                      r_out, v_out, kn_out, gate_out, bonus_out, lw_out, kd_out, bd_out,
                      xn_s, *, tm):
    i = pl.program_id(1)
    g = g_ref[...]
    keep_prev = jnp.where(i == 0, 0.0, 1.0)
    keep_next = jnp.where(i == pl.num_programs(1) - 1, 0.0, 1.0)
    xn_s[0:HALO, :] = _rms(hp_ref[0], g) * keep_prev
    xn_s[HALO:HALO + tm, :] = _rms(h_ref[0], g)
    xn_s[HALO + tm:, :] = _rms(hn_ref[0], g) * keep_next
    xn = xn_s[pl.ds(HALO, tm), :]
    xx = 0.5 * (xn_s[pl.ds(HALO - 1, tm), :] + xn_s[pl.ds(HALO + 1, tm), :]) - xn

    def mix(j):
        return (xn + xx * mu_ref[j:j + 1, :]).astype(BF16)

    r = _dot(mix(0), wrkv_ref[0])
    k = _dot(mix(1), wrkv_ref[1])
    v = _dot(mix(2), wrkv_ref[2])
    hw = jnp.tanh(_dot(mix(3), w1_ref[...])).astype(BF16)
    ha = _dot(mix(4), a1_ref[...]).astype(BF16)
    hg = jax.nn.sigmoid(_dot(mix(5), g1_ref[...])).astype(BF16)
    gate_out[0] = _dot(hg, g2_ref[...]).astype(BF16)

    hsum = hs_ref[...]
    kk = k * kk_ref[...]
    nrm2 = _head_sum(kk * kk, hsum, two_term=False)
    kn = kk / jnp.maximum(jnp.sqrt(nrm2), 1e-12)
    r_out[0] = r.astype(BF16)
    v_out[0] = v.astype(BF16)
    kn_out[0] = kn.astype(BF16)
    ksum = None
    for n in range(2):
        w_raw = w0_ref[n:n + 1, :] + _dot(hw, w2_ref[n])
        lw_out[n, 0] = -DECAY_SCALE * jax.nn.sigmoid(w_raw)
        a = jax.nn.sigmoid(a0_ref[n:n + 1, :] + _dot(ha, a2_ref[n]))
        kd = k * (1.0 + (a - 1.0) * ka_ref[...])
        kd_out[n, 0] = kd.astype(BF16)
        bd_out[n, 0] = (kn * a).astype(BF16)
        ksum = kd if ksum is None else ksum + kd
    coef = _head_sum(r * ksum * rk_ref[...], hsum, two_term=False)
    bonus_out[0] = (coef * v).astype(BF16)


def _head_sum_matrix(d, n):
    assert d // n <= LANES
    return (jnp.arange(d)[:, None] // n == jnp.arange(LANES)[None, :]).astype(BF16)


def _head_collect(x, hs, two_term):
    x_hi = x.astype(BF16)
    sums = _dot(x_hi, hs)
    if two_term:
        sums = sums + _dot((x - x_hi.astype(F32)).astype(BF16), hs)
    return sums


def _head_spread(sums, hs, two_term):
    s_hi = sums.astype(BF16)
    out = _dot_nt(s_hi, hs)
    if two_term:
        out = out + _dot_nt((sums - s_hi.astype(F32)).astype(BF16), hs)
    return out


def _head_sum(x, hs, two_term):
    return _head_spread(_head_collect(x, hs, two_term), hs, two_term)


def _rwkv_prep(h, g, mu, w_rkv, w0, w1, w2, a0, a1, a2, g1, g2, k_k, k_a, r_k):
    bsz, s, d = h.shape
    tm = min(RWKV_ROW_TILE, s)
    lora = w1.shape[-1]
    glora = g1.shape[-1]
    w1c = jnp.concatenate([w1[0], w1[1]], axis=1).astype(BF16)
    a1c = jnp.concatenate([a1[0], a1[1]], axis=1).astype(BF16)
    keep = (jnp.arange(2 * lora)[None, :, None] // lora) == jnp.arange(2)[:, None, None]
    w2p = jnp.where(keep, jnp.concatenate([w2, w2], axis=1), 0.0).astype(BF16)
    a2p = jnp.where(keep, jnp.concatenate([a2, a2], axis=1), 0.0).astype(BF16)
    kern = functools.partial(_rwkv_prep_kernel, tm=tm)
    tok = pl.BlockSpec((1, tm, d), lambda b, i: (b, i, 0))
    tok2 = pl.BlockSpec((2, 1, tm, d), lambda b, i: (0, b, i, 0))
    sd = jax.ShapeDtypeStruct
    return pl.pallas_call(
        kern,
        grid=(bsz, s // tm),
        in_specs=_halo_specs(tm, s, d) + [
            _full((1, d)), _full((6, d)), _full((3, d, d)),
            _full((d, 2 * lora)), _full((d, 2 * lora)), _full((d, glora)),
            _full((2, 2 * lora, d)), _full((2, 2 * lora, d)), _full((glora, d)),
            _full((2, d)), _full((2, d)), _full((1, d)), _full((1, d)), _full((1, d)),
            _full((d, LANES))],
        out_specs=[tok, tok, tok, tok, tok, tok2, tok2, tok2],
        out_shape=[sd((bsz, s, d), BF16)] * 5 + [sd((2, bsz, s, d), F32),
                                                 sd((2, bsz, s, d), BF16),
                                                 sd((2, bsz, s, d), BF16)],
        scratch_shapes=[pltpu.VMEM((tm + 2 * HALO, d), F32)],
        compiler_params=_params("parallel", "parallel"),
        name="rwkv_prep",
    )(h, h, h, g.reshape(1, d), mu, w_rkv.astype(BF16), w1c, a1c, g1.astype(BF16),
      w2p, a2p, g2.astype(BF16), w0, a0,
      k_k.reshape(1, d), k_a.reshape(1, d), r_k.reshape(1, d),
      _head_sum_matrix(d, RWKV_HEAD_DIM))


def _wkv_kernel(rf_ref, vf_ref, knf_ref, rr_ref, vr_ref, knr_ref,
                lwf_ref, kdf_ref, bdf_ref, lwr_ref, kdr_ref, bdr_ref,
                yf_ref, yr_ref, qf_s, qr_s, rh_s, y0_s, nm_s, q0_s, gt_s, *, c, ng):
    hd = RWKV_HEAD_DIM
    w = 2 * hd
    j = pl.program_id(2)
    n_steps = pl.num_programs(2)
    carried = (rh_s, y0_s, nm_s, q0_s, gt_s)

    @pl.when(j == 0)
    def _():
        for ref in (qf_s, qr_s) + carried:
            ref[...] = jnp.zeros_like(ref)

    ti = lax.broadcasted_iota(I32, (c, c), 0)
    si = lax.broadcasted_iota(I32, (c, c), 1)
    lane = lax.broadcasted_iota(I32, (1, w), 1)
    m_lo = lane < hd
    ti2 = lax.broadcasted_iota(I32, (c, 2 * c), 0)
    si2 = lax.broadcasted_iota(I32, (c, 2 * c), 1)
    si2 = jnp.where(si2 >= c, si2 - c, si2)
    eye2 = jnp.where(si2 == ti2, 1.0, 0.0)
    dir_masks = (
        (jnp.where(si <= ti, 1.0, 0.0).astype(BF16), si2 <= ti2, si2 < ti2),
        (jnp.where(si >= ti, 1.0, 0.0).astype(BF16), si2 >= ti2, si2 > ti2),
    )
    rr = lax.broadcasted_iota(I32, (2 * c, 2 * c), 0)
    cc = lax.broadcasted_iota(I32, (2 * c, 2 * c), 1)
    bd_mask_c = jnp.where(rr < c, 0, 1) == jnp.where(cc < c, 0, 1)
    rr = lax.broadcasted_iota(I32, (w, w), 0)
    cc = lax.broadcasted_iota(I32, (w, w), 1)
    bd_mask_h = jnp.where(rr < hd, 0, 1) == jnp.where(cc < hd, 0, 1)

    def row_stack(x):
        return jnp.concatenate([jnp.where(m_lo, x, 0.0), jnp.where(m_lo, 0.0, x)],
                               axis=0).astype(BF16)

    def block_diag(xp):
        return jnp.where(bd_mask_c, jnp.concatenate([xp, xp], axis=0), 0.0).astype(BF16)

    fwd = (0, rf_ref, vf_ref, knf_ref, lwf_ref, kdf_ref, bdf_ref)
    rev = (1, rr_ref, vr_ref, knr_ref, lwr_ref, kdr_ref, bdr_ref)
    y_refs = (yf_ref, yr_ref)
    q = [qf_s[...], qr_s[...]]

    def stages(probs, st):
        for dirn, r_ref, v_ref, kn_ref, lw_ref, kd_ref, bd_ref, ci in probs:
            tri_incl, _, _ = dir_masks[dirn]
            sl = pl.ds(ci * c, c)
            lw = lw_ref[0, 0, sl, :]
            lw_hi = lw.astype(BF16)
            lw_lo = (lw - lw_hi.astype(F32)).astype(BF16)
            l_incl = _dot(tri_incl, lw_hi) + _dot(tri_incl, lw_lo)
            st.append(dict(dirn=dirn, sl=sl, lw=lw, l_incl=l_incl,
                           r=r_ref[0, sl, :].astype(F32), v=v_ref[0, sl, :].astype(F32),
                           kn=kn_ref[0, sl, :].astype(F32), kd=kd_ref[0, 0, sl, :].astype(F32),
                           bd=bd_ref[0, 0, sl, :].astype(F32)))
        yield
        for p in st:
            stage_decay(p)
        yield
        for p in st:
            stage_masks(p)
        yield
        for _ in range(max(c.bit_length() - 3, 0)):
            for p in st:
                res = _dot(jnp.concatenate([p["pw"], p["t_p"]], axis=0).astype(BF16),
                           block_diag(p["pw"]))
                p["pw"] = res[0:c]
                p["t_p"] = p["t_p"] + res[c:2 * c]
            yield
        for p in st:
            stage_solve(p)
        yield
        for p in st:
            stage_maps(p)
        yield

    def handover(off_f, off_r):
        for i in range(ng):
            for dirn, slot, row in ((0, i, off_f + i * c), (1, ng + i, off_r + (ng - 1 - i) * c)):
                q_b = q[dirn].astype(BF16)
                y_refs[dirn][0, pl.ds(pl.multiple_of(row, c), c), :] = (
                    _dot_nt(rh_s[slot], q_b) + y0_s[slot])
                q[dirn] = q[dirn] * gt_s[slot] + _dot(q_b, nm_s[slot]) + q0_s[slot]
            yield

    def stage_decay(p):
        l_incl = p["l_incl"]
        l_tot = l_incl[0:1, :] if p["dirn"] else l_incl[c - 1:c, :]
        g_inv = jnp.exp(-l_incl)
        g_end = jnp.exp(l_tot - l_incl)
        p["g_tot"] = jnp.exp(l_tot)
        p["rt"] = p["r"] * jnp.exp(l_incl)
        p["at"] = -p["kn"] * jnp.exp(l_incl - p["lw"])
        p["v_rs"] = row_stack(p["v"])
        p["bh"] = (p["bd"] * g_end).astype(BF16)
        p["kh"] = (p["kd"] * g_end).astype(BF16)
        lhs = jnp.concatenate([p["at"], p["rt"]], axis=0).astype(BF16)
        rhs = jnp.concatenate([row_stack(p["bd"] * g_inv), row_stack(p["kd"] * g_inv)],
                              axis=0)
        p["gm"] = _dot_nt(lhs, rhs)

    def stage_masks(p):
        _, incl2, strict2 = dir_masks[p["dirn"]]
        gm = p.pop("gm")
        a_ab = jnp.where(strict2, gm[0:c, 0:2 * c], 0.0)
        a_ak = jnp.where(strict2, gm[0:c, 2 * c:4 * c], 0.0)
        a_rb = jnp.where(incl2, gm[c:2 * c, 0:2 * c], 0.0)
        a_rk = jnp.where(incl2, gm[c:2 * c, 2 * c:4 * c], 0.0)
        p["a_rb"] = a_rb.astype(BF16)
        av = _dot(jnp.concatenate([a_ak, a_rk], axis=0).astype(BF16), p["v_rs"])
        p["akv"] = av[0:c]
        p["arkv"] = av[c:2 * c]
        p["t_p"] = eye2 + a_ab
        p["pw"] = _dot(a_ab.astype(BF16), block_diag(a_ab))

    def stage_solve(p):
        t_p = p["t_p"] + _dot(p["t_p"].astype(BF16), block_diag(p["pw"]))
        wu = _dot(t_p.astype(BF16),
                  jnp.concatenate([row_stack(p["at"]), row_stack(p["akv"])], axis=1))
        p["w_m"] = wu[:, 0:w]
        p["u_t"] = wu[:, w:2 * w]

    def stage_maps(p):
        ry = _dot(p["a_rb"],
                  jnp.concatenate([row_stack(p["w_m"]), row_stack(p["u_t"])], axis=1))
        p["r_hat"] = (p["rt"] + ry[:, 0:w]).astype(BF16)
        p["y0"] = ry[:, w:2 * w] + p["arkv"]
        p["n_m"] = jnp.where(bd_mask_h, _dot_tn(p["w_m"].astype(BF16), p["bh"]),
                             0.0).astype(BF16)
        p["q0"] = jnp.where(
            bd_mask_h,
            _dot_tn(jnp.concatenate([p["u_t"], p["v"]], axis=0).astype(BF16),
                    jnp.concatenate([p["bh"], p["kh"]], axis=0)), 0.0)

    cg = ng * c
    probs = [fwd + (ci,) for ci in range(ng)] + [rev + (ci,) for ci in range(ng - 1, -1, -1)]
    st = []
    pending = handover(jnp.maximum(j - 1, 0) * cg, jnp.minimum(n_steps - j, n_steps - 1) * cg)
    for _ in stages(probs, st):
        next(pending, None)
    for _ in pending:
        pass
    for slot, p in enumerate(st):
        rh_s[slot] = p["r_hat"]
        y0_s[slot] = p["y0"]
        nm_s[slot] = p["n_m"]
        q0_s[slot] = p["q0"]
        gt_s[slot] = p["g_tot"]
    qf_s[...] = q[0]
    qr_s[...] = q[1]

    @pl.when(j == n_steps - 1)
    def _():
        q[0] = qf_s[...]
        q[1] = qr_s[...]
        for _ in handover(j * cg, 0):
            pass


def _wkv(r, v, kn, lw, kd, bd):
    bsz, s, d = r.shape
    c = min(CHUNK, s)
    ng = min(CHUNK_GROUP, s // c)
    cg = c * ng
    n_steps = s // cg
    w = 2 * RWKV_HEAD_DIM
    tok_f = pl.BlockSpec((1, cg, w), lambda b, p, j: (b, j, p))
    tok_r = pl.BlockSpec((1, cg, w), lambda b, p, j: (b, n_steps - 1 - j, p))
    dir_f = pl.BlockSpec((1, 1, cg, w), lambda b, p, j: (0, b, j, p))
    dir_r = pl.BlockSpec((1, 1, cg, w), lambda b, p, j: (1, b, n_steps - 1 - j, p))
    kern = functools.partial(_wkv_kernel, c=c, ng=ng)
    seq = pl.BlockSpec((1, s, w), lambda b, p, j: (b, 0, p))
    return pl.pallas_call(
        kern,
        grid=(bsz, d // w, n_steps),
        in_specs=[tok_f, tok_f, tok_f, tok_r, tok_r, tok_r,
                  dir_f, dir_f, dir_f, dir_r, dir_r, dir_r],
        out_specs=[seq, seq],
        out_shape=[jax.ShapeDtypeStruct((bsz, s, d), F32)] * 2,
        scratch_shapes=[pltpu.VMEM((w, w), F32), pltpu.VMEM((w, w), F32),
                        pltpu.VMEM((2 * ng, c, w), BF16), pltpu.VMEM((2 * ng, c, w), F32),
                        pltpu.VMEM((2 * ng, w, w), BF16), pltpu.VMEM((2 * ng, w, w), F32),
                        pltpu.VMEM((2 * ng, 1, w), F32)],
        compiler_params=_params("parallel", "parallel", "arbitrary"),
        name="wkv",
    )(r, v, kn, r, v, kn, lw, kd, bd, lw, kd, bd)


def _rwkv_post_kernel(h_ref, yf_ref, yr_ref, bonus_ref, gate_ref, lnw_ref, lnb_ref, hs_ref,
                      wout_ref, o_ref, z_s, *, n):
    hsum = hs_ref[...]
    tm = h_ref.shape[1]
    rows = [pl.ds(i * (tm // POST_ROW_GROUPS), tm // POST_ROW_GROUPS)
            for i in range(POST_ROW_GROUPS)]
    ys = [yf_ref[0, r, :] + yr_ref[0, r, :] for r in rows]
    sums = [_head_collect(y, hsum, two_term=True) for y in ys]
    ycs = [y - _head_spread(sm, hsum, two_term=True) * (1.0 / n) for y, sm in zip(ys, sums)]
    sums = [_head_collect(yc * yc, hsum, two_term=False) for yc in ycs]
    for r, yc, sm in zip(rows, ycs, sums):
        var = _head_spread(sm, hsum, two_term=False) * (1.0 / n)
        yn = yc * lax.rsqrt(var + GN_EPS) * lnw_ref[...] + lnb_ref[...]
        yn = yn + bonus_ref[0, r, :].astype(F32)
        z_s[r, :] = (yn * gate_ref[0, r, :].astype(F32)).astype(BF16)
    o_ref[0] = h_ref[0] + _dot(z_s[...], wout_ref[...])


def _rwkv_post(h, yf, yr, bonus, gate, ln_w, ln_b, w_out):
    bsz, s, d = h.shape
    tm = min(ROW_TILE, s)
    tok = pl.BlockSpec((1, tm, d), lambda b, i: (b, i, 0))
    kern = functools.partial(_rwkv_post_kernel, n=RWKV_HEAD_DIM)
    return pl.pallas_call(
        kern,
        grid=(bsz, s // tm),
        in_specs=[tok, tok, tok, tok, tok, _full((1, d)), _full((1, d)), _full((d, LANES)),
                  _full((d, d))],
        out_specs=tok,
        out_shape=jax.ShapeDtypeStruct((bsz, s, d), F32),
        scratch_shapes=[pltpu.VMEM((tm, d), BF16)],
        compiler_params=_params("parallel", "parallel"),
        name="rwkv_post",
    )(h, yf, yr, bonus, gate, ln_w.reshape(1, d), ln_b.reshape(1, d),
      _head_sum_matrix(d, RWKV_HEAD_DIM), w_out.astype(BF16))


def kernel(x, mem, norm_mix, norm_xattn, norm_mem, norm_ffn, norm_final,
           conv_w_in, conv_w, conv_w_out,
           rwkv_mu, rwkv_w_rkv, rwkv_w0, rwkv_w1, rwkv_w2, rwkv_a0, rwkv_a1, rwkv_a2,
           rwkv_g1, rwkv_g2, rwkv_k_k, rwkv_k_a, rwkv_r_k, rwkv_ln_w, rwkv_ln_b, rwkv_w_out,
           xattn_w_q, xattn_w_kv, xattn_w_o,
           moe_router, moe_w_gate, moe_w_up, moe_w_down):
    depth = norm_mix.shape[0]
    bsz, s, d = x.shape
    n_exp = moe_router.shape[-1]
    cap = CAPACITY_FACTOR * s // n_exp
    h = x
    for i in range(depth):
        j = i // N_MIXERS
        if i % N_MIXERS == 0:
            h = _conv_layer(h, norm_mix[i], conv_w_in[j], conv_w[j], conv_w_out[j])
        else:
            r, v, kn, gate, bonus, lw, kd, bd = _rwkv_prep(
                h, norm_mix[i], rwkv_mu[j], rwkv_w_rkv[j], rwkv_w0[j], rwkv_w1[j], rwkv_w2[j],
                rwkv_a0[j], rwkv_a1[j], rwkv_a2[j], rwkv_g1[j], rwkv_g2[j],
                rwkv_k_k[j], rwkv_k_a[j], rwkv_r_k[j])
            yf, yr = _wkv(r, v, kn, lw, kd, bd)
            h = _rwkv_post(h, yf, yr, bonus, gate, rwkv_ln_w[j], rwkv_ln_b[j], rwkv_w_out[j])
        kv = _kv_proj(mem, norm_mem[i], xattn_w_kv[i])
        h, xn, probs_t = _xattn_layer(h, kv, norm_xattn[i], xattn_w_q[i], xattn_w_o[i],
                                      norm_ffn[i], moe_router[i])
        code, starts = _select(probs_t, cap)
        xe, aff = _expert_gather(xn, code, probs_t, starts, cap)
        ye = _expert_ffn(xe, aff, moe_w_gate, moe_w_up, moe_w_down, i)
        h = _moe_combine(h, jnp.swapaxes(code, 1, 2), starts, ye, norm_final,
                         final_norm=(i == depth - 1))
    return h
```

```python
import functools

import jax
import jax.numpy as jnp
from jax import lax
from jax.experimental import pallas as pl
from jax.experimental.pallas import tpu as pltpu

F32 = jnp.float32
BF16 = jnp.bfloat16
I32 = jnp.int32

N_MIXERS = 2
RWKV_HEAD_DIM = 64
XATTN_HEADS = 4
CAPACITY_FACTOR = 2
GN_EPS = 64e-5
RMS_EPS = 1e-6
DECAY_SCALE = 0.6065306597126334

V7X_VMEM_LIMIT_BYTES = 56 * 1024 * 1024
LANES = 128
BF16_SUBLANES = 16

ROW_TILE = 512
RWKV_ROW_TILE = 512
HALO = BF16_SUBLANES
POST_ROW_GROUPS = 2
CHUNK = 64
CHUNK_GROUP = 8
TOKEN_BLOCK = 512
PICK_WINDOW = 128
GATHER_BLOCK = 512
GATHER_WINDOW = 128


def _params(*semantics):
    return pltpu.CompilerParams(dimension_semantics=semantics,
                                vmem_limit_bytes=V7X_VMEM_LIMIT_BYTES)


def _rms(x, g):
    return x * lax.rsqrt(jnp.mean(x * x, axis=-1, keepdims=True) + RMS_EPS) * g


def _dot(a, b):
    return jnp.dot(a, b, preferred_element_type=F32)


def _dot_nt(a, b):
    return lax.dot_general(a, b, (((1,), (1,)), ((), ())), preferred_element_type=F32)


def _dot_tn(a, b):
    return lax.dot_general(a, b, (((0,), (0,)), ((), ())), preferred_element_type=F32)


def _full(shape):
    n = len(shape)
    return pl.BlockSpec(shape, lambda *_: (0,) * n)


def _conv_kernel(h_ref, hp_ref, hn_ref, g_ref, win_ref, cw_ref, wout_ref, o_ref,
                 xn_s, u_s, gate_s, *, tm, d, cb):
    i = pl.program_id(1)
    g = g_ref[...]
    x = h_ref[0]
    xn_s[0:HALO, :] = _rms(hp_ref[0], g).astype(BF16)
    xn_s[HALO:HALO + tm, :] = _rms(x, g).astype(BF16)
    xn_s[HALO + tm:, :] = _rms(hn_ref[0], g).astype(BF16)
    rows = tm + 2 * HALO
    row = lax.broadcasted_iota(I32, (rows, 1), 0)
    lo = jnp.where(i == 0, HALO, 0)
    hi = jnp.where(i == pl.num_programs(1) - 1, HALO + tm, rows)
    pad = jnp.logical_or(row < lo, row >= hi)
    xa = xn_s[...]
    for c0 in range(0, d, cb):
        c_gate = _dot(xa, win_ref[:, d + c0:d + c0 + cb])
        hx = _dot(xa, win_ref[:, 2 * d + c0:2 * d + c0 + cb])
        u_s[...] = jnp.where(pad, 0.0, c_gate * hx)
        conv = (u_s[pl.ds(HALO - 1, tm), :] * cw_ref[0:1, c0:c0 + cb]
                + u_s[pl.ds(HALO, tm), :] * cw_ref[1:2, c0:c0 + cb]
                + u_s[pl.ds(HALO + 1, tm), :] * cw_ref[2:3, c0:c0 + cb])
        b_gate = _dot(xn_s[HALO:HALO + tm, :], win_ref[:, c0:c0 + cb])
        gate_s[:, c0:c0 + cb] = (b_gate * conv).astype(BF16)
    o_ref[0] = x + _dot(gate_s[...], wout_ref[...])


def _halo_specs(tm, s, d):
    nb = tm // HALO
    last = s // HALO - 1
    return [
        pl.BlockSpec((1, tm, d), lambda b, i: (b, i, 0)),
        pl.BlockSpec((1, HALO, d), lambda b, i: (b, jnp.maximum(i * nb - 1, 0), 0)),
        pl.BlockSpec((1, HALO, d), lambda b, i: (b, jnp.minimum((i + 1) * nb, last), 0)),
    ]


def _conv_layer(h, g, w_in, conv_w, w_out):
    bsz, s, d = h.shape
    tm = min(ROW_TILE, s)
    cb = 512
    kern = functools.partial(_conv_kernel, tm=tm, d=d, cb=cb)
    return pl.pallas_call(
        kern,
        grid=(bsz, s // tm),
        in_specs=_halo_specs(tm, s, d) + [
            _full((1, d)), _full((d, 3 * d)), _full((3, d)), _full((d, d))],
        out_specs=pl.BlockSpec((1, tm, d), lambda b, i: (b, i, 0)),
        out_shape=jax.ShapeDtypeStruct((bsz, s, d), F32),
        scratch_shapes=[pltpu.VMEM((tm + 2 * HALO, d), BF16),
                        pltpu.VMEM((tm + 2 * HALO, cb), F32),
                        pltpu.VMEM((tm, d), BF16)],
        compiler_params=_params("parallel", "parallel"),
        name="conv_mixer",
    )(h, h, h, g.reshape(1, d), w_in.astype(BF16), conv_w, w_out.astype(BF16))


def _kv_kernel(m_ref, g_ref, w_ref, o_ref):
    xn = _rms(m_ref[0], g_ref[...]).astype(BF16)
    o_ref[0] = _dot(xn, w_ref[...]).astype(BF16)


def _kv_proj(mem, g, w_kv):
    bsz, m, d = mem.shape
    return pl.pallas_call(
        _kv_kernel,
        grid=(bsz,),
        in_specs=[pl.BlockSpec((1, m, d), lambda b: (b, 0, 0)), _full((1, d)), _full((d, 2 * d))],
        out_specs=pl.BlockSpec((1, m, 2 * d), lambda b: (b, 0, 0)),
        out_shape=jax.ShapeDtypeStruct((bsz, m, 2 * d), BF16),
        compiler_params=_params("parallel"),
        name="kv_proj",
    )(mem, g.reshape(1, d), w_kv.astype(BF16))


def _xattn_kernel(h_ref, gx_ref, wq_ref, kv_ref, wo_ref, gf_ref, rhi_ref, rlo_ref,
                  h_out, xn_out, probs_out, o_s, *, d, heads):
    x = h_ref[0]
    xn = _rms(x, gx_ref[...]).astype(BF16)
    hd = d // heads
    q = (_dot(xn, wq_ref[...]) * (hd ** -0.5)).astype(BF16)
    kv = kv_ref[0]
    scs = [_dot_nt(q[:, a * hd:(a + 1) * hd], kv[:, a * hd:(a + 1) * hd]) for a in range(heads)]
    ps = [jnp.exp(sc - jnp.max(sc, axis=-1, keepdims=True)) for sc in scs]
    for a, p in enumerate(ps):
        l = jnp.sum(p, axis=-1, keepdims=True)
        o = _dot(p.astype(BF16), kv[:, d + a * hd:d + (a + 1) * hd]) / l
        o_s[:, a * hd:(a + 1) * hd] = o.astype(BF16)
    hn = x + _dot(o_s[...], wo_ref[...])
    h_out[0] = hn
    xf = _rms(hn, gf_ref[...])
    hi = xf.astype(BF16)
    xn_out[0] = hi
    lo = (xf - hi.astype(F32)).astype(BF16)
    lg = _dot_nt(rhi_ref[...], hi) + _dot_nt(rhi_ref[...], lo) + _dot_nt(rlo_ref[...], hi)
    e = jnp.exp(lg - jnp.max(lg, axis=0, keepdims=True))
    probs_out[0] = e / jnp.sum(e, axis=0, keepdims=True)


def _xattn_layer(h, kv, gx, w_q, w_o, gf, router):
    bsz, s, d = h.shape
    m = kv.shape[1]
    e = router.shape[1]
    tm = min(ROW_TILE, s)
    rt = router.T
    r_hi = rt.astype(BF16)
    r_lo = (rt - r_hi.astype(F32)).astype(BF16)
    kern = functools.partial(_xattn_kernel, d=d, heads=XATTN_HEADS)
    return pl.pallas_call(
        kern,
        grid=(bsz, s // tm),
        in_specs=[pl.BlockSpec((1, tm, d), lambda b, i: (b, i, 0)),
                  _full((1, d)), _full((d, d)),
                  pl.BlockSpec((1, m, 2 * d), lambda b, i: (b, 0, 0)),
                  _full((d, d)), _full((1, d)), _full((e, d)), _full((e, d))],
        out_specs=[pl.BlockSpec((1, tm, d), lambda b, i: (b, i, 0)),
                   pl.BlockSpec((1, tm, d), lambda b, i: (b, i, 0)),
                   pl.BlockSpec((1, e, tm), lambda b, i: (b, 0, i))],
        out_shape=[jax.ShapeDtypeStruct((bsz, s, d), F32),
                   jax.ShapeDtypeStruct((bsz, s, d), BF16),
                   jax.ShapeDtypeStruct((bsz, e, s), F32)],
        scratch_shapes=[pltpu.VMEM((tm, d), BF16)],
        compiler_params=_params("parallel", "parallel"),
        name="xattn_router",
    )(h, gx.reshape(1, d), w_q.astype(BF16), kv, w_o.astype(BF16), gf.reshape(1, d), r_hi, r_lo)


def _select_kernel(p_ref, code_ref, start_ref, *, cap, blk):
    p = p_ref[0]
    e, s = p.shape
    bits = pltpu.bitcast(p, I32)

    def count(mask):
        return jnp.sum(mask.astype(F32), axis=1, keepdims=True)

    def search(k, t):
        cand = jnp.bitwise_or(t, jnp.left_shift(jnp.int32(1), 30 - k))
        return jnp.where(count(bits >= cand) >= cap, cand, t)

    thr = lax.fori_loop(0, 31, search, jnp.zeros((e, 1), I32))
    gt = bits > thr
    eq = bits == thr
    need = cap - count(gt)

    r = lax.broadcasted_iota(I32, (blk, blk), 0)
    c = lax.broadcasted_iota(I32, (blk, blk), 1)
    tri = jnp.where(r < c, 1.0, 0.0).astype(BF16)

    def prefix(mask_f):
        out, carries = [], []
        carry = jnp.zeros((e, 1), F32)
        for j in range(0, s, blk):
            mb = mask_f[:, j:j + blk]
            carries.append(carry)
            out.append(_dot(mb.astype(BF16), tri) + carry)
            carry = carry + jnp.sum(mb, axis=1, keepdims=True)
        return out, carries + [carry]

    eq_f = eq.astype(F32)
    tie_rank, _ = prefix(eq_f)
    for j0, tr in zip(range(0, s, blk), tie_rank):
        sel_b = jnp.logical_or(gt[:, j0:j0 + blk],
                               jnp.logical_and(eq[:, j0:j0 + blk], tr < need))
        code_ref[0, :, j0:j0 + blk] = sel_b.astype(I32)
    sel_f = code_ref[0].astype(F32)
    rank, starts = prefix(sel_f)
    for j0, rk in zip(range(0, s, blk), rank):
        code_ref[0, :, j0:j0 + blk] = jnp.where(sel_f[:, j0:j0 + blk] > 0.0, rk.astype(I32), -1)
    start_ref[...] = jnp.zeros_like(start_ref)
    for j, st in enumerate(starts):
        start_ref[0, :, j:j + 1] = st.astype(I32)


def _select(probs_t, cap):
    bsz, e, s = probs_t.shape
    blk = min(GATHER_BLOCK, s)
    nb1 = s // blk + 1
    assert nb1 <= LANES
    kern = functools.partial(_select_kernel, cap=cap, blk=blk)
    code, starts = pl.pallas_call(
        kern,
        grid=(bsz,),
        in_specs=[pl.BlockSpec((1, e, s), lambda b: (b, 0, 0))],
        out_specs=[pl.BlockSpec((1, e, s), lambda b: (b, 0, 0)),
                   pl.BlockSpec((1, e, LANES), lambda b: (b, 0, 0))],
        out_shape=[jax.ShapeDtypeStruct((bsz, e, s), I32),
                   jax.ShapeDtypeStruct((bsz, e, LANES), I32)],
        compiler_params=_params("parallel"),
        name="expert_choice_select",
    )(probs_t)
    return code, starts[:, :, :nb1].reshape(-1)


def _window(s_lo, s_hi, win):
    s0 = jnp.bitwise_and(s_lo, -BF16_SUBLANES)
    n_win = jnp.where(s_hi > s_lo, lax.shift_right_logical(s_hi - s0 + (win - 1),
                                                           win.bit_length() - 1), 0)
    return s0, n_win


def _gather_kernel(st_ref, x_ref, code_ref, p_ref, xe_ref, aff_ref, xe_s, aff_s, *, cap, blk, win):
    b = pl.program_id(0)
    k = pl.program_id(1)
    s = x_ref.shape[1]
    nb = s // blk

    base = (b * pl.num_programs(1) + k) * (nb + 1)
    iota = lax.broadcasted_iota(I32, (win, blk), 0)

    def gather(j, row0):
        onehot = jnp.where(iota + row0 == code_ref[0, :, j * blk:(j + 1) * blk], 1.0, 0.0)
        part = _dot(onehot.astype(BF16), x_ref[0, j * blk:(j + 1) * blk, :])
        xe_s[pl.ds(row0, win), :] += part
        aff_s[pl.ds(row0, win), :] += jnp.sum(onehot * p_ref[0, :, j * blk:(j + 1) * blk],
                                              axis=1, keepdims=True)

    xe_s[...] = jnp.zeros_like(xe_s)
    aff_s[...] = jnp.zeros_like(aff_s)
    wins = [_window(st_ref[base + j], st_ref[base + j + 1], win) for j in range(nb)]
    for j, (s0, _) in enumerate(wins):
        gather(j, pl.multiple_of(s0, BF16_SUBLANES))
    for j, (s0, n_win) in enumerate(wins):
        def more(i, carry, j=j, s0=s0):
            gather(j, pl.multiple_of(s0 + i * win, BF16_SUBLANES))
            return carry
        lax.fori_loop(1, n_win, more, 0)

    xe_ref[0, 0] = xe_s[0:cap, :].astype(BF16)
    aff_ref[0, 0] = jnp.broadcast_to(aff_s[0:cap, :], (cap, LANES))


def _expert_gather(xn, code, probs_t, starts, cap):
    bsz, s, d = xn.shape
    e = code.shape[1]
    blk = min(GATHER_BLOCK, s)
    win = min(GATHER_WINDOW, cap)
    kern = functools.partial(_gather_kernel, cap=cap, blk=blk, win=win)
    row = pl.BlockSpec((1, 1, s), lambda b, k, st: (b * e + k, 0, 0))
    return pl.pallas_call(
        kern,
        grid_spec=pltpu.PrefetchScalarGridSpec(
            num_scalar_prefetch=1,
            grid=(bsz, e),
            in_specs=[pl.BlockSpec((1, s, d), lambda b, k, st: (b, 0, 0)), row, row],
            out_specs=[pl.BlockSpec((1, 1, cap, d), lambda b, k, st: (b, k, 0, 0)),
                       pl.BlockSpec((1, 1, cap, LANES), lambda b, k, st: (b, k, 0, 0))],
            scratch_shapes=[pltpu.VMEM((cap + win, d), F32), pltpu.VMEM((cap + win, 1), F32)]),
        out_shape=[jax.ShapeDtypeStruct((bsz, e, cap, d), BF16),
                   jax.ShapeDtypeStruct((bsz, e, cap, LANES), F32)],
        compiler_params=_params("parallel", "parallel"),
        name="expert_gather",
    )(starts, xn, code.reshape(bsz * e, 1, s), probs_t.reshape(bsz * e, 1, s))


def _ffn_kernel(xe_ref, aff_ref, wg0_ref, wu0_ref, wd0_ref, wgn_ref, wun_ref, wdn_ref, y_ref,
                wg_s, wu_s, wd_s):
    k = pl.program_id(0)
    b = pl.program_id(1)
    cur = lax.rem(k, 2)
    rows_in = wgn_ref.shape[3]
    rows_hid = wdn_ref.shape[3]

    @pl.when(jnp.logical_and(k == 0, b == 0))
    def _():
        wg_s[0] = wg0_ref[0, 0].astype(BF16)
        wu_s[0] = wu0_ref[0, 0].astype(BF16)
        wd_s[0] = wd0_ref[0, 0].astype(BF16)

    nxt = 1 - cur
    r_in = pl.ds(pl.multiple_of(b * rows_in, BF16_SUBLANES), rows_in)
    r_hid = pl.ds(pl.multiple_of(b * rows_hid, BF16_SUBLANES), rows_hid)
    wg_s[nxt, r_in, :] = wgn_ref[0, 0, 0].astype(BF16)
    wu_s[nxt, r_in, :] = wun_ref[0, 0, 0].astype(BF16)
    wd_s[nxt, r_hid, :] = wdn_ref[0, 0, 0].astype(BF16)

    xe = xe_ref[0, 0]
    fh = wg_s.shape[2] // 2
    halves = [(_dot(xe, wg_s[cur, :, c0:c0 + fh]), _dot(xe, wu_s[cur, :, c0:c0 + fh]))
              for c0 in (0, fh)]
    y = None
    for i, (gt, up) in enumerate(halves):
        hid = (gt * jax.nn.sigmoid(gt) * up).astype(BF16)
        part = _dot(hid, wd_s[cur, i * fh:(i + 1) * fh, :])
        y = part if y is None else y + part
    aff = aff_ref[0, 0]
    y_ref[0, 0] = (y * jnp.concatenate([aff] * (y.shape[1] // LANES), axis=1)).astype(BF16)


def _expert_ffn(xe, aff, w_gate, w_up, w_down, layer):
    bsz, e, cap, d = xe.shape
    depth, _, _, f = w_gate.shape
    assert d % (bsz * BF16_SUBLANES) == 0 and f % (bsz * BF16_SUBLANES) == 0
    first = dict(pipeline_mode=pl.Buffered(1))

    def whole(r, c):
        return pl.BlockSpec((1, 1, r, c), lambda k, b: (layer, 0, 0, 0), **first)

    def piece(r, c):
        return pl.BlockSpec((1, 1, 1, r // bsz, c),
                            lambda k, b: (layer, jnp.minimum(k + 1, e - 1), b, 0, 0))

    def pieces(wt):
        _, _, r, c = wt.shape
        return wt.reshape(depth, e, bsz, r // bsz, c)

    return pl.pallas_call(
        _ffn_kernel,
        grid=(e, bsz),
        in_specs=[pl.BlockSpec((1, 1, cap, d), lambda k, b: (b, k, 0, 0)),
                  pl.BlockSpec((1, 1, cap, LANES), lambda k, b: (b, k, 0, 0)),
                  whole(d, f), whole(d, f), whole(f, d),
                  piece(d, f), piece(d, f), piece(f, d)],
        out_specs=pl.BlockSpec((1, 1, cap, d), lambda k, b: (b, k, 0, 0)),
        out_shape=jax.ShapeDtypeStruct((bsz, e, cap, d), BF16),
        scratch_shapes=[pltpu.VMEM((2, d, f), BF16), pltpu.VMEM((2, d, f), BF16),
                        pltpu.VMEM((2, f, d), BF16)],
        compiler_params=_params("arbitrary", "arbitrary"),
        name="expert_ffn",
    )(xe, aff, w_gate, w_up, w_down, pieces(w_gate), pieces(w_up), pieces(w_down))


def _combine_kernel(st_ref, h_ref, code_ref, y_ref, g_ref, o_ref, acc_s,
                    *, cap, win, per, final_norm):
    b = pl.program_id(0)
    j = pl.program_id(1)
    nb = pl.num_programs(1)
    tb = h_ref.shape[1]
    n_exp = y_ref.shape[1]
    iota = lax.broadcasted_iota(I32, (tb, win), 1)
    code = code_ref[0]

    def window(k, lo):
        row0 = pl.multiple_of(jnp.minimum(lo, cap - win), BF16_SUBLANES)
        code_k = jnp.where(code[:, k:k + 1] >= lo, code[:, k:k + 1], -1)
        onehot = jnp.where(iota + row0 == code_k, 1.0, 0.0).astype(BF16)
        return onehot, y_ref[0, k, pl.ds(row0, win), :]

    wins = []
    for k in range(n_exp):
        base = (b * n_exp + k) * (nb * per + 1) + j * per
        wins.append(_window(st_ref[base], st_ref[base + per], win))
    acc = h_ref[0]
    for k in range(0, n_exp, 2):
        oh_a, y_a = window(k, wins[k][0])
        oh_b, y_b = window(k + 1, wins[k + 1][0])
        acc = acc + _dot(jnp.concatenate([oh_a, oh_b], axis=1),
                         jnp.concatenate([y_a, y_b], axis=0))
    acc_s[...] = acc
    for k, (s0, n_win) in enumerate(wins):
        def more(i, carry, k=k, s0=s0):
            acc_s[...] += _dot(*window(k, s0 + i * win))
            return carry
        lax.fori_loop(1, n_win, more, 0)
    acc = acc_s[...]
    o_ref[0] = _rms(acc, g_ref[...]) if final_norm else acc


def _moe_combine(h, code_t, starts, ye, g_final, final_norm):
    bsz, s, d = h.shape
    e, cap = ye.shape[1], ye.shape[2]
    assert e % 2 == 0
    tb = min(TOKEN_BLOCK, s)
    win = min(PICK_WINDOW, cap)
    per = tb // min(GATHER_BLOCK, s)
    kern = functools.partial(_combine_kernel, cap=cap, win=win, per=per, final_norm=final_norm)
    return pl.pallas_call(
        kern,
        grid_spec=pltpu.PrefetchScalarGridSpec(
            num_scalar_prefetch=1,
            grid=(bsz, s // tb),
            in_specs=[pl.BlockSpec((1, tb, d), lambda b, j, st: (b, j, 0)),
                      pl.BlockSpec((1, tb, e), lambda b, j, st: (b, j, 0)),
                      pl.BlockSpec((1, e, cap, d), lambda b, j, st: (b, 0, 0, 0)),
                      pl.BlockSpec((1, d), lambda b, j, st: (0, 0))],
            out_specs=pl.BlockSpec((1, tb, d), lambda b, j, st: (b, j, 0)),
            scratch_shapes=[pltpu.VMEM((tb, d), F32)]),
        out_shape=jax.ShapeDtypeStruct((bsz, s, d), F32),
        compiler_params=_params("parallel", "parallel"),
        name="moe_combine",
    )(starts, h, code_t, ye, g_final.reshape(1, d))


def _rwkv_prep_kernel(h_ref, hp_ref, hn_ref, g_ref, mu_ref, wrkv_ref, w1_ref, a1_ref, g1_ref,
                      w2_ref, a2_ref, g2_ref, w0_ref, a0_ref, kk_ref, ka_ref, rk_ref, hs_ref,
                      r_out, v_out, kn_out, gate_out, bonus_out, lw_out, kd_out, bd_out,
                      xn_s, *, tm):
    i = pl.program_id(1)
    g = g_ref[...]
    keep_prev = jnp.where(i == 0, 0.0, 1.0)
    keep_next = jnp.where(i == pl.num_programs(1) - 1, 0.0, 1.0)
    xn_s[0:HALO, :] = _rms(hp_ref[0], g) * keep_prev
    xn_s[HALO:HALO + tm, :] = _rms(h_ref[0], g)
    xn_s[HALO + tm:, :] = _rms(hn_ref[0], g) * keep_next
    xn = xn_s[pl.ds(HALO, tm), :]
    xx = 0.5 * (xn_s[pl.ds(HALO - 1, tm), :] + xn_s[pl.ds(HALO + 1, tm), :]) - xn

    def mix(j):
        return (xn + xx * mu_ref[j:j + 1, :]).astype(BF16)

    r = _dot(mix(0), wrkv_ref[0])
    k = _dot(mix(1), wrkv_ref[1])
    v = _dot(mix(2), wrkv_ref[2])
    hw = jnp.tanh(_dot(mix(3), w1_ref[...])).astype(BF16)
    ha = _dot(mix(4), a1_ref[...]).astype(BF16)
    hg = jax.nn.sigmoid(_dot(mix(5), g1_ref[...])).astype(BF16)
    gate_out[0] = _dot(hg, g2_ref[...]).astype(BF16)

    hsum = hs_ref[...]
    kk = k * kk_ref[...]
    nrm2 = _head_sum(kk * kk, hsum, two_term=False)
    kn = kk / jnp.maximum(jnp.sqrt(nrm2), 1e-12)
    r_out[0] = r.astype(BF16)
    v_out[0] = v.astype(BF16)
    kn_out[0] = kn.astype(BF16)
    ksum = None
    for n in range(2):
        w_raw = w0_ref[n:n + 1, :] + _dot(hw, w2_ref[n])
        lw_out[n, 0] = -DECAY_SCALE * jax.nn.sigmoid(w_raw)
        a = jax.nn.sigmoid(a0_ref[n:n + 1, :] + _dot(ha, a2_ref[n]))
        kd = k * (1.0 + (a - 1.0) * ka_ref[...])
        kd_out[n, 0] = kd.astype(BF16)
        bd_out[n, 0] = (kn * a).astype(BF16)
        ksum = kd if ksum is None else ksum + kd
    coef = _head_sum(r * ksum * rk_ref[...], hsum, two_term=False)
    bonus_out[0] = (coef * v).astype(BF16)


def _head_sum_matrix(d, n):
    assert d // n <= LANES
    return (jnp.arange(d)[:, None] // n == jnp.arange(LANES)[None, :]).astype(BF16)


def _head_collect(x, hs, two_term):
    x_hi = x.astype(BF16)
    sums = _dot(x_hi, hs)
    if two_term:
        sums = sums + _dot((x - x_hi.astype(F32)).astype(BF16), hs)
    return sums


def _head_spread(sums, hs, two_term):
    s_hi = sums.astype(BF16)
    out = _dot_nt(s_hi, hs)
    if two_term:
        out = out + _dot_nt((sums - s_hi.astype(F32)).astype(BF16), hs)
    return out


def _head_sum(x, hs, two_term):
    return _head_spread(_head_collect(x, hs, two_term), hs, two_term)


def _rwkv_prep(h, g, mu, w_rkv, w0, w1, w2, a0, a1, a2, g1, g2, k_k, k_a, r_k):
    bsz, s, d = h.shape
    tm = min(RWKV_ROW_TILE, s)
    lora = w1.shape[-1]
    glora = g1.shape[-1]
    w1c = jnp.concatenate([w1[0], w1[1]], axis=1).astype(BF16)
    a1c = jnp.concatenate([a1[0], a1[1]], axis=1).astype(BF16)
    keep = (jnp.arange(2 * lora)[None, :, None] // lora) == jnp.arange(2)[:, None, None]
    w2p = jnp.where(keep, jnp.concatenate([w2, w2], axis=1), 0.0).astype(BF16)
    a2p = jnp.where(keep, jnp.concatenate([a2, a2], axis=1), 0.0).astype(BF16)
    kern = functools.partial(_rwkv_prep_kernel, tm=tm)
    tok = pl.BlockSpec((1, tm, d), lambda b, i: (b, i, 0))
    tok2 = pl.BlockSpec((2, 1, tm, d), lambda b, i: (0, b, i, 0))
    sd = jax.ShapeDtypeStruct
    return pl.pallas_call(
        kern,
        grid=(bsz, s // tm),
        in_specs=_halo_specs(tm, s, d) + [
            _full((1, d)), _full((6, d)), _full((3, d, d)),
            _full((d, 2 * lora)), _full((d, 2 * lora)), _full((d, glora)),
            _full((2, 2 * lora, d)), _full((2, 2 * lora, d)), _full((glora, d)),
            _full((2, d)), _full((2, d)), _full((1, d)), _full((1, d)), _full((1, d)),
            _full((d, LANES))],
        out_specs=[tok, tok, tok, tok, tok, tok2, tok2, tok2],
        out_shape=[sd((bsz, s, d), BF16)] * 5 + [sd((2, bsz, s, d), F32),
                                                 sd((2, bsz, s, d), BF16),
                                                 sd((2, bsz, s, d), BF16)],
        scratch_shapes=[pltpu.VMEM((tm + 2 * HALO, d), F32)],
        compiler_params=_params("parallel", "parallel"),
        name="rwkv_prep",
    )(h, h, h, g.reshape(1, d), mu, w_rkv.astype(BF16), w1c, a1c, g1.astype(BF16),
      w2p, a2p, g2.astype(BF16), w0, a0,
      k_k.reshape(1, d), k_a.reshape(1, d), r_k.reshape(1, d),
      _head_sum_matrix(d, RWKV_HEAD_DIM))


def _wkv_kernel(rf_ref, vf_ref, knf_ref, rr_ref, vr_ref, knr_ref,
                lwf_ref, kdf_ref, bdf_ref, lwr_ref, kdr_ref, bdr_ref,
                yf_ref, yr_ref, qf_s, qr_s, rh_s, y0_s, nm_s, q0_s, gt_s, *, c, ng):
    hd = RWKV_HEAD_DIM
    w = 2 * hd
    j = pl.program_id(2)
    n_steps = pl.num_programs(2)
    carried = (rh_s, y0_s, nm_s, q0_s, gt_s)

    @pl.when(j == 0)
    def _():
        for ref in (qf_s, qr_s) + carried:
            ref[...] = jnp.zeros_like(ref)

    ti = lax.broadcasted_iota(I32, (c, c), 0)
    si = lax.broadcasted_iota(I32, (c, c), 1)
    lane = lax.broadcasted_iota(I32, (1, w), 1)
    m_lo = lane < hd
    ti2 = lax.broadcasted_iota(I32, (c, 2 * c), 0)
    si2 = lax.broadcasted_iota(I32, (c, 2 * c), 1)
    si2 = jnp.where(si2 >= c, si2 - c, si2)
    eye2 = jnp.where(si2 == ti2, 1.0, 0.0)
    dir_masks = (
        (jnp.where(si <= ti, 1.0, 0.0).astype(BF16), si2 <= ti2, si2 < ti2),
        (jnp.where(si >= ti, 1.0, 0.0).astype(BF16), si2 >= ti2, si2 > ti2),
    )
    rr = lax.broadcasted_iota(I32, (2 * c, 2 * c), 0)
    cc = lax.broadcasted_iota(I32, (2 * c, 2 * c), 1)
    bd_mask_c = jnp.where(rr < c, 0, 1) == jnp.where(cc < c, 0, 1)
    rr = lax.broadcasted_iota(I32, (w, w), 0)
    cc = lax.broadcasted_iota(I32, (w, w), 1)
    bd_mask_h = jnp.where(rr < hd, 0, 1) == jnp.where(cc < hd, 0, 1)

    def row_stack(x):
        return jnp.concatenate([jnp.where(m_lo, x, 0.0), jnp.where(m_lo, 0.0, x)],
                               axis=0).astype(BF16)

    def block_diag(xp):
        return jnp.where(bd_mask_c, jnp.concatenate([xp, xp], axis=0), 0.0).astype(BF16)

    fwd = (0, rf_ref, vf_ref, knf_ref, lwf_ref, kdf_ref, bdf_ref)
    rev = (1, rr_ref, vr_ref, knr_ref, lwr_ref, kdr_ref, bdr_ref)
    y_refs = (yf_ref, yr_ref)
    q = [qf_s[...], qr_s[...]]

    def stages(probs, st):
        for dirn, r_ref, v_ref, kn_ref, lw_ref, kd_ref, bd_ref, ci in probs:
            tri_incl, _, _ = dir_masks[dirn]
            sl = pl.ds(ci * c, c)
            lw = lw_ref[0, 0, sl, :]
            lw_hi = lw.astype(BF16)
            lw_lo = (lw - lw_hi.astype(F32)).astype(BF16)
            l_incl = _dot(tri_incl, lw_hi) + _dot(tri_incl, lw_lo)
            st.append(dict(dirn=dirn, sl=sl, lw=lw, l_incl=l_incl,
                           r=r_ref[0, sl, :].astype(F32), v=v_ref[0, sl, :].astype(F32),
                           kn=kn_ref[0, sl, :].astype(F32), kd=kd_ref[0, 0, sl, :].astype(F32),
                           bd=bd_ref[0, 0, sl, :].astype(F32)))
        yield
        for p in st:
            stage_decay(p)
        yield
        for p in st:
            stage_masks(p)
        yield
        for _ in range(max(c.bit_length() - 3, 0)):
            for p in st:
                res = _dot(jnp.concatenate([p["pw"], p["t_p"]], axis=0).astype(BF16),
                           block_diag(p["pw"]))
                p["pw"] = res[0:c]
                p["t_p"] = p["t_p"] + res[c:2 * c]
            yield
        for p in st:
            stage_solve(p)
        yield
        for p in st:
            stage_maps(p)
        yield

    def handover(off_f, off_r):
        for i in range(ng):
            for dirn, slot, row in ((0, i, off_f + i * c), (1, ng + i, off_r + (ng - 1 - i) * c)):
                q_b = q[dirn].astype(BF16)
                y_refs[dirn][0, pl.ds(pl.multiple_of(row, c), c), :] = (
                    _dot_nt(rh_s[slot], q_b) + y0_s[slot])
                q[dirn] = q[dirn] * gt_s[slot] + _dot(q_b, nm_s[slot]) + q0_s[slot]
            yield

    def stage_decay(p):
        l_incl = p["l_incl"]
        l_tot = l_incl[0:1, :] if p["dirn"] else l_incl[c - 1:c, :]
        g_inv = jnp.exp(-l_incl)
        g_end = jnp.exp(l_tot - l_incl)
        p["g_tot"] = jnp.exp(l_tot)
        p["rt"] = p["r"] * jnp.exp(l_incl)
        p["at"] = -p["kn"] * jnp.exp(l_incl - p["lw"])
        p["v_rs"] = row_stack(p["v"])
        p["bh"] = (p["bd"] * g_end).astype(BF16)
        p["kh"] = (p["kd"] * g_end).astype(BF16)
        lhs = jnp.concatenate([p["at"], p["rt"]], axis=0).astype(BF16)
        rhs = jnp.concatenate([row_stack(p["bd"] * g_inv), row_stack(p["kd"] * g_inv)],
                              axis=0)
        p["gm"] = _dot_nt(lhs, rhs)

    def stage_masks(p):
        _, incl2, strict2 = dir_masks[p["dirn"]]
        gm = p.pop("gm")
        a_ab = jnp.where(strict2, gm[0:c, 0:2 * c], 0.0)
        a_ak = jnp.where(strict2, gm[0:c, 2 * c:4 * c], 0.0)
        a_rb = jnp.where(incl2, gm[c:2 * c, 0:2 * c], 0.0)
        a_rk = jnp.where(incl2, gm[c:2 * c, 2 * c:4 * c], 0.0)
        p["a_rb"] = a_rb.astype(BF16)
        av = _dot(jnp.concatenate([a_ak, a_rk], axis=0).astype(BF16), p["v_rs"])
        p["akv"] = av[0:c]
        p["arkv"] = av[c:2 * c]
        p["t_p"] = eye2 + a_ab
        p["pw"] = _dot(a_ab.astype(BF16), block_diag(a_ab))

    def stage_solve(p):
        t_p = p["t_p"] + _dot(p["t_p"].astype(BF16), block_diag(p["pw"]))
        wu = _dot(t_p.astype(BF16),
                  jnp.concatenate([row_stack(p["at"]), row_stack(p["akv"])], axis=1))
        p["w_m"] = wu[:, 0:w]
        p["u_t"] = wu[:, w:2 * w]

    def stage_maps(p):
        ry = _dot(p["a_rb"],
                  jnp.concatenate([row_stack(p["w_m"]), row_stack(p["u_t"])], axis=1))
        p["r_hat"] = (p["rt"] + ry[:, 0:w]).astype(BF16)
        p["y0"] = ry[:, w:2 * w] + p["arkv"]
        p["n_m"] = jnp.where(bd_mask_h, _dot_tn(p["w_m"].astype(BF16), p["bh"]),
                             0.0).astype(BF16)
        p["q0"] = jnp.where(
            bd_mask_h,
            _dot_tn(jnp.concatenate([p["u_t"], p["v"]], axis=0).astype(BF16),
                    jnp.concatenate([p["bh"], p["kh"]], axis=0)), 0.0)

    cg = ng * c
    probs = [fwd + (ci,) for ci in range(ng)] + [rev + (ci,) for ci in range(ng - 1, -1, -1)]
    st = []
    pending = handover(jnp.maximum(j - 1, 0) * cg, jnp.minimum(n_steps - j, n_steps - 1) * cg)
    for _ in stages(probs, st):
        next(pending, None)
    for _ in pending:
        pass
    for slot, p in enumerate(st):
        rh_s[slot] = p["r_hat"]
        y0_s[slot] = p["y0"]
        nm_s[slot] = p["n_m"]
        q0_s[slot] = p["q0"]
        gt_s[slot] = p["g_tot"]
    qf_s[...] = q[0]
    qr_s[...] = q[1]

    @pl.when(j == n_steps - 1)
    def _():
        q[0] = qf_s[...]
        q[1] = qr_s[...]
        for _ in handover(j * cg, 0):
            pass


def _wkv(r, v, kn, lw, kd, bd):
    bsz, s, d = r.shape
    c = min(CHUNK, s)
    ng = min(CHUNK_GROUP, s // c)
    cg = c * ng
    n_steps = s // cg
    w = 2 * RWKV_HEAD_DIM
    tok_f = pl.BlockSpec((1, cg, w), lambda b, p, j: (b, j, p))
    tok_r = pl.BlockSpec((1, cg, w), lambda b, p, j: (b, n_steps - 1 - j, p))
    dir_f = pl.BlockSpec((1, 1, cg, w), lambda b, p, j: (0, b, j, p))
    dir_r = pl.BlockSpec((1, 1, cg, w), lambda b, p, j: (1, b, n_steps - 1 - j, p))
    kern = functools.partial(_wkv_kernel, c=c, ng=ng)
    seq = pl.BlockSpec((1, s, w), lambda b, p, j: (b, 0, p))
    return pl.pallas_call(
        kern,
        grid=(bsz, d // w, n_steps),
        in_specs=[tok_f, tok_f, tok_f, tok_r, tok_r, tok_r,
                  dir_f, dir_f, dir_f, dir_r, dir_r, dir_r],
        out_specs=[seq, seq],
        out_shape=[jax.ShapeDtypeStruct((bsz, s, d), F32)] * 2,
        scratch_shapes=[pltpu.VMEM((w, w), F32), pltpu.VMEM((w, w), F32),
                        pltpu.VMEM((2 * ng, c, w), BF16), pltpu.VMEM((2 * ng, c, w), F32),
                        pltpu.VMEM((2 * ng, w, w), BF16), pltpu.VMEM((2 * ng, w, w), F32),
                        pltpu.VMEM((2 * ng, 1, w), F32)],
        compiler_params=_params("parallel", "parallel", "arbitrary"),
        name="wkv",
    )(r, v, kn, r, v, kn, lw, kd, bd, lw, kd, bd)


def _rwkv_post_kernel(h_ref, yf_ref, yr_ref, bonus_ref, gate_ref, lnw_ref, lnb_ref, hs_ref,
                      wout_ref, o_ref, z_s, *, n):
    hsum = hs_ref[...]
    tm = h_ref.shape[1]
    rows = [pl.ds(i * (tm // POST_ROW_GROUPS), tm // POST_ROW_GROUPS)
            for i in range(POST_ROW_GROUPS)]
    ys = [yf_ref[0, r, :] + yr_ref[0, r, :] for r in rows]
    sums = [_head_collect(y, hsum, two_term=True) for y in ys]
    ycs = [y - _head_spread(sm, hsum, two_term=True) * (1.0 / n) for y, sm in zip(ys, sums)]
    sums = [_head_collect(yc * yc, hsum, two_term=False) for yc in ycs]
    for r, yc, sm in zip(rows, ycs, sums):
        var = _head_spread(sm, hsum, two_term=False) * (1.0 / n)
        yn = yc * lax.rsqrt(var + GN_EPS) * lnw_ref[...] + lnb_ref[...]
        yn = yn + bonus_ref[0, r, :].astype(F32)
        z_s[r, :] = (yn * gate_ref[0, r, :].astype(F32)).astype(BF16)
    o_ref[0] = h_ref[0] + _dot(z_s[...], wout_ref[...])


def _rwkv_post(h, yf, yr, bonus, gate, ln_w, ln_b, w_out):
    bsz, s, d = h.shape
    tm = min(ROW_TILE, s)
    tok = pl.BlockSpec((1, tm, d), lambda b, i: (b, i, 0))
    kern = functools.partial(_rwkv_post_kernel, n=RWKV_HEAD_DIM)
    return pl.pallas_call(
        kern,
        grid=(bsz, s // tm),
        in_specs=[tok, tok, tok, tok, tok, _full((1, d)), _full((1, d)), _full((d, LANES)),
                  _full((d, d))],
        out_specs=tok,
        out_shape=jax.ShapeDtypeStruct((bsz, s, d), F32),
        scratch_shapes=[pltpu.VMEM((tm, d), BF16)],
        compiler_params=_params("parallel", "parallel"),
        name="rwkv_post",
    )(h, yf, yr, bonus, gate, ln_w.reshape(1, d), ln_b.reshape(1, d),
      _head_sum_matrix(d, RWKV_HEAD_DIM), w_out.astype(BF16))


def kernel(x, mem, norm_mix, norm_xattn, norm_mem, norm_ffn, norm_final,
           conv_w_in, conv_w, conv_w_out,
           rwkv_mu, rwkv_w_rkv, rwkv_w0, rwkv_w1, rwkv_w2, rwkv_a0, rwkv_a1, rwkv_a2,
           rwkv_g1, rwkv_g2, rwkv_k_k, rwkv_k_a, rwkv_r_k, rwkv_ln_w, rwkv_ln_b, rwkv_w_out,
           xattn_w_q, xattn_w_kv, xattn_w_o,
           moe_router, moe_w_gate, moe_w_up, moe_w_down):
    depth = norm_mix.shape[0]
    bsz, s, d = x.shape
    n_exp = moe_router.shape[-1]
    cap = CAPACITY_FACTOR * s // n_exp
    h = x
    for i in range(depth):
        j = i // N_MIXERS
        if i % N_MIXERS == 0:
            h = _conv_layer(h, norm_mix[i], conv_w_in[j], conv_w[j], conv_w_out[j])
        else:
            r, v, kn, gate, bonus, lw, kd, bd = _rwkv_prep(
                h, norm_mix[i], rwkv_mu[j], rwkv_w_rkv[j], rwkv_w0[j], rwkv_w1[j], rwkv_w2[j],
                rwkv_a0[j], rwkv_a1[j], rwkv_a2[j], rwkv_g1[j], rwkv_g2[j],
                rwkv_k_k[j], rwkv_k_a[j], rwkv_r_k[j])
            yf, yr = _wkv(r, v, kn, lw, kd, bd)
            h = _rwkv_post(h, yf, yr, bonus, gate, rwkv_ln_w[j], rwkv_ln_b[j], rwkv_w_out[j])
        kv = _kv_proj(mem, norm_mem[i], xattn_w_kv[i])
        h, xn, probs_t = _xattn_layer(h, kv, norm_xattn[i], xattn_w_q[i], xattn_w_o[i],
                                      norm_ffn[i], moe_router[i])
        code, starts = _select(probs_t, cap)
        xe, aff = _expert_gather(xn, code, probs_t, starts, cap)
        ye = _expert_ffn(xe, aff, moe_w_gate, moe_w_up, moe_w_down, i)
        h = _moe_combine(h, jnp.swapaxes(code, 1, 2), starts, ye, norm_final,
                         final_norm=(i == depth - 1))
    return h
```

```python
import functools

import jax
import jax.numpy as jnp
from jax import lax
from jax.experimental import pallas as pl
from jax.experimental.pallas import tpu as pltpu

F32 = jnp.float32
BF16 = jnp.bfloat16
I32 = jnp.int32

N_MIXERS = 2
RWKV_HEAD_DIM = 64
XATTN_HEADS = 4
CAPACITY_FACTOR = 2
GN_EPS = 64e-5
RMS_EPS = 1e-6
DECAY_SCALE = 0.6065306597126334

V7X_VMEM_LIMIT_BYTES = 56 * 1024 * 1024
LANES = 128
BF16_SUBLANES = 16

ROW_TILE = 512
RWKV_ROW_TILE = 512
HALO = BF16_SUBLANES
POST_ROW_GROUPS = 2
CHUNK = 64
CHUNK_GROUP = 8
WKV_PAIRS = 2
TOKEN_BLOCK = 512
PICK_WINDOW = 128
GATHER_BLOCK = 512
GATHER_WINDOW = 128


def _params(*semantics):
    return pltpu.CompilerParams(dimension_semantics=semantics,
                                vmem_limit_bytes=V7X_VMEM_LIMIT_BYTES)


def _rms(x, g):
    return x * lax.rsqrt(jnp.mean(x * x, axis=-1, keepdims=True) + RMS_EPS) * g


def _dot(a, b):
    return jnp.dot(a, b, preferred_element_type=F32)


def _dot_nt(a, b):
    return lax.dot_general(a, b, (((1,), (1,)), ((), ())), preferred_element_type=F32)


def _dot_tn(a, b):
    return lax.dot_general(a, b, (((0,), (0,)), ((), ())), preferred_element_type=F32)


def _full(shape):
    n = len(shape)
    return pl.BlockSpec(shape, lambda *_: (0,) * n)


def _conv_kernel(h_ref, hp_ref, hn_ref, g_ref, win_ref, cw_ref, wout_ref, o_ref,
                 xn_s, u_s, gate_s, *, tm, d, cb):
    i = pl.program_id(1)
    g = g_ref[...]
    x = h_ref[0]
    xn_s[0:HALO, :] = _rms(hp_ref[0], g).astype(BF16)
    xn_s[HALO:HALO + tm, :] = _rms(x, g).astype(BF16)
    xn_s[HALO + tm:, :] = _rms(hn_ref[0], g).astype(BF16)
    rows = tm + 2 * HALO
    row = lax.broadcasted_iota(I32, (rows, 1), 0)
    lo = jnp.where(i == 0, HALO, 0)
    hi = jnp.where(i == pl.num_programs(1) - 1, HALO + tm, rows)
    pad = jnp.logical_or(row < lo, row >= hi)
    xa = xn_s[...]
    for c0 in range(0, d, cb):
        c_gate = _dot(xa, win_ref[:, d + c0:d + c0 + cb])
        hx = _dot(xa, win_ref[:, 2 * d + c0:2 * d + c0 + cb])
        u_s[...] = jnp.where(pad, 0.0, c_gate * hx)
        conv = (u_s[pl.ds(HALO - 1, tm), :] * cw_ref[0:1, c0:c0 + cb]
                + u_s[pl.ds(HALO, tm), :] * cw_ref[1:2, c0:c0 + cb]
                + u_s[pl.ds(HALO + 1, tm), :] * cw_ref[2:3, c0:c0 + cb])
        b_gate = _dot(xn_s[HALO:HALO + tm, :], win_ref[:, c0:c0 + cb])
        gate_s[:, c0:c0 + cb] = (b_gate * conv).astype(BF16)
    o_ref[0] = x + _dot(gate_s[...], wout_ref[...])


def _halo_specs(tm, s, d):
    nb = tm // HALO
    last = s // HALO - 1
    return [
        pl.BlockSpec((1, tm, d), lambda b, i: (b, i, 0)),
        pl.BlockSpec((1, HALO, d), lambda b, i: (b, jnp.maximum(i * nb - 1, 0), 0)),
        pl.BlockSpec((1, HALO, d), lambda b, i: (b, jnp.minimum((i + 1) * nb, last), 0)),
    ]


def _conv_layer(h, g, w_in, conv_w, w_out):
    bsz, s, d = h.shape
    tm = min(ROW_TILE, s)
    cb = 512
    kern = functools.partial(_conv_kernel, tm=tm, d=d, cb=cb)
    return pl.pallas_call(
        kern,
        grid=(bsz, s // tm),
        in_specs=_halo_specs(tm, s, d) + [
            _full((1, d)), _full((d, 3 * d)), _full((3, d)), _full((d, d))],
        out_specs=pl.BlockSpec((1, tm, d), lambda b, i: (b, i, 0)),
        out_shape=jax.ShapeDtypeStruct((bsz, s, d), F32),
        scratch_shapes=[pltpu.VMEM((tm + 2 * HALO, d), BF16),
                        pltpu.VMEM((tm + 2 * HALO, cb), F32),
                        pltpu.VMEM((tm, d), BF16)],
        compiler_params=_params("parallel", "parallel"),
        name="conv_mixer",
    )(h, h, h, g.reshape(1, d), w_in.astype(BF16), conv_w, w_out.astype(BF16))


def _kv_kernel(m_ref, g_ref, w_ref, o_ref):
    xn = _rms(m_ref[0], g_ref[...]).astype(BF16)
    o_ref[0] = _dot(xn, w_ref[...]).astype(BF16)


def _kv_proj(mem, g, w_kv):
    bsz, m, d = mem.shape
    return pl.pallas_call(
        _kv_kernel,
        grid=(bsz,),
        in_specs=[pl.BlockSpec((1, m, d), lambda b: (b, 0, 0)), _full((1, d)), _full((d, 2 * d))],
        out_specs=pl.BlockSpec((1, m, 2 * d), lambda b: (b, 0, 0)),
        out_shape=jax.ShapeDtypeStruct((bsz, m, 2 * d), BF16),
        compiler_params=_params("parallel"),
        name="kv_proj",
    )(mem, g.reshape(1, d), w_kv.astype(BF16))


def _xattn_kernel(h_ref, gx_ref, wq_ref, kv_ref, wo_ref, gf_ref, rhi_ref, rlo_ref,
                  h_out, xn_out, probs_out, o_s, *, d, heads):
    x = h_ref[0]
    xn = _rms(x, gx_ref[...]).astype(BF16)
    hd = d // heads
    q = (_dot(xn, wq_ref[...]) * (hd ** -0.5)).astype(BF16)
    kv = kv_ref[0]
    scs = [_dot_nt(q[:, a * hd:(a + 1) * hd], kv[:, a * hd:(a + 1) * hd]) for a in range(heads)]
    ps = [jnp.exp(sc - jnp.max(sc, axis=-1, keepdims=True)) for sc in scs]
    for a, p in enumerate(ps):
        l = jnp.sum(p, axis=-1, keepdims=True)
        o = _dot(p.astype(BF16), kv[:, d + a * hd:d + (a + 1) * hd]) / l
        o_s[:, a * hd:(a + 1) * hd] = o.astype(BF16)
    hn = x + _dot(o_s[...], wo_ref[...])
    h_out[0] = hn
    xf = _rms(hn, gf_ref[...])
    hi = xf.astype(BF16)
    xn_out[0] = hi
    lo = (xf - hi.astype(F32)).astype(BF16)
    lg = _dot_nt(rhi_ref[...], hi) + _dot_nt(rhi_ref[...], lo) + _dot_nt(rlo_ref[...], hi)
    e = jnp.exp(lg - jnp.max(lg, axis=0, keepdims=True))
    probs_out[0] = e / jnp.sum(e, axis=0, keepdims=True)


def _xattn_layer(h, kv, gx, w_q, w_o, gf, router):
    bsz, s, d = h.shape
    m = kv.shape[1]
    e = router.shape[1]
    tm = min(ROW_TILE, s)
    rt = router.T
    r_hi = rt.astype(BF16)
    r_lo = (rt - r_hi.astype(F32)).astype(BF16)
    kern = functools.partial(_xattn_kernel, d=d, heads=XATTN_HEADS)
    return pl.pallas_call(
        kern,
        grid=(bsz, s // tm),
        in_specs=[pl.BlockSpec((1, tm, d), lambda b, i: (b, i, 0)),
                  _full((1, d)), _full((d, d)),
                  pl.BlockSpec((1, m, 2 * d), lambda b, i: (b, 0, 0)),
                  _full((d, d)), _full((1, d)), _full((e, d)), _full((e, d))],
        out_specs=[pl.BlockSpec((1, tm, d), lambda b, i: (b, i, 0)),
                   pl.BlockSpec((1, tm, d), lambda b, i: (b, i, 0)),
                   pl.BlockSpec((1, e, tm), lambda b, i: (b, 0, i))],
        out_shape=[jax.ShapeDtypeStruct((bsz, s, d), F32),
                   jax.ShapeDtypeStruct((bsz, s, d), BF16),
                   jax.ShapeDtypeStruct((bsz, e, s), F32)],
        scratch_shapes=[pltpu.VMEM((tm, d), BF16)],
        compiler_params=_params("parallel", "parallel"),
        name="xattn_router",
    )(h, gx.reshape(1, d), w_q.astype(BF16), kv, w_o.astype(BF16), gf.reshape(1, d), r_hi, r_lo)


def _select_kernel(p_ref, code_ref, start_ref, *, cap, blk):
    p = p_ref[0]
    e, s = p.shape
    bits = pltpu.bitcast(p, I32)

    def count(mask):
        return jnp.sum(mask.astype(F32), axis=1, keepdims=True)

    def search(k, t):
        cand = jnp.bitwise_or(t, jnp.left_shift(jnp.int32(1), 30 - k))
        return jnp.where(count(bits >= cand) >= cap, cand, t)

    thr = lax.fori_loop(0, 31, search, jnp.zeros((e, 1), I32))
    gt = bits > thr
    eq = bits == thr
    need = cap - count(gt)

    r = lax.broadcasted_iota(I32, (blk, blk), 0)
    c = lax.broadcasted_iota(I32, (blk, blk), 1)
    tri = jnp.where(r < c, 1.0, 0.0).astype(BF16)

    def prefix(mask_f):
        out, carries = [], []
        carry = jnp.zeros((e, 1), F32)
        for j in range(0, s, blk):
            mb = mask_f[:, j:j + blk]
            carries.append(carry)
            out.append(_dot(mb.astype(BF16), tri) + carry)
            carry = carry + jnp.sum(mb, axis=1, keepdims=True)
        return out, carries + [carry]

    eq_f = eq.astype(F32)
    tie_rank, _ = prefix(eq_f)
    for j0, tr in zip(range(0, s, blk), tie_rank):
        sel_b = jnp.logical_or(gt[:, j0:j0 + blk],
                               jnp.logical_and(eq[:, j0:j0 + blk], tr < need))
        code_ref[0, :, j0:j0 + blk] = sel_b.astype(I32)
    sel_f = code_ref[0].astype(F32)
    rank, starts = prefix(sel_f)
    for j0, rk in zip(range(0, s, blk), rank):
        code_ref[0, :, j0:j0 + blk] = jnp.where(sel_f[:, j0:j0 + blk] > 0.0, rk.astype(I32), -1)
    start_ref[...] = jnp.zeros_like(start_ref)
    for j, st in enumerate(starts):
        start_ref[0, :, j:j + 1] = st.astype(I32)


def _select(probs_t, cap):
    bsz, e, s = probs_t.shape
    blk = min(GATHER_BLOCK, s)
    nb1 = s // blk + 1
    assert nb1 <= LANES
    kern = functools.partial(_select_kernel, cap=cap, blk=blk)
    code, starts = pl.pallas_call(
        kern,
        grid=(bsz,),
        in_specs=[pl.BlockSpec((1, e, s), lambda b: (b, 0, 0))],
        out_specs=[pl.BlockSpec((1, e, s), lambda b: (b, 0, 0)),
                   pl.BlockSpec((1, e, LANES), lambda b: (b, 0, 0))],
        out_shape=[jax.ShapeDtypeStruct((bsz, e, s), I32),
                   jax.ShapeDtypeStruct((bsz, e, LANES), I32)],
        compiler_params=_params("parallel"),
        name="expert_choice_select",
    )(probs_t)
    return code, starts[:, :, :nb1].reshape(-1)


def _window(s_lo, s_hi, win):
    s0 = jnp.bitwise_and(s_lo, -BF16_SUBLANES)
    n_win = jnp.where(s_hi > s_lo, lax.shift_right_logical(s_hi - s0 + (win - 1),
                                                           win.bit_length() - 1), 0)
    return s0, n_win


def _gather_kernel(st_ref, x_ref, code_ref, p_ref, xe_ref, aff_ref, xe_s, aff_s, *, cap, blk, win):
    b = pl.program_id(0)
    k = pl.program_id(1)
    s = x_ref.shape[1]
    nb = s // blk

    base = (b * pl.num_programs(1) + k) * (nb + 1)
    iota = lax.broadcasted_iota(I32, (win, blk), 0)

    def gather(j, row0):
        onehot = jnp.where(iota + row0 == code_ref[0, :, j * blk:(j + 1) * blk], 1.0, 0.0)
        part = _dot(onehot.astype(BF16), x_ref[0, j * blk:(j + 1) * blk, :])
        xe_s[pl.ds(row0, win), :] += part
        aff_s[pl.ds(row0, win), :] += jnp.sum(onehot * p_ref[0, :, j * blk:(j + 1) * blk],
                                              axis=1, keepdims=True)

    xe_s[...] = jnp.zeros_like(xe_s)
    aff_s[...] = jnp.zeros_like(aff_s)
    wins = [_window(st_ref[base + j], st_ref[base + j + 1], win) for j in range(nb)]
    for j, (s0, _) in enumerate(wins):
        gather(j, pl.multiple_of(s0, BF16_SUBLANES))
    for j, (s0, n_win) in enumerate(wins):
        def more(i, carry, j=j, s0=s0):
            gather(j, pl.multiple_of(s0 + i * win, BF16_SUBLANES))
            return carry
        lax.fori_loop(1, n_win, more, 0)

    xe_ref[0, 0] = xe_s[0:cap, :].astype(BF16)
    aff_ref[0, 0] = jnp.broadcast_to(aff_s[0:cap, :], (cap, LANES))


def _expert_gather(xn, code, probs_t, starts, cap):
    bsz, s, d = xn.shape
    e = code.shape[1]
    blk = min(GATHER_BLOCK, s)
    win = min(GATHER_WINDOW, cap)
    kern = functools.partial(_gather_kernel, cap=cap, blk=blk, win=win)
    row = pl.BlockSpec((1, 1, s), lambda b, k, st: (b * e + k, 0, 0))
    return pl.pallas_call(
        kern,
        grid_spec=pltpu.PrefetchScalarGridSpec(
            num_scalar_prefetch=1,
            grid=(bsz, e),
            in_specs=[pl.BlockSpec((1, s, d), lambda b, k, st: (b, 0, 0)), row, row],
            out_specs=[pl.BlockSpec((1, 1, cap, d), lambda b, k, st: (b, k, 0, 0)),
                       pl.BlockSpec((1, 1, cap, LANES), lambda b, k, st: (b, k, 0, 0))],
            scratch_shapes=[pltpu.VMEM((cap + win, d), F32), pltpu.VMEM((cap + win, 1), F32)]),
        out_shape=[jax.ShapeDtypeStruct((bsz, e, cap, d), BF16),
                   jax.ShapeDtypeStruct((bsz, e, cap, LANES), F32)],
        compiler_params=_params("parallel", "parallel"),
        name="expert_gather",
    )(starts, xn, code.reshape(bsz * e, 1, s), probs_t.reshape(bsz * e, 1, s))


def _ffn_kernel(xe_ref, aff_ref, wg0_ref, wu0_ref, wd0_ref, wgn_ref, wun_ref, wdn_ref, y_ref,
                wg_s, wu_s, wd_s):
    k = pl.program_id(0)
    b = pl.program_id(1)
    cur = lax.rem(k, 2)
    rows_in = wgn_ref.shape[3]
    rows_hid = wdn_ref.shape[3]

    @pl.when(jnp.logical_and(k == 0, b == 0))
    def _():
        wg_s[0] = wg0_ref[0, 0].astype(BF16)
        wu_s[0] = wu0_ref[0, 0].astype(BF16)
        wd_s[0] = wd0_ref[0, 0].astype(BF16)

    nxt = 1 - cur
    r_in = pl.ds(pl.multiple_of(b * rows_in, BF16_SUBLANES), rows_in)
    r_hid = pl.ds(pl.multiple_of(b * rows_hid, BF16_SUBLANES), rows_hid)
    wg_s[nxt, r_in, :] = wgn_ref[0, 0, 0].astype(BF16)
    wu_s[nxt, r_in, :] = wun_ref[0, 0, 0].astype(BF16)
    wd_s[nxt, r_hid, :] = wdn_ref[0, 0, 0].astype(BF16)

    xe = xe_ref[0, 0]
    fh = wg_s.shape[2] // 2
    halves = [(_dot(xe, wg_s[cur, :, c0:c0 + fh]), _dot(xe, wu_s[cur, :, c0:c0 + fh]))
              for c0 in (0, fh)]
    y = None
    for i, (gt, up) in enumerate(halves):
        hid = (gt * jax.nn.sigmoid(gt) * up).astype(BF16)
        part = _dot(hid, wd_s[cur, i * fh:(i + 1) * fh, :])
        y = part if y is None else y + part
    aff = aff_ref[0, 0]
    y_ref[0, 0] = (y * jnp.concatenate([aff] * (y.shape[1] // LANES), axis=1)).astype(BF16)


def _expert_ffn(xe, aff, w_gate, w_up, w_down, layer):
    bsz, e, cap, d = xe.shape
    depth, _, _, f = w_gate.shape
    assert d % (bsz * BF16_SUBLANES) == 0 and f % (bsz * BF16_SUBLANES) == 0
    first = dict(pipeline_mode=pl.Buffered(1))

    def whole(r, c):
        return pl.BlockSpec((1, 1, r, c), lambda k, b: (layer, 0, 0, 0), **first)

    def piece(r, c):
        return pl.BlockSpec((1, 1, 1, r // bsz, c),
                            lambda k, b: (layer, jnp.minimum(k + 1, e - 1), b, 0, 0))

    def pieces(wt):
        _, _, r, c = wt.shape
        return wt.reshape(depth, e, bsz, r // bsz, c)

    return pl.pallas_call(
        _ffn_kernel,
        grid=(e, bsz),
        in_specs=[pl.BlockSpec((1, 1, cap, d), lambda k, b: (b, k, 0, 0)),
                  pl.BlockSpec((1, 1, cap, LANES), lambda k, b: (b, k, 0, 0)),
                  whole(d, f), whole(d, f), whole(f, d),
                  piece(d, f), piece(d, f), piece(f, d)],
        out_specs=pl.BlockSpec((1, 1, cap, d), lambda k, b: (b, k, 0, 0)),
        out_shape=jax.ShapeDtypeStruct((bsz, e, cap, d), BF16),
        scratch_shapes=[pltpu.VMEM((2, d, f), BF16), pltpu.VMEM((2, d, f), BF16),
                        pltpu.VMEM((2, f, d), BF16)],
        compiler_params=_params("arbitrary", "arbitrary"),
        name="expert_ffn",
    )(xe, aff, w_gate, w_up, w_down, pieces(w_gate), pieces(w_up), pieces(w_down))


def _combine_kernel(st_ref, h_ref, code_ref, y_ref, g_ref, o_ref, acc_s,
                    *, cap, win, per, final_norm):
    b = pl.program_id(0)
    j = pl.program_id(1)
    nb = pl.num_programs(1)
    tb = h_ref.shape[1]
    n_exp = y_ref.shape[1]
    iota = lax.broadcasted_iota(I32, (tb, win), 1)
    code = code_ref[0]

    def window(k, lo):
        row0 = pl.multiple_of(jnp.minimum(lo, cap - win), BF16_SUBLANES)
        code_k = jnp.where(code[:, k:k + 1] >= lo, code[:, k:k + 1], -1)
        onehot = jnp.where(iota + row0 == code_k, 1.0, 0.0).astype(BF16)
        return onehot, y_ref[0, k, pl.ds(row0, win), :]

    wins = []
    for k in range(n_exp):
        base = (b * n_exp + k) * (nb * per + 1) + j * per
        wins.append(_window(st_ref[base], st_ref[base + per], win))
    acc = h_ref[0]
    for k in range(0, n_exp, 2):
        oh_a, y_a = window(k, wins[k][0])
        oh_b, y_b = window(k + 1, wins[k + 1][0])
        acc = acc + _dot(jnp.concatenate([oh_a, oh_b], axis=1),
                         jnp.concatenate([y_a, y_b], axis=0))
    acc_s[...] = acc
    for k, (s0, n_win) in enumerate(wins):
        def more(i, carry, k=k, s0=s0):
            acc_s[...] += _dot(*window(k, s0 + i * win))
            return carry
        lax.fori_loop(1, n_win, more, 0)
    acc = acc_s[...]
    o_ref[0] = _rms(acc, g_ref[...]) if final_norm else acc


def _moe_combine(h, code_t, starts, ye, g_final, final_norm):
    bsz, s, d = h.shape
    e, cap = ye.shape[1], ye.shape[2]
    assert e % 2 == 0
    tb = min(TOKEN_BLOCK, s)
    win = min(PICK_WINDOW, cap)
    per = tb // min(GATHER_BLOCK, s)
    kern = functools.partial(_combine_kernel, cap=cap, win=win, per=per, final_norm=final_norm)
    return pl.pallas_call(
        kern,
        grid_spec=pltpu.PrefetchScalarGridSpec(
            num_scalar_prefetch=1,
            grid=(bsz, s // tb),
            in_specs=[pl.BlockSpec((1, tb, d), lambda b, j, st: (b, j, 0)),
                      pl.BlockSpec((1, tb, e), lambda b, j, st: (b, j, 0)),
                      pl.BlockSpec((1, e, cap, d), lambda b, j, st: (b, 0, 0, 0)),
                      pl.BlockSpec((1, d), lambda b, j, st: (0, 0))],
            out_specs=pl.BlockSpec((1, tb, d), lambda b, j, st: (b, j, 0)),
            scratch_shapes=[pltpu.VMEM((tb, d), F32)]),
        out_shape=jax.ShapeDtypeStruct((bsz, s, d), F32),
        compiler_params=_params("parallel", "parallel"),
        name="moe_combine",
    )(starts, h, code_t, ye, g_final.reshape(1, d))


def _rwkv_prep_kernel(h_ref, hp_ref, hn_ref, g_ref, mu_ref, wrkv_ref, w1_ref, a1_ref, g1_ref,
                      w2_ref, a2_ref, g2_ref, w0_ref, a0_ref, kk_ref, ka_ref, rk_ref, hs_ref,
                      r_out, v_out, kn_out, gate_out, bonus_out, lw_out, kd_out, bd_out,
                      xn_s, *, tm):
    i = pl.program_id(1)
    g = g_ref[...]
    keep_prev = jnp.where(i == 0, 0.0, 1.0)
    keep_next = jnp.where(i == pl.num_programs(1) - 1, 0.0, 1.0)
    xn_s[0:HALO, :] = _rms(hp_ref[0], g) * keep_prev
    xn_s[HALO:HALO + tm, :] = _rms(h_ref[0], g)
    xn_s[HALO + tm:, :] = _rms(hn_ref[0], g) * keep_next
    xn = xn_s[pl.ds(HALO, tm), :]
    xx = 0.5 * (xn_s[pl.ds(HALO - 1, tm), :] + xn_s[pl.ds(HALO + 1, tm), :]) - xn

    def mix(j):
        return (xn + xx * mu_ref[j:j + 1, :]).astype(BF16)

    r = _dot(mix(0), wrkv_ref[0])
    k = _dot(mix(1), wrkv_ref[1])
    v = _dot(mix(2), wrkv_ref[2])
    hw = jnp.tanh(_dot(mix(3), w1_ref[...])).astype(BF16)
    ha = _dot(mix(4), a1_ref[...]).astype(BF16)
    hg = jax.nn.sigmoid(_dot(mix(5), g1_ref[...])).astype(BF16)
    gate_out[0] = _dot(hg, g2_ref[...]).astype(BF16)

    hsum = hs_ref[...]
    kk = k * kk_ref[...]
    nrm2 = _head_sum(kk * kk, hsum, two_term=False)
    kn = kk / jnp.maximum(jnp.sqrt(nrm2), 1e-12)
    r_out[0] = r.astype(BF16)
    v_out[0] = v.astype(BF16)
    kn_out[0] = kn.astype(BF16)
    ksum = None
    for n in range(2):
        w_raw = w0_ref[n:n + 1, :] + _dot(hw, w2_ref[n])
        lw_out[n, 0] = -DECAY_SCALE * jax.nn.sigmoid(w_raw)
        a = jax.nn.sigmoid(a0_ref[n:n + 1, :] + _dot(ha, a2_ref[n]))
        kd = k * (1.0 + (a - 1.0) * ka_ref[...])
        kd_out[n, 0] = kd.astype(BF16)
        bd_out[n, 0] = (kn * a).astype(BF16)
        ksum = kd if ksum is None else ksum + kd
    coef = _head_sum(r * ksum * rk_ref[...], hsum, two_term=False)
    bonus_out[0] = (coef * v).astype(BF16)


def _head_sum_matrix(d, n):
    assert d // n <= LANES
    return (jnp.arange(d)[:, None] // n == jnp.arange(LANES)[None, :]).astype(BF16)


def _head_collect(x, hs, two_term):
    x_hi = x.astype(BF16)
    sums = _dot(x_hi, hs)
    if two_term:
        sums = sums + _dot((x - x_hi.astype(F32)).astype(BF16), hs)
    return sums


def _head_spread(sums, hs, two_term):
    s_hi = sums.astype(BF16)
    out = _dot_nt(s_hi, hs)
    if two_term:
        out = out + _dot_nt((sums - s_hi.astype(F32)).astype(BF16), hs)
    return out


def _head_sum(x, hs, two_term):
    return _head_spread(_head_collect(x, hs, two_term), hs, two_term)


def _rwkv_prep(h, g, mu, w_rkv, w0, w1, w2, a0, a1, a2, g1, g2, k_k, k_a, r_k):
    bsz, s, d = h.shape
    tm = min(RWKV_ROW_TILE, s)
    lora = w1.shape[-1]
    glora = g1.shape[-1]
    w1c = jnp.concatenate([w1[0], w1[1]], axis=1).astype(BF16)
    a1c = jnp.concatenate([a1[0], a1[1]], axis=1).astype(BF16)
    keep = (jnp.arange(2 * lora)[None, :, None] // lora) == jnp.arange(2)[:, None, None]
    w2p = jnp.where(keep, jnp.concatenate([w2, w2], axis=1), 0.0).astype(BF16)
    a2p = jnp.where(keep, jnp.concatenate([a2, a2], axis=1), 0.0).astype(BF16)
    kern = functools.partial(_rwkv_prep_kernel, tm=tm)
    tok = pl.BlockSpec((1, tm, d), lambda b, i: (b, i, 0))
    tok2 = pl.BlockSpec((2, 1, tm, d), lambda b, i: (0, b, i, 0))
    sd = jax.ShapeDtypeStruct
    return pl.pallas_call(
        kern,
        grid=(bsz, s // tm),
        in_specs=_halo_specs(tm, s, d) + [
            _full((1, d)), _full((6, d)), _full((3, d, d)),
            _full((d, 2 * lora)), _full((d, 2 * lora)), _full((d, glora)),
            _full((2, 2 * lora, d)), _full((2, 2 * lora, d)), _full((glora, d)),
            _full((2, d)), _full((2, d)), _full((1, d)), _full((1, d)), _full((1, d)),
            _full((d, LANES))],
        out_specs=[tok, tok, tok, tok, tok, tok2, tok2, tok2],
        out_shape=[sd((bsz, s, d), BF16)] * 5 + [sd((2, bsz, s, d), F32),
                                                 sd((2, bsz, s, d), BF16),
                                                 sd((2, bsz, s, d), BF16)],
        scratch_shapes=[pltpu.VMEM((tm + 2 * HALO, d), F32)],
        compiler_params=_params("parallel", "parallel"),
        name="rwkv_prep",
    )(h, h, h, g.reshape(1, d), mu, w_rkv.astype(BF16), w1c, a1c, g1.astype(BF16),
      w2p, a2p, g2.astype(BF16), w0, a0,
      k_k.reshape(1, d), k_a.reshape(1, d), r_k.reshape(1, d),
      _head_sum_matrix(d, RWKV_HEAD_DIM))


def _wkv_kernel(rf_ref, vf_ref, knf_ref, rr_ref, vr_ref, knr_ref,
                lwf_ref, kdf_ref, bdf_ref, lwr_ref, kdr_ref, bdr_ref,
                yf_ref, yr_ref, q_s, rh_s, y0_s, nm_s, q0_s, gt_s, *, c, ng, npair):
    hd = RWKV_HEAD_DIM
    w = 2 * hd
    j = pl.program_id(2)
    n_steps = pl.num_programs(2)
    carried = (rh_s, y0_s, nm_s, q0_s, gt_s)

    @pl.when(j == 0)
    def _():
        for ref in (q_s,) + carried:
            ref[...] = jnp.zeros_like(ref)

    row_c = lax.broadcasted_iota(I32, (c, 1), 0)
    lane = lax.broadcasted_iota(I32, (1, w), 1)
    m_lo = lane < hd
    ti2 = lax.broadcasted_iota(I32, (c, 2 * c), 0)
    si2 = lax.broadcasted_iota(I32, (c, 2 * c), 1)
    si2 = jnp.where(si2 >= c, si2 - c, si2)
    eye2 = jnp.where(si2 == ti2, 1.0, 0.0)
    dir_masks = ((si2 <= ti2, si2 < ti2), (si2 >= ti2, si2 > ti2))
    rr = lax.broadcasted_iota(I32, (2 * c, 2 * c), 0)
    cc = lax.broadcasted_iota(I32, (2 * c, 2 * c), 1)
    bd_mask_c = jnp.where(rr < c, 0, 1) == jnp.where(cc < c, 0, 1)
    rr = lax.broadcasted_iota(I32, (w, w), 0)
    cc = lax.broadcasted_iota(I32, (w, w), 1)
    bd_mask_h = jnp.where(rr < hd, 0, 1) == jnp.where(cc < hd, 0, 1)

    def row_stack(x):
        return jnp.concatenate([jnp.where(m_lo, x, 0.0), jnp.where(m_lo, 0.0, x)],
                               axis=0).astype(BF16)

    def block_diag(xp):
        return jnp.where(bd_mask_c, jnp.concatenate([xp, xp], axis=0), 0.0).astype(BF16)

    fwd = (0, rf_ref, vf_ref, knf_ref, lwf_ref, kdf_ref, bdf_ref)
    rev = (1, rr_ref, vr_ref, knr_ref, lwr_ref, kdr_ref, bdr_ref)
    y_refs = (yf_ref, yr_ref)
    chains = [(dirn, pi) for dirn in (0, 1) for pi in range(npair)]
    q = [q_s[ch] for ch in range(len(chains))]

    def stages(probs, st):
        for (dirn, r_ref, v_ref, kn_ref, lw_ref, kd_ref, bd_ref), pi, ci in probs:
            sl = (pl.ds(ci * c, c), pl.ds(pi * w, w))
            lw = lw_ref[(0, 0) + sl]
            l_incl = lw
            step = 1
            while step < c:
                if dirn:
                    l_incl = l_incl + jnp.where(row_c < c - step,
                                                pltpu.roll(l_incl, c - step, axis=0), 0.0)
                else:
                    l_incl = l_incl + jnp.where(row_c >= step,
                                                pltpu.roll(l_incl, step, axis=0), 0.0)
                step *= 2
            st.append(dict(dirn=dirn, lw=lw, l_incl=l_incl,
                           r=r_ref[(0,) + sl].astype(F32), v=v_ref[(0,) + sl].astype(F32),
                           kn=kn_ref[(0,) + sl].astype(F32), kd=kd_ref[(0, 0) + sl].astype(F32),
                           bd=bd_ref[(0, 0) + sl].astype(F32)))
        yield
        for p in st:
            stage_decay(p)
        yield
        for p in st:
            stage_masks(p)
        yield
        for _ in range(max(c.bit_length() - 3, 0)):
            for p in st:
                res = _dot(jnp.concatenate([p["pw"], p["t_p"]], axis=0).astype(BF16),
                           block_diag(p["pw"]))
                p["pw"] = res[0:c]
                p["t_p"] = p["t_p"] + res[c:2 * c]
            yield
        for p in st:
            stage_solve(p)
        yield
        for p in st:
            stage_maps(p)
        yield

    def handover(off_f, off_r):
        for i in range(ng):
            for ch, (dirn, pi) in enumerate(chains):
                slot = ch * ng + i
                row = off_r + (ng - 1 - i) * c if dirn else off_f + i * c
                q_b = q[ch].astype(BF16)
                y_refs[dirn][0, pl.ds(pl.multiple_of(row, c), c), pl.ds(pi * w, w)] = (
                    _dot_nt(rh_s[slot], q_b) + y0_s[slot])
                q[ch] = q[ch] * gt_s[slot] + _dot(q_b, nm_s[slot]) + q0_s[slot]
            yield

    def stage_decay(p):
        l_incl = p["l_incl"]
        l_tot = l_incl[0:1, :] if p["dirn"] else l_incl[c - 1:c, :]
        g_inv = jnp.exp(-l_incl)
        g_end = jnp.exp(l_tot - l_incl)
        p["g_tot"] = jnp.exp(l_tot)
        p["rt"] = p["r"] * jnp.exp(l_incl)
        p["at"] = -p["kn"] * jnp.exp(l_incl - p["lw"])
        p["v_rs"] = row_stack(p["v"])
        p["bh"] = (p["bd"] * g_end).astype(BF16)
        p["kh"] = (p["kd"] * g_end).astype(BF16)
        lhs = jnp.concatenate([p["at"], p["rt"]], axis=0).astype(BF16)
        rhs = jnp.concatenate([row_stack(p["bd"] * g_inv), row_stack(p["kd"] * g_inv)],
                              axis=0)
        p["gm"] = _dot_nt(lhs, rhs)

    def stage_masks(p):
        incl2, strict2 = dir_masks[p["dirn"]]
        gm = p.pop("gm")
        a_ab = jnp.where(strict2, gm[0:c, 0:2 * c], 0.0)
        a_ak = jnp.where(strict2, gm[0:c, 2 * c:4 * c], 0.0)
        a_rb = jnp.where(incl2, gm[c:2 * c, 0:2 * c], 0.0)
        a_rk = jnp.where(incl2, gm[c:2 * c, 2 * c:4 * c], 0.0)
        p["a_rb"] = a_rb.astype(BF16)
        av = _dot(jnp.concatenate([a_ak, a_rk], axis=0).astype(BF16), p["v_rs"])
        p["akv"] = av[0:c]
        p["arkv"] = av[c:2 * c]
        p["t_p"] = eye2 + a_ab
        p["pw"] = _dot(a_ab.astype(BF16), block_diag(a_ab))

    def stage_solve(p):
        t_p = p["t_p"] + _dot(p["t_p"].astype(BF16), block_diag(p["pw"]))
        wu = _dot(t_p.astype(BF16),
                  jnp.concatenate([row_stack(p["at"]), row_stack(p["akv"])], axis=1))
        p["w_m"] = wu[:, 0:w]
        p["u_t"] = wu[:, w:2 * w]

    def stage_maps(p):
        ry = _dot(p["a_rb"],
                  jnp.concatenate([row_stack(p["w_m"]), row_stack(p["u_t"])], axis=1))
        p["r_hat"] = (p["rt"] + ry[:, 0:w]).astype(BF16)
        p["y0"] = ry[:, w:2 * w] + p["arkv"]
        p["n_m"] = jnp.where(bd_mask_h, _dot_tn(p["w_m"].astype(BF16), p["bh"]),
                             0.0).astype(BF16)
        p["q0"] = jnp.where(
            bd_mask_h,
            _dot_tn(jnp.concatenate([p["u_t"], p["v"]], axis=0).astype(BF16),
                    jnp.concatenate([p["bh"], p["kh"]], axis=0)), 0.0)

    cg = ng * c
    probs = [((rev if dirn else fwd), pi, (ng - 1 - i if dirn else i))
             for dirn, pi in chains for i in range(ng)]
    st = []
    pending = handover(jnp.maximum(j - 1, 0) * cg, jnp.minimum(n_steps - j, n_steps - 1) * cg)
    for _ in stages(probs, st):
        next(pending, None)
    for _ in pending:
        pass
    for slot, p in enumerate(st):
        rh_s[slot] = p["r_hat"]
        y0_s[slot] = p["y0"]
        nm_s[slot] = p["n_m"]
        q0_s[slot] = p["q0"]
        gt_s[slot] = p["g_tot"]
    for ch in range(len(chains)):
        q_s[ch] = q[ch]

    @pl.when(j == n_steps - 1)
    def _():
        for ch in range(len(chains)):
            q[ch] = q_s[ch]
        for _ in handover(j * cg, 0):
            pass


def _wkv(r, v, kn, lw, kd, bd):
    bsz, s, d = r.shape
    c = min(CHUNK, s)
    ng = min(CHUNK_GROUP, s // c)
    cg = c * ng
    n_steps = s // cg
    w = 2 * RWKV_HEAD_DIM
    npair = WKV_PAIRS
    wb = w * npair
    assert d % wb == 0
    tok_f = pl.BlockSpec((1, cg, wb), lambda b, p, j: (b, j, p))
    tok_r = pl.BlockSpec((1, cg, wb), lambda b, p, j: (b, n_steps - 1 - j, p))
    dir_f = pl.BlockSpec((1, 1, cg, wb), lambda b, p, j: (0, b, j, p))
    dir_r = pl.BlockSpec((1, 1, cg, wb), lambda b, p, j: (1, b, n_steps - 1 - j, p))
    kern = functools.partial(_wkv_kernel, c=c, ng=ng, npair=npair)
    seq = pl.BlockSpec((1, s, wb), lambda b, p, j: (b, 0, p))
    nch = 2 * npair
    return pl.pallas_call(
        kern,
        grid=(bsz, d // wb, n_steps),
        in_specs=[tok_f, tok_f, tok_f, tok_r, tok_r, tok_r,
                  dir_f, dir_f, dir_f, dir_r, dir_r, dir_r],
        out_specs=[seq, seq],
        out_shape=[jax.ShapeDtypeStruct((bsz, s, d), F32)] * 2,
        scratch_shapes=[pltpu.VMEM((nch, w, w), F32),
                        pltpu.VMEM((nch * ng, c, w), BF16), pltpu.VMEM((nch * ng, c, w), F32),
                        pltpu.VMEM((nch * ng, w, w), BF16), pltpu.VMEM((nch * ng, w, w), F32),
                        pltpu.VMEM((nch * ng, 1, w), F32)],
        compiler_params=_params("parallel", "parallel", "arbitrary"),
        name="wkv",
    )(r, v, kn, r, v, kn, lw, kd, bd, lw, kd, bd)


def _rwkv_post_kernel(h_ref, yf_ref, yr_ref, bonus_ref, gate_ref, lnw_ref, lnb_ref, hs_ref,
                      wout_ref, o_ref, z_s, *, n):
    hsum = hs_ref[...]
    tm = h_ref.shape[1]
    rows = [pl.ds(i * (tm // POST_ROW_GROUPS), tm // POST_ROW_GROUPS)
            for i in range(POST_ROW_GROUPS)]
    ys = [yf_ref[0, r, :] + yr_ref[0, r, :] for r in rows]
    sums = [_head_collect(y, hsum, two_term=True) for y in ys]
    ycs = [y - _head_spread(sm, hsum, two_term=True) * (1.0 / n) for y, sm in zip(ys, sums)]
    sums = [_head_collect(yc * yc, hsum, two_term=False) for yc in ycs]
    for r, yc, sm in zip(rows, ycs, sums):
        var = _head_spread(sm, hsum, two_term=False) * (1.0 / n)
        yn = yc * lax.rsqrt(var + GN_EPS) * lnw_ref[...] + lnb_ref[...]
        yn = yn + bonus_ref[0, r, :].astype(F32)
        z_s[r, :] = (yn * gate_ref[0, r, :].astype(F32)).astype(BF16)
    o_ref[0] = h_ref[0] + _dot(z_s[...], wout_ref[...])


def _rwkv_post(h, yf, yr, bonus, gate, ln_w, ln_b, w_out):
    bsz, s, d = h.shape
    tm = min(ROW_TILE, s)
    tok = pl.BlockSpec((1, tm, d), lambda b, i: (b, i, 0))
    kern = functools.partial(_rwkv_post_kernel, n=RWKV_HEAD_DIM)
    return pl.pallas_call(
        kern,
        grid=(bsz, s // tm),
        in_specs=[tok, tok, tok, tok, tok, _full((1, d)), _full((1, d)), _full((d, LANES)),
                  _full((d, d))],
        out_specs=tok,
        out_shape=jax.ShapeDtypeStruct((bsz, s, d), F32),
        scratch_shapes=[pltpu.VMEM((tm, d), BF16)],
        compiler_params=_params("parallel", "parallel"),
        name="rwkv_post",
    )(h, yf, yr, bonus, gate, ln_w.reshape(1, d), ln_b.reshape(1, d),
      _head_sum_matrix(d, RWKV_HEAD_DIM), w_out.astype(BF16))


def kernel(x, mem, norm_mix, norm_xattn, norm_mem, norm_ffn, norm_final,
           conv_w_in, conv_w, conv_w_out,
           rwkv_mu, rwkv_w_rkv, rwkv_w0, rwkv_w1, rwkv_w2, rwkv_a0, rwkv_a1, rwkv_a2,
           rwkv_g1, rwkv_g2, rwkv_k_k, rwkv_k_a, rwkv_r_k, rwkv_ln_w, rwkv_ln_b, rwkv_w_out,
           xattn_w_q, xattn_w_kv, xattn_w_o,
           moe_router, moe_w_gate, moe_w_up, moe_w_down):
    depth = norm_mix.shape[0]
    bsz, s, d = x.shape
    n_exp = moe_router.shape[-1]
    cap = CAPACITY_FACTOR * s // n_exp
    h = x
    for i in range(depth):
        j = i // N_MIXERS
        if i % N_MIXERS == 0:
            h = _conv_layer(h, norm_mix[i], conv_w_in[j], conv_w[j], conv_w_out[j])
        else:
            r, v, kn, gate, bonus, lw, kd, bd = _rwkv_prep(
                h, norm_mix[i], rwkv_mu[j], rwkv_w_rkv[j], rwkv_w0[j], rwkv_w1[j], rwkv_w2[j],
                rwkv_a0[j], rwkv_a1[j], rwkv_a2[j], rwkv_g1[j], rwkv_g2[j],
                rwkv_k_k[j], rwkv_k_a[j], rwkv_r_k[j])
            yf, yr = _wkv(r, v, kn, lw, kd, bd)
            h = _rwkv_post(h, yf, yr, bonus, gate, rwkv_ln_w[j], rwkv_ln_b[j], rwkv_w_out[j])
        kv = _kv_proj(mem, norm_mem[i], xattn_w_kv[i])
        h, xn, probs_t = _xattn_layer(h, kv, norm_xattn[i], xattn_w_q[i], xattn_w_o[i],
                                      norm_ffn[i], moe_router[i])
        code, starts = _select(probs_t, cap)
        xe, aff = _expert_gather(xn, code, probs_t, starts, cap)
        ye = _expert_ffn(xe, aff, moe_w_gate, moe_w_up, moe_w_down, i)
        h = _moe_combine(h, jnp.swapaxes(code, 1, 2), starts, ye, norm_final,
                         final_norm=(i == depth - 1))
    return h
```

```python
import functools

import jax
import jax.numpy as jnp
from jax import lax
from jax.experimental import pallas as pl
from jax.experimental.pallas import tpu as pltpu

F32 = jnp.float32
BF16 = jnp.bfloat16
I32 = jnp.int32

N_MIXERS = 2
RWKV_HEAD_DIM = 64
XATTN_HEADS = 4
CAPACITY_FACTOR = 2
GN_EPS = 64e-5
RMS_EPS = 1e-6
DECAY_SCALE = 0.6065306597126334

V7X_VMEM_LIMIT_BYTES = 56 * 1024 * 1024
LANES = 128
BF16_SUBLANES = 16

ROW_TILE = 512
RWKV_ROW_TILE = 512
HALO = BF16_SUBLANES
POST_ROW_GROUPS = 2
CHUNK = 64
CHUNK_GROUP = 8
WKV_PAIRS = 2
TOKEN_BLOCK = 512
PICK_WINDOW = 128
GATHER_BLOCK = 512
GATHER_WINDOW = 128


def _params(*semantics):
    return pltpu.CompilerParams(dimension_semantics=semantics,
                                vmem_limit_bytes=V7X_VMEM_LIMIT_BYTES)


def _rms(x, g):
    return x * lax.rsqrt(jnp.mean(x * x, axis=-1, keepdims=True) + RMS_EPS) * g


def _dot(a, b):
    return jnp.dot(a, b, preferred_element_type=F32)


def _dot_nt(a, b):
    return lax.dot_general(a, b, (((1,), (1,)), ((), ())), preferred_element_type=F32)


def _dot_tn(a, b):
    return lax.dot_general(a, b, (((0,), (0,)), ((), ())), preferred_element_type=F32)


def _full(shape):
    n = len(shape)
    return pl.BlockSpec(shape, lambda *_: (0,) * n)


def _conv_kernel(h_ref, hp_ref, hn_ref, g_ref, win_ref, cw_ref, wout_ref, o_ref,
                 xn_s, u_s, gate_s, *, tm, d, cb):
    i = pl.program_id(1)
    g = g_ref[...]
    x = h_ref[0]
    xn_s[0:HALO, :] = _rms(hp_ref[0], g).astype(BF16)
    xn_s[HALO:HALO + tm, :] = _rms(x, g).astype(BF16)
    xn_s[HALO + tm:, :] = _rms(hn_ref[0], g).astype(BF16)
    rows = tm + 2 * HALO
    row = lax.broadcasted_iota(I32, (rows, 1), 0)
    lo = jnp.where(i == 0, HALO, 0)
    hi = jnp.where(i == pl.num_programs(1) - 1, HALO + tm, rows)
    pad = jnp.logical_or(row < lo, row >= hi)
    xa = xn_s[...]
    for c0 in range(0, d, cb):
        c_gate = _dot(xa, win_ref[:, d + c0:d + c0 + cb])
        hx = _dot(xa, win_ref[:, 2 * d + c0:2 * d + c0 + cb])
        u_s[...] = jnp.where(pad, 0.0, c_gate * hx)
        conv = (u_s[pl.ds(HALO - 1, tm), :] * cw_ref[0:1, c0:c0 + cb]
                + u_s[pl.ds(HALO, tm), :] * cw_ref[1:2, c0:c0 + cb]
                + u_s[pl.ds(HALO + 1, tm), :] * cw_ref[2:3, c0:c0 + cb])
        b_gate = _dot(xn_s[HALO:HALO + tm, :], win_ref[:, c0:c0 + cb])
        gate_s[:, c0:c0 + cb] = (b_gate * conv).astype(BF16)
    o_ref[0] = x + _dot(gate_s[...], wout_ref[...])


def _halo_specs(tm, s, d):
    nb = tm // HALO
    last = s // HALO - 1
    return [
        pl.BlockSpec((1, tm, d), lambda b, i: (b, i, 0)),
        pl.BlockSpec((1, HALO, d), lambda b, i: (b, jnp.maximum(i * nb - 1, 0), 0)),
        pl.BlockSpec((1, HALO, d), lambda b, i: (b, jnp.minimum((i + 1) * nb, last), 0)),
    ]


def _conv_layer(h, g, w_in, conv_w, w_out):
    bsz, s, d = h.shape
    tm = min(ROW_TILE, s)
    cb = 512
    kern = functools.partial(_conv_kernel, tm=tm, d=d, cb=cb)
    return pl.pallas_call(
        kern,
        grid=(bsz, s // tm),
        in_specs=_halo_specs(tm, s, d) + [
            _full((1, d)), _full((d, 3 * d)), _full((3, d)), _full((d, d))],
        out_specs=pl.BlockSpec((1, tm, d), lambda b, i: (b, i, 0)),
        out_shape=jax.ShapeDtypeStruct((bsz, s, d), F32),
        scratch_shapes=[pltpu.VMEM((tm + 2 * HALO, d), BF16),
                        pltpu.VMEM((tm + 2 * HALO, cb), F32),
                        pltpu.VMEM((tm, d), BF16)],
        compiler_params=_params("parallel", "parallel"),
        name="conv_mixer",
    )(h, h, h, g.reshape(1, d), w_in.astype(BF16), conv_w, w_out.astype(BF16))


def _kv_kernel(m_ref, g_ref, w_ref, o_ref):
    xn = _rms(m_ref[0], g_ref[...]).astype(BF16)
    o_ref[0] = _dot(xn, w_ref[...]).astype(BF16)


def _kv_proj(mem, g, w_kv):
    bsz, m, d = mem.shape
    return pl.pallas_call(
        _kv_kernel,
        grid=(bsz,),
        in_specs=[pl.BlockSpec((1, m, d), lambda b: (b, 0, 0)), _full((1, d)), _full((d, 2 * d))],
        out_specs=pl.BlockSpec((1, m, 2 * d), lambda b: (b, 0, 0)),
        out_shape=jax.ShapeDtypeStruct((bsz, m, 2 * d), BF16),
        compiler_params=_params("parallel"),
        name="kv_proj",
    )(mem, g.reshape(1, d), w_kv.astype(BF16))


def _xattn_kernel(h_ref, gx_ref, wq_ref, kv_ref, wo_ref, gf_ref, rhi_ref, rlo_ref,
                  h_out, xn_out, probs_out, o_s, *, d, heads):
    x = h_ref[0]
    xn = _rms(x, gx_ref[...]).astype(BF16)
    hd = d // heads
    q = (_dot(xn, wq_ref[...]) * (hd ** -0.5)).astype(BF16)
    kv = kv_ref[0]
    scs = [_dot_nt(q[:, a * hd:(a + 1) * hd], kv[:, a * hd:(a + 1) * hd]) for a in range(heads)]
    ps = [jnp.exp(sc - jnp.max(sc, axis=-1, keepdims=True)) for sc in scs]
    for a, p in enumerate(ps):
        l = jnp.sum(p, axis=-1, keepdims=True)
        o = _dot(p.astype(BF16), kv[:, d + a * hd:d + (a + 1) * hd]) / l
        o_s[:, a * hd:(a + 1) * hd] = o.astype(BF16)
    hn = x + _dot(o_s[...], wo_ref[...])
    h_out[0] = hn
    xf = _rms(hn, gf_ref[...])
    hi = xf.astype(BF16)
    xn_out[0] = hi
    lo = (xf - hi.astype(F32)).astype(BF16)
    lg = _dot_nt(rhi_ref[...], hi) + _dot_nt(rhi_ref[...], lo) + _dot_nt(rlo_ref[...], hi)
    e = jnp.exp(lg - jnp.max(lg, axis=0, keepdims=True))
    probs_out[0] = e / jnp.sum(e, axis=0, keepdims=True)


def _xattn_layer(h, kv, gx, w_q, w_o, gf, router):
    bsz, s, d = h.shape
    m = kv.shape[1]
    e = router.shape[1]
    tm = min(ROW_TILE, s)
    rt = router.T
    r_hi = rt.astype(BF16)
    r_lo = (rt - r_hi.astype(F32)).astype(BF16)
    kern = functools.partial(_xattn_kernel, d=d, heads=XATTN_HEADS)
    return pl.pallas_call(
        kern,
        grid=(bsz, s // tm),
        in_specs=[pl.BlockSpec((1, tm, d), lambda b, i: (b, i, 0)),
                  _full((1, d)), _full((d, d)),
                  pl.BlockSpec((1, m, 2 * d), lambda b, i: (b, 0, 0)),
                  _full((d, d)), _full((1, d)), _full((e, d)), _full((e, d))],
        out_specs=[pl.BlockSpec((1, tm, d), lambda b, i: (b, i, 0)),
                   pl.BlockSpec((1, tm, d), lambda b, i: (b, i, 0)),
                   pl.BlockSpec((1, e, tm), lambda b, i: (b, 0, i))],
        out_shape=[jax.ShapeDtypeStruct((bsz, s, d), F32),
                   jax.ShapeDtypeStruct((bsz, s, d), BF16),
                   jax.ShapeDtypeStruct((bsz, e, s), F32)],
        scratch_shapes=[pltpu.VMEM((tm, d), BF16)],
        compiler_params=_params("parallel", "parallel"),
        name="xattn_router",
    )(h, gx.reshape(1, d), w_q.astype(BF16), kv, w_o.astype(BF16), gf.reshape(1, d), r_hi, r_lo)


def _select_kernel(p_ref, code_ref, start_ref, *, cap, blk):
    p = p_ref[0]
    e, s = p.shape
    bits = pltpu.bitcast(p, I32)

    def count(mask):
        return jnp.sum(mask.astype(F32), axis=1, keepdims=True)

    def search(k, t):
        cand = jnp.bitwise_or(t, jnp.left_shift(jnp.int32(1), 30 - k))
        return jnp.where(count(bits >= cand) >= cap, cand, t)

    thr = lax.fori_loop(0, 31, search, jnp.zeros((e, 1), I32))
    gt = bits > thr
    eq = bits == thr
    need = cap - count(gt)

    r = lax.broadcasted_iota(I32, (blk, blk), 0)
    c = lax.broadcasted_iota(I32, (blk, blk), 1)
    tri = jnp.where(r < c, 1.0, 0.0).astype(BF16)

    def prefix(mask_f):
        out, carries = [], []
        carry = jnp.zeros((e, 1), F32)
        for j in range(0, s, blk):
            mb = mask_f[:, j:j + blk]
            carries.append(carry)
            out.append(_dot(mb.astype(BF16), tri) + carry)
            carry = carry + jnp.sum(mb, axis=1, keepdims=True)
        return out, carries + [carry]

    eq_f = eq.astype(F32)
    tie_rank, _ = prefix(eq_f)
    for j0, tr in zip(range(0, s, blk), tie_rank):
        sel_b = jnp.logical_or(gt[:, j0:j0 + blk],
                               jnp.logical_and(eq[:, j0:j0 + blk], tr < need))
        code_ref[0, :, j0:j0 + blk] = sel_b.astype(I32)
    sel_f = code_ref[0].astype(F32)
    rank, starts = prefix(sel_f)
    for j0, rk in zip(range(0, s, blk), rank):
        code_ref[0, :, j0:j0 + blk] = jnp.where(sel_f[:, j0:j0 + blk] > 0.0, rk.astype(I32), -1)
    start_ref[...] = jnp.zeros_like(start_ref)
    for j, st in enumerate(starts):
        start_ref[0, :, j:j + 1] = st.astype(I32)


def _select(probs_t, cap):
    bsz, e, s = probs_t.shape
    blk = min(GATHER_BLOCK, s)
    nb1 = s // blk + 1
    assert nb1 <= LANES
    kern = functools.partial(_select_kernel, cap=cap, blk=blk)
    code, starts = pl.pallas_call(
        kern,
        grid=(bsz,),
        in_specs=[pl.BlockSpec((1, e, s), lambda b: (b, 0, 0))],
        out_specs=[pl.BlockSpec((1, e, s), lambda b: (b, 0, 0)),
                   pl.BlockSpec((1, e, LANES), lambda b: (b, 0, 0))],
        out_shape=[jax.ShapeDtypeStruct((bsz, e, s), I32),
                   jax.ShapeDtypeStruct((bsz, e, LANES), I32)],
        compiler_params=_params("parallel"),
        name="expert_choice_select",
    )(probs_t)
    return code, starts[:, :, :nb1].reshape(-1)


def _window(s_lo, s_hi, win):
    s0 = jnp.bitwise_and(s_lo, -BF16_SUBLANES)
    n_win = jnp.where(s_hi > s_lo, lax.shift_right_logical(s_hi - s0 + (win - 1),
                                                           win.bit_length() - 1), 0)
    return s0, n_win


def _gather_kernel(st_ref, x_ref, code_ref, p_ref, xe_ref, aff_ref, xe_s, aff_s, *, cap, blk, win):
    b = pl.program_id(0)
    k = pl.program_id(1)
    s = x_ref.shape[1]
    nb = s // blk

    base = (b * pl.num_programs(1) + k) * (nb + 1)
    iota = lax.broadcasted_iota(I32, (win, blk), 0)

    def gather(j, row0):
        onehot = jnp.where(iota + row0 == code_ref[0, :, j * blk:(j + 1) * blk], 1.0, 0.0)
        part = _dot(onehot.astype(BF16), x_ref[0, j * blk:(j + 1) * blk, :])
        xe_s[pl.ds(row0, win), :] += part.astype(BF16)
        aff_s[pl.ds(row0, win), :] += jnp.sum(onehot * p_ref[0, :, j * blk:(j + 1) * blk],
                                              axis=1, keepdims=True)

    xe_s[...] = jnp.zeros_like(xe_s)
    aff_s[...] = jnp.zeros_like(aff_s)
    wins = [_window(st_ref[base + j], st_ref[base + j + 1], win) for j in range(nb)]
    for j, (s0, _) in enumerate(wins):
        gather(j, pl.multiple_of(s0, BF16_SUBLANES))
    for j, (s0, n_win) in enumerate(wins):
        def more(i, carry, j=j, s0=s0):
            gather(j, pl.multiple_of(s0 + i * win, BF16_SUBLANES))
            return carry
        lax.fori_loop(1, n_win, more, 0)

    xe_ref[0, 0] = xe_s[0:cap, :]
    aff_ref[0, 0] = jnp.broadcast_to(aff_s[0:cap, :], (cap, LANES))


def _expert_gather(xn, code, probs_t, starts, cap):
    bsz, s, d = xn.shape
    e = code.shape[1]
    blk = min(GATHER_BLOCK, s)
    win = min(GATHER_WINDOW, cap)
    kern = functools.partial(_gather_kernel, cap=cap, blk=blk, win=win)
    row = pl.BlockSpec((1, 1, s), lambda b, k, st: (b * e + k, 0, 0))
    return pl.pallas_call(
        kern,
        grid_spec=pltpu.PrefetchScalarGridSpec(
            num_scalar_prefetch=1,
            grid=(bsz, e),
            in_specs=[pl.BlockSpec((1, s, d), lambda b, k, st: (b, 0, 0)), row, row],
            out_specs=[pl.BlockSpec((1, 1, cap, d), lambda b, k, st: (b, k, 0, 0)),
                       pl.BlockSpec((1, 1, cap, LANES), lambda b, k, st: (b, k, 0, 0))],
            scratch_shapes=[pltpu.VMEM((cap + win, d), BF16), pltpu.VMEM((cap + win, 1), F32)]),
        out_shape=[jax.ShapeDtypeStruct((bsz, e, cap, d), BF16),
                   jax.ShapeDtypeStruct((bsz, e, cap, LANES), F32)],
        compiler_params=_params("parallel", "parallel"),
        name="expert_gather",
    )(starts, xn, code.reshape(bsz * e, 1, s), probs_t.reshape(bsz * e, 1, s))


def _ffn_kernel(xe_ref, aff_ref, wg0_ref, wu0_ref, wd0_ref, wgn_ref, wun_ref, wdn_ref, y_ref,
                wg_s, wu_s, wd_s):
    k = pl.program_id(0)
    b = pl.program_id(1)
    cur = lax.rem(k, 2)
    rows_in = wgn_ref.shape[3]
    rows_hid = wdn_ref.shape[3]

    @pl.when(jnp.logical_and(k == 0, b == 0))
    def _():
        wg_s[0] = wg0_ref[0, 0].astype(BF16)
        wu_s[0] = wu0_ref[0, 0].astype(BF16)
        wd_s[0] = wd0_ref[0, 0].astype(BF16)

    nxt = 1 - cur
    r_in = pl.ds(pl.multiple_of(b * rows_in, BF16_SUBLANES), rows_in)
    r_hid = pl.ds(pl.multiple_of(b * rows_hid, BF16_SUBLANES), rows_hid)
    wg_s[nxt, r_in, :] = wgn_ref[0, 0, 0].astype(BF16)
    wu_s[nxt, r_in, :] = wun_ref[0, 0, 0].astype(BF16)
    wd_s[nxt, r_hid, :] = wdn_ref[0, 0, 0].astype(BF16)

    xe = xe_ref[0, 0]
    fh = wg_s.shape[2] // 2
    halves = [(_dot(xe, wg_s[cur, :, c0:c0 + fh]), _dot(xe, wu_s[cur, :, c0:c0 + fh]))
              for c0 in (0, fh)]
    y = None
    for i, (gt, up) in enumerate(halves):
        hid = (gt * jax.nn.sigmoid(gt) * up).astype(BF16)
        part = _dot(hid, wd_s[cur, i * fh:(i + 1) * fh, :])
        y = part if y is None else y + part
    aff = aff_ref[0, 0]
    y_ref[0, 0] = (y * jnp.concatenate([aff] * (y.shape[1] // LANES), axis=1)).astype(BF16)


def _expert_ffn(xe, aff, w_gate, w_up, w_down, layer):
    bsz, e, cap, d = xe.shape
    depth, _, _, f = w_gate.shape
    assert d % (bsz * BF16_SUBLANES) == 0 and f % (bsz * BF16_SUBLANES) == 0
    first = dict(pipeline_mode=pl.Buffered(1))

    def whole(r, c):
        return pl.BlockSpec((1, 1, r, c), lambda k, b: (layer, 0, 0, 0), **first)

    def piece(r, c):
        return pl.BlockSpec((1, 1, 1, r // bsz, c),
                            lambda k, b: (layer, jnp.minimum(k + 1, e - 1), b, 0, 0))

    def pieces(wt):
        _, _, r, c = wt.shape
        return wt.reshape(depth, e, bsz, r // bsz, c)

    return pl.pallas_call(
        _ffn_kernel,
        grid=(e, bsz),
        in_specs=[pl.BlockSpec((1, 1, cap, d), lambda k, b: (b, k, 0, 0)),
                  pl.BlockSpec((1, 1, cap, LANES), lambda k, b: (b, k, 0, 0)),
                  whole(d, f), whole(d, f), whole(f, d),
                  piece(d, f), piece(d, f), piece(f, d)],
        out_specs=pl.BlockSpec((1, 1, cap, d), lambda k, b: (b, k, 0, 0)),
        out_shape=jax.ShapeDtypeStruct((bsz, e, cap, d), BF16),
        scratch_shapes=[pltpu.VMEM((2, d, f), BF16), pltpu.VMEM((2, d, f), BF16),
                        pltpu.VMEM((2, f, d), BF16)],
        compiler_params=_params("arbitrary", "arbitrary"),
        name="expert_ffn",
    )(xe, aff, w_gate, w_up, w_down, pieces(w_gate), pieces(w_up), pieces(w_down))


def _combine_kernel(st_ref, h_ref, code_ref, y_ref, g_ref, o_ref, acc_s,
                    *, cap, win, per, final_norm):
    b = pl.program_id(0)
    j = pl.program_id(1)
    nb = pl.num_programs(1)
    tb = h_ref.shape[1]
    n_exp = y_ref.shape[1]
    iota = lax.broadcasted_iota(I32, (tb, win), 1)
    code = code_ref[0]

    def window(k, lo):
        row0 = pl.multiple_of(jnp.minimum(lo, cap - win), BF16_SUBLANES)
        code_k = jnp.where(code[:, k:k + 1] >= lo, code[:, k:k + 1], -1)
        onehot = jnp.where(iota + row0 == code_k, 1.0, 0.0).astype(BF16)
        return onehot, y_ref[0, k, pl.ds(row0, win), :]

    wins = []
    for k in range(n_exp):
        base = (b * n_exp + k) * (nb * per + 1) + j * per
        wins.append(_window(st_ref[base], st_ref[base + per], win))
    acc = h_ref[0]
    for k in range(0, n_exp, 2):
        oh_a, y_a = window(k, wins[k][0])
        oh_b, y_b = window(k + 1, wins[k + 1][0])
        acc = acc + _dot(jnp.concatenate([oh_a, oh_b], axis=1),
                         jnp.concatenate([y_a, y_b], axis=0))
    acc_s[...] = acc
    for k, (s0, n_win) in enumerate(wins):
        def more(i, carry, k=k, s0=s0):
            acc_s[...] += _dot(*window(k, s0 + i * win))
            return carry
        lax.fori_loop(1, n_win, more, 0)
    acc = acc_s[...]
    o_ref[0] = _rms(acc, g_ref[...]) if final_norm else acc


def _moe_combine(h, code_t, starts, ye, g_final, final_norm):
    bsz, s, d = h.shape
    e, cap = ye.shape[1], ye.shape[2]
    assert e % 2 == 0
    tb = min(TOKEN_BLOCK, s)
    win = min(PICK_WINDOW, cap)
    per = tb // min(GATHER_BLOCK, s)
    kern = functools.partial(_combine_kernel, cap=cap, win=win, per=per, final_norm=final_norm)
    return pl.pallas_call(
        kern,
        grid_spec=pltpu.PrefetchScalarGridSpec(
            num_scalar_prefetch=1,
            grid=(bsz, s // tb),
            in_specs=[pl.BlockSpec((1, tb, d), lambda b, j, st: (b, j, 0)),
                      pl.BlockSpec((1, tb, e), lambda b, j, st: (b, j, 0)),
                      pl.BlockSpec((1, e, cap, d), lambda b, j, st: (b, 0, 0, 0)),
                      pl.BlockSpec((1, d), lambda b, j, st: (0, 0))],
            out_specs=pl.BlockSpec((1, tb, d), lambda b, j, st: (b, j, 0)),
            scratch_shapes=[pltpu.VMEM((tb, d), F32)]),
        out_shape=jax.ShapeDtypeStruct((bsz, s, d), F32),
        compiler_params=_params("parallel", "parallel"),
        name="moe_combine",
    )(starts, h, code_t, ye, g_final.reshape(1, d))


def _rwkv_prep_kernel(h_ref, hp_ref, hn_ref, g_ref, mu_ref, wrkv_ref, w1_ref, a1_ref, g1_ref,
                      w2_ref, a2_ref, g2_ref, w0_ref, a0_ref, kk_ref, ka_ref, rk_ref, hs_ref,
                      r_out, v_out, kn_out, gate_out, bonus_out, lw_out, kd_out, bd_out,
                      xn_s, *, tm):
    i = pl.program_id(1)
    g = g_ref[...]
    keep_prev = jnp.where(i == 0, 0.0, 1.0)
    keep_next = jnp.where(i == pl.num_programs(1) - 1, 0.0, 1.0)
    xn_s[0:HALO, :] = _rms(hp_ref[0], g) * keep_prev
    xn_s[HALO:HALO + tm, :] = _rms(h_ref[0], g)
    xn_s[HALO + tm:, :] = _rms(hn_ref[0], g) * keep_next
    xn = xn_s[pl.ds(HALO, tm), :]
    xx = 0.5 * (xn_s[pl.ds(HALO - 1, tm), :] + xn_s[pl.ds(HALO + 1, tm), :]) - xn

    def mix(j):
        return (xn + xx * mu_ref[j:j + 1, :]).astype(BF16)

    r = _dot(mix(0), wrkv_ref[0])
    k = _dot(mix(1), wrkv_ref[1])
    v = _dot(mix(2), wrkv_ref[2])
    hw = jnp.tanh(_dot(mix(3), w1_ref[...])).astype(BF16)
    ha = _dot(mix(4), a1_ref[...]).astype(BF16)
    hg = jax.nn.sigmoid(_dot(mix(5), g1_ref[...])).astype(BF16)
    gate_out[0] = _dot(hg, g2_ref[...]).astype(BF16)

    hsum = hs_ref[...]
    kk = k * kk_ref[...]
    nrm2 = _head_sum(kk * kk, hsum, two_term=False)
    kn = kk * lax.rsqrt(jnp.maximum(nrm2, 1e-24))
    r_out[0] = r.astype(BF16)
    v_out[0] = v.astype(BF16)
    kn_out[0] = kn.astype(BF16)
    ksum = None
    for n in range(2):
        w_raw = w0_ref[n:n + 1, :] + _dot(hw, w2_ref[n])
        lw_out[n, 0] = -DECAY_SCALE * jax.nn.sigmoid(w_raw)
        a = jax.nn.sigmoid(a0_ref[n:n + 1, :] + _dot(ha, a2_ref[n]))
        kd = k * (1.0 + (a - 1.0) * ka_ref[...])
        kd_out[n, 0] = kd.astype(BF16)
        bd_out[n, 0] = (kn * a).astype(BF16)
        ksum = kd if ksum is None else ksum + kd
    coef = _head_sum(r * ksum * rk_ref[...], hsum, two_term=False)
    bonus_out[0] = (coef * v).astype(BF16)


def _head_sum_matrix(d, n):
    assert d // n <= LANES
    return (jnp.arange(d)[:, None] // n == jnp.arange(LANES)[None, :]).astype(BF16)


def _head_collect(x, hs, two_term):
    x_hi = x.astype(BF16)
    sums = _dot(x_hi, hs)
    if two_term:
        sums = sums + _dot((x - x_hi.astype(F32)).astype(BF16), hs)
    return sums


def _head_spread(sums, hs, two_term):
    s_hi = sums.astype(BF16)
    out = _dot_nt(s_hi, hs)
    if two_term:
        out = out + _dot_nt((sums - s_hi.astype(F32)).astype(BF16), hs)
    return out


def _head_sum(x, hs, two_term):
    return _head_spread(_head_collect(x, hs, two_term), hs, two_term)


def _rwkv_prep(h, g, mu, w_rkv, w0, w1, w2, a0, a1, a2, g1, g2, k_k, k_a, r_k):
    bsz, s, d = h.shape
    tm = min(RWKV_ROW_TILE, s)
    lora = w1.shape[-1]
    glora = g1.shape[-1]
    w1c = jnp.concatenate([w1[0], w1[1]], axis=1).astype(BF16)
    a1c = jnp.concatenate([a1[0], a1[1]], axis=1).astype(BF16)
    keep = (jnp.arange(2 * lora)[None, :, None] // lora) == jnp.arange(2)[:, None, None]
    w2p = jnp.where(keep, jnp.concatenate([w2, w2], axis=1), 0.0).astype(BF16)
    a2p = jnp.where(keep, jnp.concatenate([a2, a2], axis=1), 0.0).astype(BF16)
    kern = functools.partial(_rwkv_prep_kernel, tm=tm)
    tok = pl.BlockSpec((1, tm, d), lambda b, i: (b, i, 0))
    tok2 = pl.BlockSpec((2, 1, tm, d), lambda b, i: (0, b, i, 0))
    sd = jax.ShapeDtypeStruct
    return pl.pallas_call(
        kern,
        grid=(bsz, s // tm),
        in_specs=_halo_specs(tm, s, d) + [
            _full((1, d)), _full((6, d)), _full((3, d, d)),
            _full((d, 2 * lora)), _full((d, 2 * lora)), _full((d, glora)),
            _full((2, 2 * lora, d)), _full((2, 2 * lora, d)), _full((glora, d)),
            _full((2, d)), _full((2, d)), _full((1, d)), _full((1, d)), _full((1, d)),
            _full((d, LANES))],
        out_specs=[tok, tok, tok, tok, tok, tok2, tok2, tok2],
        out_shape=[sd((bsz, s, d), BF16)] * 5 + [sd((2, bsz, s, d), F32),
                                                 sd((2, bsz, s, d), BF16),
                                                 sd((2, bsz, s, d), BF16)],
        scratch_shapes=[pltpu.VMEM((tm + 2 * HALO, d), F32)],
        compiler_params=_params("parallel", "parallel"),
        name="rwkv_prep",
    )(h, h, h, g.reshape(1, d), mu, w_rkv.astype(BF16), w1c, a1c, g1.astype(BF16),
      w2p, a2p, g2.astype(BF16), w0, a0,
      k_k.reshape(1, d), k_a.reshape(1, d), r_k.reshape(1, d),
      _head_sum_matrix(d, RWKV_HEAD_DIM))


def _wkv_kernel(rf_ref, vf_ref, knf_ref, rr_ref, vr_ref, knr_ref,
                lwf_ref, kdf_ref, bdf_ref, lwr_ref, kdr_ref, bdr_ref,
                yf_ref, yr_ref, q_s, rh_s, y0_s, nm_s, q0_s, gt_s, *, c, ng, npair):
    hd = RWKV_HEAD_DIM
    w = 2 * hd
    j = pl.program_id(2)
    n_steps = pl.num_programs(2)
    carried = (rh_s, y0_s, nm_s, q0_s, gt_s)

    @pl.when(j == 0)
    def _():
        for ref in (q_s,) + carried:
            ref[...] = jnp.zeros_like(ref)

    row_c = lax.broadcasted_iota(I32, (c, 1), 0)
    lane = lax.broadcasted_iota(I32, (1, w), 1)
    m_lo = lane < hd
    ti2 = lax.broadcasted_iota(I32, (c, 2 * c), 0)
    si2 = lax.broadcasted_iota(I32, (c, 2 * c), 1)
    si2 = jnp.where(si2 >= c, si2 - c, si2)
    eye2 = jnp.where(si2 == ti2, 1.0, 0.0)
    dir_masks = ((si2 <= ti2, si2 < ti2), (si2 >= ti2, si2 > ti2))
    rr = lax.broadcasted_iota(I32, (2 * c, 2 * c), 0)
    cc = lax.broadcasted_iota(I32, (2 * c, 2 * c), 1)
    bd_mask_c = jnp.where(rr < c, 0, 1) == jnp.where(cc < c, 0, 1)
    rr = lax.broadcasted_iota(I32, (w, w), 0)
    cc = lax.broadcasted_iota(I32, (w, w), 1)
    bd_mask_h = jnp.where(rr < hd, 0, 1) == jnp.where(cc < hd, 0, 1)

    def row_stack(x):
        return jnp.concatenate([jnp.where(m_lo, x, 0.0), jnp.where(m_lo, 0.0, x)],
                               axis=0).astype(BF16)

    def block_diag(xp):
        return jnp.where(bd_mask_c, jnp.concatenate([xp, xp], axis=0), 0.0).astype(BF16)

    fwd = (0, rf_ref, vf_ref, knf_ref, lwf_ref, kdf_ref, bdf_ref)
    rev = (1, rr_ref, vr_ref, knr_ref, lwr_ref, kdr_ref, bdr_ref)
    y_refs = (yf_ref, yr_ref)
    chains = [(dirn, pi) for dirn in (0, 1) for pi in range(npair)]
    q = [q_s[ch] for ch in range(len(chains))]

    def stages(probs, st):
        for (dirn, r_ref, v_ref, kn_ref, lw_ref, kd_ref, bd_ref), pi, ci in probs:
            sl = (pl.ds(ci * c, c), pl.ds(pi * w, w))
            lw = lw_ref[(0, 0) + sl]
            l_incl = lw
            step = 1
            while step < c:
                if dirn:
                    l_incl = l_incl + jnp.where(row_c < c - step,
                                                pltpu.roll(l_incl, c - step, axis=0), 0.0)
                else:
                    l_incl = l_incl + jnp.where(row_c >= step,
                                                pltpu.roll(l_incl, step, axis=0), 0.0)
                step *= 2
            st.append(dict(dirn=dirn, lw=lw, l_incl=l_incl,
                           r=r_ref[(0,) + sl].astype(F32), v=v_ref[(0,) + sl].astype(F32),
                           kn=kn_ref[(0,) + sl].astype(F32), kd=kd_ref[(0, 0) + sl].astype(F32),
                           bd=bd_ref[(0, 0) + sl].astype(F32)))
        yield
        for p in st:
            stage_decay(p)
        yield
        for p in st:
            stage_masks(p)
        yield
        for _ in range(max(c.bit_length() - 3, 0)):
            for p in st:
                res = _dot(jnp.concatenate([p["pw"], p["t_p"]], axis=0).astype(BF16),
                           block_diag(p["pw"]))
                p["pw"] = res[0:c]
                p["t_p"] = p["t_p"] + res[c:2 * c]
            yield
        for p in st:
            stage_solve(p)
        yield
        for p in st:
            stage_maps(p)
        yield

    def handover(off_f, off_r):
        for i in range(ng):
            for ch, (dirn, pi) in enumerate(chains):
                slot = ch * ng + i
                row = off_r + (ng - 1 - i) * c if dirn else off_f + i * c
                q_b = q[ch].astype(BF16)
                y_refs[dirn][0, pl.ds(pl.multiple_of(row, c), c), pl.ds(pi * w, w)] = (
                    _dot_nt(rh_s[slot], q_b) + y0_s[slot])
                q[ch] = q[ch] * gt_s[slot] + _dot(q_b, nm_s[slot]) + q0_s[slot]
            yield

    def stage_decay(p):
        l_incl = p["l_incl"]
        l_tot = l_incl[0:1, :] if p["dirn"] else l_incl[c - 1:c, :]
        g_inv = jnp.exp(-l_incl)
        g_end = jnp.exp(l_tot - l_incl)
        p["g_tot"] = jnp.exp(l_tot)
        p["rt"] = p["r"] * jnp.exp(l_incl)
        p["at"] = -p["kn"] * jnp.exp(l_incl - p["lw"])
        p["v_rs"] = row_stack(p["v"])
        p["bh"] = (p["bd"] * g_end).astype(BF16)
        p["kh"] = (p["kd"] * g_end).astype(BF16)
        lhs = jnp.concatenate([p["at"], p["rt"]], axis=0).astype(BF16)
        rhs = jnp.concatenate([row_stack(p["bd"] * g_inv), row_stack(p["kd"] * g_inv)],
                              axis=0)
        p["gm"] = _dot_nt(lhs, rhs)

    def stage_masks(p):
        incl2, strict2 = dir_masks[p["dirn"]]
        gm = p.pop("gm")
        a_ab = jnp.where(strict2, gm[0:c, 0:2 * c], 0.0)
        a_ak = jnp.where(strict2, gm[0:c, 2 * c:4 * c], 0.0)
        a_rb = jnp.where(incl2, gm[c:2 * c, 0:2 * c], 0.0)
        a_rk = jnp.where(incl2, gm[c:2 * c, 2 * c:4 * c], 0.0)
        p["a_rb"] = a_rb.astype(BF16)
        av = _dot(jnp.concatenate([a_ak, a_rk], axis=0).astype(BF16), p["v_rs"])
        p["akv"] = av[0:c]
        p["arkv"] = av[c:2 * c]
        p["t_p"] = eye2 + a_ab
        p["pw"] = _dot(a_ab.astype(BF16), block_diag(a_ab))

    def stage_solve(p):
        t_p = p["t_p"] + _dot(p["t_p"].astype(BF16), block_diag(p["pw"]))
        wu = _dot(t_p.astype(BF16),
                  jnp.concatenate([row_stack(p["at"]), row_stack(p["akv"])], axis=1))
        p["w_m"] = wu[:, 0:w]
        p["u_t"] = wu[:, w:2 * w]

    def stage_maps(p):
        ry = _dot(p["a_rb"],
                  jnp.concatenate([row_stack(p["w_m"]), row_stack(p["u_t"])], axis=1))
        p["r_hat"] = (p["rt"] + ry[:, 0:w]).astype(BF16)
        p["y0"] = ry[:, w:2 * w] + p["arkv"]
        p["n_m"] = jnp.where(bd_mask_h, _dot_tn(p["w_m"].astype(BF16), p["bh"]),
                             0.0).astype(BF16)
        p["q0"] = jnp.where(
            bd_mask_h,
            _dot_tn(jnp.concatenate([p["u_t"], p["v"]], axis=0).astype(BF16),
                    jnp.concatenate([p["bh"], p["kh"]], axis=0)), 0.0)

    cg = ng * c
    probs = [((rev if dirn else fwd), pi, (ng - 1 - i if dirn else i))
             for dirn, pi in chains for i in range(ng)]
    st = []
    pending = handover(jnp.maximum(j - 1, 0) * cg, jnp.minimum(n_steps - j, n_steps - 1) * cg)
    for _ in stages(probs, st):
        next(pending, None)
    for _ in pending:
        pass
    for slot, p in enumerate(st):
        rh_s[slot] = p["r_hat"]
        y0_s[slot] = p["y0"]
        nm_s[slot] = p["n_m"]
        q0_s[slot] = p["q0"]
        gt_s[slot] = p["g_tot"]
    for ch in range(len(chains)):
        q_s[ch] = q[ch]

    @pl.when(j == n_steps - 1)
    def _():
        for ch in range(len(chains)):
            q[ch] = q_s[ch]
        for _ in handover(j * cg, 0):
            pass


def _wkv(r, v, kn, lw, kd, bd):
    bsz, s, d = r.shape
    c = min(CHUNK, s)
    ng = min(CHUNK_GROUP, s // c)
    cg = c * ng
    n_steps = s // cg
    w = 2 * RWKV_HEAD_DIM
    npair = WKV_PAIRS
    wb = w * npair
    assert d % wb == 0
    tok_f = pl.BlockSpec((1, cg, wb), lambda b, p, j: (b, j, p))
    tok_r = pl.BlockSpec((1, cg, wb), lambda b, p, j: (b, n_steps - 1 - j, p))
    dir_f = pl.BlockSpec((1, 1, cg, wb), lambda b, p, j: (0, b, j, p))
    dir_r = pl.BlockSpec((1, 1, cg, wb), lambda b, p, j: (1, b, n_steps - 1 - j, p))
    kern = functools.partial(_wkv_kernel, c=c, ng=ng, npair=npair)
    seq = pl.BlockSpec((1, s, wb), lambda b, p, j: (b, 0, p))
    nch = 2 * npair
    return pl.pallas_call(
        kern,
        grid=(bsz, d // wb, n_steps),
        in_specs=[tok_f, tok_f, tok_f, tok_r, tok_r, tok_r,
                  dir_f, dir_f, dir_f, dir_r, dir_r, dir_r],
        out_specs=[seq, seq],
        out_shape=[jax.ShapeDtypeStruct((bsz, s, d), F32)] * 2,
        scratch_shapes=[pltpu.VMEM((nch, w, w), F32),
                        pltpu.VMEM((nch * ng, c, w), BF16), pltpu.VMEM((nch * ng, c, w), F32),
                        pltpu.VMEM((nch * ng, w, w), BF16), pltpu.VMEM((nch * ng, w, w), F32),
                        pltpu.VMEM((nch * ng, 1, w), F32)],
        compiler_params=_params("parallel", "parallel", "arbitrary"),
        name="wkv",
    )(r, v, kn, r, v, kn, lw, kd, bd, lw, kd, bd)


def _rwkv_post_kernel(h_ref, yf_ref, yr_ref, bonus_ref, gate_ref, lnw_ref, lnb_ref, hs_ref,
                      wout_ref, o_ref, z_s, *, n):
    hsum = hs_ref[...]
    tm = h_ref.shape[1]
    rows = [pl.ds(i * (tm // POST_ROW_GROUPS), tm // POST_ROW_GROUPS)
            for i in range(POST_ROW_GROUPS)]
    ys = [yf_ref[0, r, :] + yr_ref[0, r, :] for r in rows]
    sums = [_head_collect(y, hsum, two_term=True) for y in ys]
    ycs = [y - _head_spread(sm, hsum, two_term=True) * (1.0 / n) for y, sm in zip(ys, sums)]
    sums = [_head_collect(yc * yc, hsum, two_term=False) for yc in ycs]
    for r, yc, sm in zip(rows, ycs, sums):
        var = _head_spread(sm, hsum, two_term=False) * (1.0 / n)
        yn = yc * lax.rsqrt(var + GN_EPS) * lnw_ref[...] + lnb_ref[...]
        yn = yn + bonus_ref[0, r, :].astype(F32)
        z_s[r, :] = (yn * gate_ref[0, r, :].astype(F32)).astype(BF16)
    o_ref[0] = h_ref[0] + _dot(z_s[...], wout_ref[...])


def _rwkv_post(h, yf, yr, bonus, gate, ln_w, ln_b, w_out):
    bsz, s, d = h.shape
    tm = min(ROW_TILE, s)
    tok = pl.BlockSpec((1, tm, d), lambda b, i: (b, i, 0))
    kern = functools.partial(_rwkv_post_kernel, n=RWKV_HEAD_DIM)
    return pl.pallas_call(
        kern,
        grid=(bsz, s // tm),
        in_specs=[tok, tok, tok, tok, tok, _full((1, d)), _full((1, d)), _full((d, LANES)),
                  _full((d, d))],
        out_specs=tok,
        out_shape=jax.ShapeDtypeStruct((bsz, s, d), F32),
        scratch_shapes=[pltpu.VMEM((tm, d), BF16)],
        compiler_params=_params("parallel", "parallel"),
        name="rwkv_post",
    )(h, yf, yr, bonus, gate, ln_w.reshape(1, d), ln_b.reshape(1, d),
      _head_sum_matrix(d, RWKV_HEAD_DIM), w_out.astype(BF16))


def kernel(x, mem, norm_mix, norm_xattn, norm_mem, norm_ffn, norm_final,
           conv_w_in, conv_w, conv_w_out,
           rwkv_mu, rwkv_w_rkv, rwkv_w0, rwkv_w1, rwkv_w2, rwkv_a0, rwkv_a1, rwkv_a2,
           rwkv_g1, rwkv_g2, rwkv_k_k, rwkv_k_a, rwkv_r_k, rwkv_ln_w, rwkv_ln_b, rwkv_w_out,
           xattn_w_q, xattn_w_kv, xattn_w_o,
           moe_router, moe_w_gate, moe_w_up, moe_w_down):
    depth = norm_mix.shape[0]
    bsz, s, d = x.shape
    n_exp = moe_router.shape[-1]
    cap = CAPACITY_FACTOR * s // n_exp
    h = x
    for i in range(depth):
        j = i // N_MIXERS
        if i % N_MIXERS == 0:
            h = _conv_layer(h, norm_mix[i], conv_w_in[j], conv_w[j], conv_w_out[j])
        else:
            r, v, kn, gate, bonus, lw, kd, bd = _rwkv_prep(
                h, norm_mix[i], rwkv_mu[j], rwkv_w_rkv[j], rwkv_w0[j], rwkv_w1[j], rwkv_w2[j],
                rwkv_a0[j], rwkv_a1[j], rwkv_a2[j], rwkv_g1[j], rwkv_g2[j],
                rwkv_k_k[j], rwkv_k_a[j], rwkv_r_k[j])
            yf, yr = _wkv(r, v, kn, lw, kd, bd)
            h = _rwkv_post(h, yf, yr, bonus, gate, rwkv_ln_w[j], rwkv_ln_b[j], rwkv_w_out[j])
        kv = _kv_proj(mem, norm_mem[i], xattn_w_kv[i])
        h, xn, probs_t = _xattn_layer(h, kv, norm_xattn[i], xattn_w_q[i], xattn_w_o[i],
                                      norm_ffn[i], moe_router[i])
        code, starts = _select(probs_t, cap)
        xe, aff = _expert_gather(xn, code, probs_t, starts, cap)
        ye = _expert_ffn(xe, aff, moe_w_gate, moe_w_up, moe_w_down, i)
        h = _moe_combine(h, jnp.swapaxes(code, 1, 2), starts, ye, norm_final,
                         final_norm=(i == depth - 1))
    return h
```
